```python
import math, functools
import jax, jax.numpy as jnp
from jax import lax
import numpy as np

D_MODEL = 2048
BATCH = 4
SEQ = 2048
DEPTH = 4
DEC_BATCH = 128
DEC_SEQ = 1
PAST_LEN = 16384
PAGE_SIZE = 128

EPS = 1e-6
MIX_WIDTH = D_MODEL
GLA_WIDTH = MIX_WIDTH // 2
GDN_WIDTH = MIX_WIDTH - GLA_WIDTH
GLA_HEADS = 4
GLA_DV = GLA_WIDTH // GLA_HEADS
GLA_DK = GLA_DV // 2
GLA_KW = GLA_HEADS * GLA_DK
GLA_GATE_RANK = 16
GLA_GATE_TEMP = 16.0
GDN_HEADS = 8
GDN_DV = GDN_WIDTH // GDN_HEADS
GDN_DK = GDN_DV
GDN_KW = GDN_HEADS * GDN_DK
GDN_VW = GDN_HEADS * GDN_DV
GDN_CONV_DIM = 2 * GDN_KW + GDN_VW
CONV_W = 4
CHUNK = 64
N_MEM = 256
XA_HEADS = 4
XA_HD = D_MODEL // XA_HEADS
D_FF = ((8 * D_MODEL + 767) // 768) * 256
IN_SIZES = (GLA_KW, GLA_KW, GLA_WIDTH, GLA_WIDTH, GLA_GATE_RANK, GDN_CONV_DIM, GDN_VW, GDN_HEADS, GDN_HEADS)
IN_COLS = sum(IN_SIZES)

kernel_name = "hybrid_gla_gdn_memory_decoder"


def _split_points(sizes):
    return [int(s) for s in np.cumsum(sizes)[:-1]]


def rmsnorm(x, g):
    xf = x.astype(jnp.float32)
    y = xf * lax.rsqrt(jnp.mean(xf * xf, axis=-1, keepdims=True) + EPS)
    return (y * g.astype(jnp.float32)).astype(x.dtype)


def l2norm(x):
    return x * lax.rsqrt(jnp.sum(x * x, axis=-1, keepdims=True) + EPS)


def gla_chunked(q, k, v, log_a, s0):
    B, H, T, DK = q.shape
    DV = v.shape[-1]
    n = T // CHUNK
    to_c = lambda t: jnp.moveaxis(t.reshape(B, H, n, CHUNK, t.shape[-1]), 2, 0)
    idx = jnp.arange(CHUNK)
    incl = (idx[:, None] >= idx[None, :])[:, :, None]

    def step(s, inp):
        qi, ki, vi, ai = inp
        b = jnp.cumsum(ai, axis=2)
        decay = jnp.exp(jnp.where(incl, b[:, :, :, None, :] - b[:, :, None, :, :], -jnp.inf))
        scores = jnp.einsum('bhtd,bhsd,bhtsd->bhts', qi, ki, decay)
        o = jnp.einsum('bhtd,bhde->bhte', qi * jnp.exp(b), s) + jnp.einsum('bhts,bhse->bhte', scores, vi)
        b_last = b[:, :, -1:, :]
        s = jnp.exp(b_last[:, :, 0, :])[..., None] * s + jnp.einsum('bhsd,bhse->bhde', ki * jnp.exp(b_last - b), vi)
        return s, o

    s_fin, oc = lax.scan(step, s0, (to_c(q), to_c(k), to_c(v), to_c(log_a)))
    return jnp.moveaxis(oc, 0, 2).reshape(B, H, T, DV), s_fin


def gla_recurrent(q, k, v, log_a, s0):
    def step(s, inp):
        qt, kt, vt, at = inp
        s = jnp.exp(at)[..., None] * s + kt[..., None] * vt[..., None, :]
        return s, jnp.einsum('bhd,bhde->bhe', qt, s)
    xs = tuple(jnp.moveaxis(t, 2, 0) for t in (q, k, v, log_a))
    s_fin, o = lax.scan(step, s0, xs)
    return jnp.moveaxis(o, 0, 2), s_fin


def gdn_chunked(q, k, v, g, beta, s0):
    B, H, T, DK = q.shape
    DV = v.shape[-1]
    n = T // CHUNK
    qc = q.reshape(B, H, n, CHUNK, DK)
    kc = k.reshape(B, H, n, CHUNK, DK)
    vc = v.reshape(B, H, n, CHUNK, DV)
    gc = g.reshape(B, H, n, CHUNK)
    bc = beta.reshape(B, H, n, CHUNK)
    G = jnp.cumsum(gc, axis=-1)
    idx = jnp.arange(CHUNK)
    incl = idx[:, None] >= idx[None, :]
    strict = idx[:, None] > idx[None, :]
    decay = jnp.exp(jnp.where(incl, G[..., :, None] - G[..., None, :], -jnp.inf))
    kb = kc * bc[..., None]
    lower = jnp.where(strict, jnp.einsum('bhntd,bhnsd->bhnts', kb, kc) * decay, 0.0)
    a_mat = lower + jnp.eye(CHUNK, dtype=lower.dtype)
    rhs = jnp.concatenate([vc * bc[..., None], kb * jnp.exp(G)[..., None]], axis=-1)
    sol = lax.linalg.triangular_solve(a_mat, rhs, left_side=True, lower=True, unit_diagonal=True)
    u_base, w = sol[..., :DV], sol[..., DV:]
    qk = jnp.einsum('bhntd,bhnsd->bhnts', qc, kc) * decay
    q_dec = qc * jnp.exp(G)[..., None]
    k_dec = kc * jnp.exp(G[..., -1:] - G)[..., None]
    g_last = jnp.exp(G[..., -1])
    xs = tuple(jnp.moveaxis(t, 2, 0) for t in (u_base, w, qk, q_dec, k_dec, g_last))

    def step(s, inp):
        u_b, w_i, qk_i, qd_i, kd_i, gl_i = inp
        u = u_b - jnp.einsum('bhtd,bhde->bhte', w_i, s)
        o = jnp.einsum('bhtd,bhde->bhte', qd_i, s) + jnp.einsum('bhts,bhse->bhte', qk_i, u)
        s = gl_i[..., None, None] * s + jnp.einsum('bhsd,bhse->bhde', kd_i, u)
        return s, o

    s_fin, oc = lax.scan(step, s0, xs)
    return jnp.moveaxis(oc, 0, 2).reshape(B, H, T, DV), s_fin


def gdn_recurrent(q, k, v, g, beta, s0):
    def step(s, inp):
        qt, kt, vt, gt, bt = inp
        s = jnp.exp(gt)[..., None, None] * s
        u = bt[..., None] * (vt - jnp.einsum('bhd,bhde->bhe', kt, s))
        s = s + kt[..., None] * u[..., None, :]
        return s, jnp.einsum('bhd,bhde->bhe', qt, s)
    xs = tuple(jnp.moveaxis(t, 2, 0) for t in (q, k, v, g, beta))
    s_fin, o = lax.scan(step, s0, xs)
    return jnp.moveaxis(o, 0, 2), s_fin


def memory_kv(mem, g_mem, w_k, w_v):
    B = mem.shape[0]
    m = rmsnorm(mem, g_mem)
    k = jnp.einsum('bmd,de->bme', m, w_k).reshape(B, N_MEM, XA_HEADS, XA_HD)
    v = jnp.einsum('bmd,de->bme', m, w_v).reshape(B, N_MEM, XA_HEADS, XA_HD)
    return k, v


def cross_attention(h, mem_k, mem_v, w_q, w_o):
    B, T, _ = h.shape
    q = jnp.einsum('btd,de->bte', h, w_q).reshape(B, T, XA_HEADS, XA_HD)
    s = jnp.einsum('bthd,bmhd->bhtm', q, mem_k.astype(h.dtype)).astype(jnp.float32) * (XA_HD ** -0.5)
    p = jax.nn.softmax(s, axis=-1).astype(h.dtype)
    o = jnp.einsum('bhtm,bmhd->bthd', p, mem_v.astype(h.dtype)).reshape(B, T, D_MODEL)
    return jnp.einsum('btd,de->bte', o, w_o)


def block(x, mem_k, mem_v, conv_buf, s_gla, s_gdn, chunked,
          g_mix, w_in, gla_w_gate2, gla_b_gate, gla_g_out, gdn_conv_w, gdn_a_log,
          gdn_dt_bias, gdn_g_out, w_out, g_xattn, xa_w_q, xa_w_o,
          g_ffn, ffn_w_gate, ffn_w_up, ffn_w_down):
    B, T, _ = x.shape
    f32 = jnp.float32
    heads = lambda t, nh, d: t.reshape(B, T, nh, d).transpose(0, 2, 1, 3).astype(f32)

    h = rmsnorm(x, g_mix)
    proj = jnp.einsum('btd,de->bte', h, w_in)
    gq, gk, gv, gr, glr, cqkv, dz, da, db = jnp.split(proj, _split_points(IN_SIZES), axis=-1)

    q_a = heads(gq, GLA_HEADS, GLA_DK) * (GLA_DK ** -0.5)
    k_a = heads(gk, GLA_HEADS, GLA_DK)
    v_a = heads(gv, GLA_HEADS, GLA_DV)
    gate_pre = jnp.einsum('btr,rk->btk', glr, gla_w_gate2) + gla_b_gate
    la = heads(jax.nn.log_sigmoid(gate_pre.astype(f32)) / GLA_GATE_TEMP, GLA_HEADS, GLA_DK)

    xc = jnp.concatenate([conv_buf.astype(cqkv.dtype), cqkv], axis=1)
    conv = jax.nn.silu(sum(xc[:, j:j + T] * gdn_conv_w[j] for j in range(CONV_W)))
    new_buf = xc[:, T:]
    cq, ck, cv = jnp.split(conv, [GDN_KW, 2 * GDN_KW], axis=-1)
    q_b = l2norm(heads(cq, GDN_HEADS, GDN_DK)) * (GDN_DK ** -0.5)
    k_b = l2norm(heads(ck, GDN_HEADS, GDN_DK))
    v_b = heads(cv, GDN_HEADS, GDN_DV)
    beta = jax.nn.sigmoid(db.astype(f32)).transpose(0, 2, 1)
    g = (-jnp.exp(gdn_a_log.astype(f32)) * jax.nn.softplus(da.astype(f32) + gdn_dt_bias.astype(f32))).transpose(0, 2, 1)

    s_gla = s_gla.astype(f32)
    s_gdn = s_gdn.astype(f32)
    if chunked:
        o_a, s_gla = gla_chunked(q_a, k_a, v_a, la, s_gla)
        o_b, s_gdn = gdn_chunked(q_b, k_b, v_b, g, beta, s_gdn)
    else:
        o_a, s_gla = gla_recurrent(q_a, k_a, v_a, la, s_gla)
        o_b, s_gdn = gdn_recurrent(q_b, k_b, v_b, g, beta, s_gdn)

    o_a = rmsnorm(o_a.transpose(0, 2, 1, 3), gla_g_out) * jax.nn.silu(gr.astype(f32).reshape(B, T, GLA_HEADS, GLA_DV))
    o_b = rmsnorm(o_b.transpose(0, 2, 1, 3), gdn_g_out) * jax.nn.silu(dz.astype(f32).reshape(B, T, GDN_HEADS, GDN_DV))
    o = jnp.concatenate([o_a.reshape(B, T, GLA_WIDTH), o_b.reshape(B, T, GDN_WIDTH)], axis=-1).astype(x.dtype)
    x = x + jnp.einsum('bte,ed->btd', o, w_out)

    x = x + cross_attention(rmsnorm(x, g_xattn), mem_k, mem_v, xa_w_q, xa_w_o)

    h = rmsnorm(x, g_ffn)
    ff = jax.nn.silu(jnp.einsum('btd,df->btf', h, ffn_w_gate)) * jnp.einsum('btd,df->btf', h, ffn_w_up)
    x = x + jnp.einsum('btf,fd->btd', ff, ffn_w_down)
    return x, s_gla, s_gdn, new_buf


def setup_inputs(seed: int = 0) -> dict:
    key = jax.random.key(seed)
    ks = jax.random.split(key, 32)
    f32 = jnp.float32
    nrm = lambda k, shape, scale: jax.random.normal(k, shape, f32) * scale
    gain = lambda k, shape: 1.0 + 0.02 * jax.random.normal(k, shape, f32)
    return {
        "x_prompt": nrm(ks[0], (BATCH, SEQ, D_MODEL), 1.0),
        "x_sample": nrm(ks[1], (DEC_BATCH, DEC_SEQ, D_MODEL), 1.0),
        "mem_prompt": nrm(ks[2], (BATCH, N_MEM, D_MODEL), 1.0),
        "cache_mem_k": nrm(ks[3], (DEPTH, DEC_BATCH, N_MEM, XA_HEADS, XA_HD), 1.0),
        "cache_mem_v": nrm(ks[4], (DEPTH, DEC_BATCH, N_MEM, XA_HEADS, XA_HD), 1.0),
        "state_gla": nrm(ks[5], (DEPTH, DEC_BATCH, GLA_HEADS, GLA_DK, GLA_DV), 0.1),
        "state_gdn": nrm(ks[6], (DEPTH, DEC_BATCH, GDN_HEADS, GDN_DK, GDN_DV), 0.1),
        "state_conv": nrm(ks[7], (DEPTH, DEC_BATCH, CONV_W - 1, GDN_CONV_DIM), 1.0),
        "g_mix": gain(ks[8], (DEPTH, D_MODEL)),
        "w_in": nrm(ks[9], (DEPTH, D_MODEL, IN_COLS), D_MODEL ** -0.5),
        "gla_w_gate2": nrm(ks[10], (DEPTH, GLA_GATE_RANK, GLA_KW), GLA_GATE_RANK ** -0.5),
        "gla_b_gate": nrm(ks[11], (DEPTH, GLA_KW), 0.01),
        "gla_g_out": gain(ks[12], (DEPTH, GLA_DV)),
        "gdn_conv_w": nrm(ks[13], (DEPTH, CONV_W, GDN_CONV_DIM), CONV_W ** -0.5),
        "gdn_a_log": jnp.log(jax.random.uniform(ks[14], (DEPTH, GDN_HEADS), f32, 1.0, 16.0)),
        "gdn_dt_bias": 1.0 + nrm(ks[15], (DEPTH, GDN_HEADS), 0.01),
        "gdn_g_out": gain(ks[16], (DEPTH, GDN_DV)),
        "w_out": nrm(ks[17], (DEPTH, MIX_WIDTH, D_MODEL), MIX_WIDTH ** -0.5),
        "g_xattn": gain(ks[18], (DEPTH, D_MODEL)),
        "g_mem": gain(ks[19], (DEPTH, D_MODEL)),
        "xa_w_q": nrm(ks[20], (DEPTH, D_MODEL, D_MODEL), D_MODEL ** -0.5),
        "xa_w_k": nrm(ks[21], (DEPTH, D_MODEL, D_MODEL), D_MODEL ** -0.5),
        "xa_w_v": nrm(ks[22], (DEPTH, D_MODEL, D_MODEL), D_MODEL ** -0.5),
        "xa_w_o": nrm(ks[23], (DEPTH, D_MODEL, D_MODEL), D_MODEL ** -0.5),
        "g_ffn": gain(ks[24], (DEPTH, D_MODEL)),
        "ffn_w_gate": nrm(ks[25], (DEPTH, D_MODEL, D_FF), D_MODEL ** -0.5),
        "ffn_w_up": nrm(ks[26], (DEPTH, D_MODEL, D_FF), D_MODEL ** -0.5),
        "ffn_w_down": nrm(ks[27], (DEPTH, D_FF, D_MODEL), D_FF ** -0.5),
        "g_final": gain(ks[28], (D_MODEL,)),
    }


def reference(x_prompt, x_sample, mem_prompt, cache_mem_k, cache_mem_v, state_gla, state_gdn, state_conv,
              g_mix, w_in, gla_w_gate2, gla_b_gate, gla_g_out, gdn_conv_w, gdn_a_log, gdn_dt_bias, gdn_g_out,
              w_out, g_xattn, g_mem, xa_w_q, xa_w_k, xa_w_v, xa_w_o, g_ffn, ffn_w_gate, ffn_w_up, ffn_w_down,
              g_final):
    bp = x_prompt.shape[0]
    zero_conv = jnp.zeros((bp, CONV_W - 1, GDN_CONV_DIM), x_prompt.dtype)
    zero_gla = jnp.zeros((bp, GLA_HEADS, GLA_DK, GLA_DV), jnp.float32)
    zero_gdn = jnp.zeros((bp, GDN_HEADS, GDN_DK, GDN_DV), jnp.float32)
    xp, xs = x_prompt, x_sample
    p_gla, p_gdn, p_conv, p_mk, p_mv = [], [], [], [], []
    s_gla_l, s_gdn_l, s_conv_l = [], [], []
    for l in range(DEPTH):
        lp = (g_mix[l], w_in[l], gla_w_gate2[l], gla_b_gate[l], gla_g_out[l], gdn_conv_w[l], gdn_a_log[l],
              gdn_dt_bias[l], gdn_g_out[l], w_out[l], g_xattn[l], xa_w_q[l], xa_w_o[l],
              g_ffn[l], ffn_w_gate[l], ffn_w_up[l], ffn_w_down[l])
        mk, mv = memory_kv(mem_prompt, g_mem[l], xa_w_k[l], xa_w_v[l])
        xp, sa, sb, cb = block(xp, mk, mv, zero_conv, zero_gla, zero_gdn, True, *lp)
        p_gla.append(sa); p_gdn.append(sb); p_conv.append(cb); p_mk.append(mk); p_mv.append(mv)
        xs, sa, sb, cb = block(xs, cache_mem_k[l], cache_mem_v[l], state_conv[l], state_gla[l], state_gdn[l], False, *lp)
        s_gla_l.append(sa); s_gdn_l.append(sb); s_conv_l.append(cb)
    y_prompt = rmsnorm(xp, g_final)
    y_sample = rmsnorm(xs, g_final)
    new_gla_prompt = jnp.stack(p_gla).astype(state_gla.dtype)
    new_gdn_prompt = jnp.stack(p_gdn).astype(state_gdn.dtype)
    new_conv_prompt = jnp.stack(p_conv).astype(state_conv.dtype)
    new_mem_k_prompt = jnp.stack(p_mk).astype(cache_mem_k.dtype)
    new_mem_v_prompt = jnp.stack(p_mv).astype(cache_mem_v.dtype)
    new_gla_sample = jnp.stack(s_gla_l).astype(state_gla.dtype)
    new_gdn_sample = jnp.stack(s_gdn_l).astype(state_gdn.dtype)
    new_conv_sample = jnp.stack(s_conv_l).astype(state_conv.dtype)
    return (y_prompt, y_sample, new_gla_prompt, new_gdn_prompt, new_conv_prompt, new_mem_k_prompt, new_mem_v_prompt, new_gla_sample, new_gdn_sample, new_conv_sample)
```

```python
import functools

import jax
import jax.numpy as jnp
from jax import lax
from jax.experimental import pallas as pl
from jax.experimental.pallas import tpu as pltpu

F32 = jnp.float32
BF16 = jnp.bfloat16

EPS = 1e-6
GLA_HEADS = 4
GLA_GATE_RANK = 16
GLA_GATE_TEMP = 16.0
GDN_HEADS = 8
CONV_W = 4
XA_HEADS = 4

V7X_VMEM_LIMIT_BYTES = 56 * 1024 * 1024
LANES = 128
CHUNK = 64
SUB = 16
SMALL_COLS = LANES
DA_LANE = 16
DB_LANE = 24


def _cparams(n_axes):
    return pltpu.CompilerParams(
        dimension_semantics=("arbitrary",) * n_axes,
        vmem_limit_bytes=V7X_VMEM_LIMIT_BYTES)


def _dot(a, b):
    return jnp.dot(a, b, preferred_element_type=F32)


def _dot_nt(a, b):
    return lax.dot_general(a, b, (((1,), (1,)), ((), ())), preferred_element_type=F32)


def _dot_tn(a, b):
    return lax.dot_general(a, b, (((0,), (0,)), ((), ())), preferred_element_type=F32)


def _dot_exact(a, b):
    return jnp.dot(a, b, preferred_element_type=F32, precision=lax.Precision.HIGHEST)


def _silu(x):
    return x * (1.0 / (1.0 + jnp.exp(-x)))


def _sigmoid(x):
    return 1.0 / (1.0 + jnp.exp(-x))


def _softplus(x):
    return jnp.maximum(x, 0.0) + jnp.log1p(jnp.exp(-jnp.abs(x)))


def _log_sigmoid(x):
    return jnp.minimum(x, 0.0) - jnp.log1p(jnp.exp(-jnp.abs(x)))


def _rms(x, g):
    return x * lax.rsqrt(jnp.mean(x * x, axis=-1, keepdims=True) + EPS) * g


def _iota2(shape, axis):
    return lax.broadcasted_iota(jnp.int32, shape, axis)


def _rmsnorm_kernel(x_ref, g_ref, o_ref):
    o_ref[...] = _rms(x_ref[...], g_ref[...]).astype(o_ref.dtype)


def rmsnorm_rows(x, g3, layer, out_dtype, tm, row0=0, n_rows=None):
    m, d = x.shape
    n_rows = m if n_rows is None else n_rows
    blk0 = row0 // tm
    return pl.pallas_call(
        _rmsnorm_kernel,
        grid=(n_rows // tm,),
        in_specs=[pl.BlockSpec((tm, d), lambda i: (blk0 + i, 0)),
                  pl.BlockSpec((None, 1, d), lambda i: (layer, 0, 0))],
        out_specs=pl.BlockSpec((tm, d), lambda i: (i, 0)),
        out_shape=jax.ShapeDtypeStruct((n_rows, d), out_dtype),
        compiler_params=_cparams(1),
        name="rmsnorm",
    )(x, g3)


CAST_ROWS = 256


def _stage_weight(w_ref, wb_ref):
    k = w_ref.shape[0]

    def body(c, carry):
        r = pl.multiple_of(c * CAST_ROWS, CAST_ROWS)
        wb_ref[pl.ds(r, CAST_ROWS), :] = w_ref[pl.ds(r, CAST_ROWS), :].astype(BF16)
        return carry

    lax.fori_loop(0, k // CAST_ROWS, body, 0)


def _mm_kernel(*refs, has_res, stage, transposed):
    a_ref, w_ref = refs[0], refs[1]
    if transposed:
        w_ref = w_ref.at[0]
    pos = 2
    res_ref = None
    if has_res:
        res_ref = refs[pos]
        pos += 1
    o_ref = refs[pos]
    if stage:
        wb_ref = refs[pos + 1]

        @pl.when(pl.program_id(1) == 0)
        def _():
            _stage_weight(w_ref, wb_ref)

        w = wb_ref[...]
    else:
        w = w_ref[...]
    acc = _dot_nt(a_ref[...], w) if transposed else _dot(a_ref[...], w)
    if has_res:
        acc = res_ref[...] + acc
    o_ref[...] = acc.astype(o_ref.dtype)


def matmul(a, w3, layer, *, tm, tn, col0=0, n_cols=None, res=None, out_dtype=F32, transposed=False,
           name="matmul"):
    m, k = a.shape
    n_total = w3.shape[1] if transposed else w3.shape[2]
    n_cols = n_total if n_cols is None else n_cols
    assert m % tm == 0 and n_cols % tn == 0
    stage = w3.dtype != BF16
    assert not stage or (tn if transposed else k) % CAST_ROWS == 0
    if transposed:
        assert col0 % 8 == 0
        w_spec = pl.BlockSpec((pl.Element(1), pl.Element(tn), pl.Element(k)),
                              lambda j, i: (layer, pl.multiple_of(col0 + j * tn, 8), 0))
        wb_shape = (tn, k)
    else:
        assert col0 % tn == 0
        cb0 = col0 // tn
        w_spec = pl.BlockSpec((None, k, tn), lambda j, i: (layer, 0, cb0 + j))
        wb_shape = (k, tn)
    in_specs = [pl.BlockSpec((tm, k), lambda j, i: (i, 0)), w_spec]
    args = [a, w3]
    if res is not None:
        in_specs.append(pl.BlockSpec((tm, tn), lambda j, i: (i, j)))
        args.append(res)
    return pl.pallas_call(
        functools.partial(_mm_kernel, has_res=res is not None, stage=stage, transposed=transposed),
        grid=(n_cols // tn, m // tm),
        in_specs=in_specs,
        out_specs=pl.BlockSpec((tm, tn), lambda j, i: (i, j)),
        out_shape=jax.ShapeDtypeStruct((m, n_cols), out_dtype),
        scratch_shapes=[pltpu.VMEM(wb_shape, BF16)] if stage else [],
        compiler_params=_cparams(2),
        name=name,
    )(*args)


SMALL_PIECE = 16


def _proj_small_kernel(a_ref, wa_ref, wb_ref, o_ref, w_ref):
    @pl.when(pl.program_id(0) == 0)
    def _():
        w_ref[...] = jnp.zeros_like(w_ref)
        w_ref[0:SMALL_PIECE, :] = wa_ref[...].astype(BF16)
        w_ref[SMALL_PIECE:2 * SMALL_PIECE, :] = wb_ref[...].astype(BF16)

    o_ref[...] = _dot_nt(a_ref[...], w_ref[...])


def proj_small(a, wt3, layer, *, tm, row_a, row_b):
    m, k = a.shape
    assert row_a % SMALL_PIECE == 0 and row_b % SMALL_PIECE == 0
    piece = lambda r: pl.BlockSpec((None, SMALL_PIECE, k), lambda i: (layer, r // SMALL_PIECE, 0))
    return pl.pallas_call(
        _proj_small_kernel,
        grid=(m // tm,),
        in_specs=[pl.BlockSpec((tm, k), lambda i: (i, 0)), piece(row_a), piece(row_b)],
        out_specs=pl.BlockSpec((tm, SMALL_COLS), lambda i: (i, 0)),
        out_shape=jax.ShapeDtypeStruct((m, SMALL_COLS), F32),
        scratch_shapes=[pltpu.VMEM((SMALL_COLS, k), BF16)],
        compiler_params=_cparams(1),
        name="proj_small",
    )(a, wt3, wt3)


def _swiglu_kernel(a_ref, wg_ref, wu_ref, o_ref, wgb_ref, wub_ref):
    @pl.when(pl.program_id(1) == 0)
    def _():
        _stage_weight(wg_ref, wgb_ref)
        _stage_weight(wu_ref, wub_ref)

    a = a_ref[...]
    gate = _dot(a, wgb_ref[...])
    up = _dot(a, wub_ref[...])
    o_ref[...] = (_silu(gate) * up).astype(o_ref.dtype)


def swiglu(a, wg3, wu3, layer, *, tm, tn):
    m, k = a.shape
    f = wg3.shape[2]
    assert m % tm == 0 and f % tn == 0 and k % CAST_ROWS == 0
    wspec = pl.BlockSpec((None, k, tn), lambda j, i: (layer, 0, j))
    return pl.pallas_call(
        _swiglu_kernel,
        grid=(f // tn, m // tm),
        in_specs=[pl.BlockSpec((tm, k), lambda j, i: (i, 0)), wspec, wspec],
        out_specs=pl.BlockSpec((tm, tn), lambda j, i: (i, j)),
        out_shape=jax.ShapeDtypeStruct((m, f), BF16),
        scratch_shapes=[pltpu.VMEM((k, tn), BF16), pltpu.VMEM((k, tn), BF16)],
        compiler_params=_cparams(2),
        name="swiglu",
    )(a, wg3, wu3)


def _xattn_prompt_kernel(q_ref, k_ref, v_ref, o_ref, *, scale):
    s = _dot_nt(q_ref[...], k_ref[...].astype(BF16)) * scale
    e = jnp.exp(s - jnp.max(s, axis=-1, keepdims=True))
    p = e / jnp.sum(e, axis=-1, keepdims=True)
    o_ref[...] = _dot(p.astype(BF16), v_ref[...].astype(BF16)).astype(o_ref.dtype)


def xattn_prompt(qx, mk, mv, *, batch, seq, n_mem, tq):
    m, d = qx.shape
    hd = d // XA_HEADS
    nt = seq // tq
    qspec = pl.BlockSpec((tq, hd), lambda b, t, h: (b * nt + t, h))
    kvspec = pl.BlockSpec((n_mem, hd), lambda b, t, h: (b, h))
    return pl.pallas_call(
        functools.partial(_xattn_prompt_kernel, scale=hd ** -0.5),
        grid=(batch, nt, XA_HEADS),
        in_specs=[qspec, kvspec, kvspec],
        out_specs=qspec,
        out_shape=jax.ShapeDtypeStruct((m, d), BF16),
        compiler_params=_cparams(3),
        name="xattn_prompt",
    )(qx, mk, mv)


XS_ROWS = 16
XS_STEP = 2


def _xattn_sample_kernel(q_ref, k_ref, v_ref, o_in_ref, o_ref, qf_ref, of_ref, *, scale, hd):
    del o_in_ref
    j = pl.program_id(1)

    @pl.when(j == 0)
    def _():
        qf_ref[...] = q_ref[...].astype(F32)

    for t in range(XS_STEP):
        r = j * XS_STEP + t
        q_row = qf_ref[pl.ds(r, 1), :]
        q_hd = jnp.concatenate([q_row[:, h * hd:(h + 1) * hd] for h in range(XA_HEADS)], axis=0)
        s = jnp.sum(k_ref[t] * q_hd[None], axis=-1, keepdims=True) * scale
        e = jnp.exp(s - jnp.max(s, axis=0, keepdims=True))
        p = e / jnp.sum(e, axis=0, keepdims=True)
        o = jnp.sum(p * v_ref[t], axis=0)
        for h in range(XA_HEADS):
            of_ref[pl.ds(r, 1), h * hd:(h + 1) * hd] = o[h:h + 1, :]

    @pl.when(j == pl.num_programs(1) - 1)
    def _():
        o_ref[...] = of_ref[...].astype(o_ref.dtype)


def xattn_sample(qx, cache_k4, cache_v4, layer, o_all, *, row0, n_samples):
    m, d = qx.shape
    n_mem = cache_k4.shape[2]
    hd = d // XA_HEADS
    blk0 = row0 // XS_ROWS
    inner = XS_ROWS // XS_STEP
    qspec = pl.BlockSpec((XS_ROWS, d), lambda i, j: (blk0 + i, 0))
    cspec = pl.BlockSpec((None, XS_STEP, n_mem, XA_HEADS, hd), lambda i, j: (layer, i * inner + j, 0, 0, 0))
    return pl.pallas_call(
        functools.partial(_xattn_sample_kernel, scale=hd ** -0.5, hd=hd),
        grid=(n_samples // XS_ROWS, inner),
        in_specs=[qspec, cspec, cspec, pl.BlockSpec(memory_space=pl.ANY)],
        out_specs=qspec,
        out_shape=jax.ShapeDtypeStruct((m, d), BF16),
        scratch_shapes=[pltpu.VMEM((XS_ROWS, d), F32), pltpu.VMEM((XS_ROWS, d), F32)],
        input_output_aliases={3: 0},
        compiler_params=_cparams(2),
        name="xattn_sample",
    )(qx, cache_k4, cache_v4, o_all)


def _gla_log_decay(code, w2_ref, bg_ref):
    pre = _dot_exact(code, w2_ref[...]) + bg_ref[...]
    return _log_sigmoid(pre) * (1.0 / GLA_GATE_TEMP)


def _gla_prompt_kernel(pg_ref, ps_ref, w2_ref, bg_ref, go_ref, o_ref, st_ref, stt_ref, *, dk, dv):
    c = pl.program_id(1)
    C = CHUNK
    kw = GLA_HEADS * dk

    @pl.when(c == 0)
    def _():
        stt_ref[...] = jnp.zeros_like(stt_ref)

    la = _gla_log_decay(ps_ref[:, 0:GLA_GATE_RANK], w2_ref, bg_ref)
    tri = (_iota2((C, C), 0) >= _iota2((C, C), 1)).astype(F32)
    b_all = _dot_exact(tri, la)
    row_c = _iota2((C, dk), 0)
    row_s = _iota2((SUB, dk), 0)
    lane_a = _iota2((SUB, C), 1)
    ones_k = jnp.ones((dk, LANES), BF16)
    g_out = go_ref[...]

    for h in range(GLA_HEADS):
        q = pg_ref[:, h * dk:(h + 1) * dk] * (dk ** -0.5)
        k = pg_ref[:, kw + h * dk: kw + (h + 1) * dk]
        v = pg_ref[:, 2 * kw + h * dv: 2 * kw + (h + 1) * dv]
        r = pg_ref[:, 2 * kw + GLA_HEADS * dv + h * dv: 2 * kw + GLA_HEADS * dv + (h + 1) * dv]
        b = b_all[:, h * dk:(h + 1) * dk]
        b_last = b[C - 1:C, :]
        vb = v.astype(BF16)
        s_t = stt_ref[h]

        o = _dot_nt((q * jnp.exp(b)).astype(BF16), s_t.astype(BF16))
        k_dec = (k * jnp.exp(b_last - b)).astype(BF16)
        stt_ref[h] = s_t * jnp.exp(b_last) + _dot_tn(vb, k_dec)

        a_rows = []
        for i in range(C // SUB):
            lo = i * SUB
            qi, ki, bi = q[lo:lo + SUB], k[lo:lo + SUB], b[lo:lo + SUB]
            zs = []
            for s in range(SUB):
                dec = jnp.exp(jnp.minimum(bi - bi[s:s + 1, :], 0.0))
                zs.append(jnp.where(row_s >= s, qi * ki[s:s + 1, :] * dec, 0.0))
            sums = _dot(jnp.concatenate(zs, axis=0).astype(BF16), ones_k)
            a_i = jnp.zeros((SUB, C), F32)
            for s in range(SUB):
                a_i = jnp.where(lane_a == lo + s, sums[s * SUB:(s + 1) * SUB, 0:C], a_i)
            if i > 0:
                b_ref = b[lo - 1:lo, :]
                q_rel = (qi * jnp.exp(bi - b_ref)).astype(BF16)
                k_rel = jnp.where(row_c < lo, k * jnp.exp(jnp.minimum(b_ref - b, 0.0)), 0.0).astype(BF16)
                a_i = a_i + _dot_nt(q_rel, k_rel)
            a_rows.append(a_i)
        a = jnp.concatenate(a_rows, axis=0)
        o = o + _dot(a.astype(BF16), vb)

        o_ref[:, h * dv:(h + 1) * dv] = (_rms(o, g_out) * _silu(r)).astype(o_ref.dtype)

    @pl.when(c == pl.num_programs(1) - 1)
    def _():
        for h in range(GLA_HEADS):
            st_ref[0, h] = stt_ref[h].T


def gla_prompt(pg, ps, w2_3, bg_3, go_3, layer, *, batch, seq, d_model):
    m, pg_cols = pg.shape
    kw = w2_3.shape[2]
    dk = kw // GLA_HEADS
    dv = go_3.shape[2]
    width = GLA_HEADS * dv
    nc = seq // CHUNK
    row = lambda b, c: (b * nc + c, 0)
    return pl.pallas_call(
        functools.partial(_gla_prompt_kernel, dk=dk, dv=dv),
        grid=(batch, nc),
        in_specs=[pl.BlockSpec((CHUNK, pg_cols), row),
                  pl.BlockSpec((CHUNK, SMALL_COLS), row),
                  pl.BlockSpec((None, GLA_GATE_RANK, kw), lambda b, c: (layer, 0, 0)),
                  pl.BlockSpec((None, 1, kw), lambda b, c: (layer, 0, 0)),
                  pl.BlockSpec((None, 1, dv), lambda b, c: (layer, 0, 0))],
        out_specs=[pl.BlockSpec((CHUNK, width), row),
                   pl.BlockSpec((1, GLA_HEADS, dk, dv), lambda b, c: (b, 0, 0, 0))],
        out_shape=[jax.ShapeDtypeStruct((m, d_model), BF16),
                   jax.ShapeDtypeStruct((batch, GLA_HEADS, dk, dv), F32)],
        scratch_shapes=[pltpu.VMEM((GLA_HEADS, dv, dk), F32)],
        compiler_params=_cparams(2),
        name="gla_prompt",
    )(pg, ps, w2_3, bg_3, go_3)


MIX_ROWS = 16


def _columns(x):
    pad = jnp.zeros((LANES - MIX_ROWS, x.shape[1]), F32)
    return jnp.concatenate([x, pad], axis=0).T


def _gla_sample_kernel(pg_ref, ps_ref, w2_ref, bg_ref, go_ref, st_ref, o_in_ref, st_in_ref,
                       o_ref, sto_ref, *, dk, dv):
    del o_in_ref, st_in_ref
    kw = GLA_HEADS * dk
    decay = jnp.exp(_gla_log_decay(ps_ref[:, 0:GLA_GATE_RANK], w2_ref, bg_ref))
    g_out = go_ref[...]
    for h in range(GLA_HEADS):
        q_c = _columns(pg_ref[:, h * dk:(h + 1) * dk] * (dk ** -0.5))
        k_c = _columns(pg_ref[:, kw + h * dk: kw + (h + 1) * dk])
        a_c = _columns(decay[:, h * dk:(h + 1) * dk])
        v = pg_ref[:, 2 * kw + h * dv: 2 * kw + (h + 1) * dv]
        r = pg_ref[:, 2 * kw + GLA_HEADS * dv + h * dv: 2 * kw + GLA_HEADS * dv + (h + 1) * dv]
        rows = []
        for s in range(MIX_ROWS):
            st = st_ref[s, h] * a_c[:, s:s + 1] + k_c[:, s:s + 1] * v[s:s + 1, :]
            sto_ref[s, h] = st
            rows.append(jnp.sum(q_c[:, s:s + 1] * st, axis=0, keepdims=True))
        o = jnp.concatenate(rows, axis=0)
        o_ref[:, h * dv:(h + 1) * dv] = (_rms(o, g_out) * _silu(r)).astype(o_ref.dtype)


def gla_sample(pg, ps, w2_3, bg_3, go_3, state5, layer, o_all, st_all, *, row0, n_samples):
    m, pg_cols = pg.shape
    kw = w2_3.shape[2]
    dk = kw // GLA_HEADS
    dv = go_3.shape[2]
    width = GLA_HEADS * dv
    blk0 = row0 // MIX_ROWS
    row = lambda i: (blk0 + i, 0)
    st_spec = pl.BlockSpec((None, MIX_ROWS, GLA_HEADS, dk, dv), lambda i: (layer, i, 0, 0, 0))
    any_spec = pl.BlockSpec(memory_space=pl.ANY)
    in_specs = [pl.BlockSpec((MIX_ROWS, pg_cols), row),
                pl.BlockSpec((MIX_ROWS, SMALL_COLS), row),
                pl.BlockSpec((None, GLA_GATE_RANK, kw), lambda i: (layer, 0, 0)),
                pl.BlockSpec((None, 1, kw), lambda i: (layer, 0, 0)),
                pl.BlockSpec((None, 1, dv), lambda i: (layer, 0, 0)),
                st_spec, any_spec]
    args = [pg, ps, w2_3, bg_3, go_3, state5, o_all]
    aliases = {6: 0}
    kern = functools.partial(_gla_sample_kernel, dk=dk, dv=dv)
    if st_all is not None:
        in_specs.append(any_spec)
        args.append(st_all)
        aliases[7] = 1
    else:
        kern = functools.partial(_no_state_in, kern, 7)
    return pl.pallas_call(
        kern,
        grid=(n_samples // MIX_ROWS,),
        in_specs=in_specs,
        out_specs=[pl.BlockSpec((MIX_ROWS, width), row), st_spec],
        out_shape=[jax.ShapeDtypeStruct(o_all.shape, BF16),
                   jax.ShapeDtypeStruct(state5.shape, F32)],
        input_output_aliases=aliases,
        compiler_params=_cparams(1),
        name="gla_sample",
    )(*args)


def _no_state_in(kern, pos, *refs):
    return kern(*refs[:pos], None, *refs[pos:])


def _gdn_gates(ps, alog_ref, dtb_ref):
    g = -jnp.exp(alog_ref[...]) * _softplus(ps + dtb_ref[...])
    return g, _sigmoid(ps)


def _l2norm(x):
    return x * lax.rsqrt(jnp.sum(x * x, axis=-1, keepdims=True) + EPS)


def _inv_unit_lower(low):
    n = low.shape[0]
    r = _iota2((n, n), 0)
    c = _iota2((n, n), 1)
    eye = (r == c).astype(F32)
    base = 8
    blk = lambda size: (jnp.right_shift(r, size.bit_length() - 1), jnp.right_shift(c, size.bit_length() - 1))
    rb, cb = blk(base)
    l8 = jnp.where(rb == cb, low, 0.0).astype(BF16)
    x = eye - l8.astype(F32)
    p = _dot(l8, l8)
    x = x + _dot(x.astype(BF16), p.astype(BF16))
    p = _dot(p.astype(BF16), p.astype(BF16))
    x = x + _dot(x.astype(BF16), p.astype(BF16))
    m = base
    while m < n:
        r2, c2 = blk(2 * m)
        r1, c1 = blk(m)
        off = jnp.where(r2 == c2, jnp.where(r1 != c1, low, 0.0), 0.0).astype(BF16)
        xb = x.astype(BF16)
        x = x - _dot(xb, _dot(off, xb).astype(BF16))
        m *= 2
    return x


def _gdn_prompt_kernel(pd_ref, ps_ref, cw_ref, alog_ref, dtb_ref, go_ref, o_in_ref, o_ref, st_ref,
                       xb_ref, s_ref, *, dk, dv):
    del o_in_ref
    c = pl.program_id(1)
    C = CHUNK
    kw = GDN_HEADS * dk
    conv_dim = 2 * kw + GDN_HEADS * dv
    halo = 8

    @pl.when(c == 0)
    def _():
        xb_ref[0:halo, :] = jnp.zeros((halo, conv_dim), F32)
        s_ref[...] = jnp.zeros_like(s_ref)

    xb_ref[halo:halo + C, :] = pd_ref[:, 0:conv_dim]
    conv = xb_ref[halo - (CONV_W - 1):halo - (CONV_W - 1) + C, :] * cw_ref[0:1, :]
    for j in range(1, CONV_W):
        lo = halo - (CONV_W - 1) + j
        conv = conv + xb_ref[lo:lo + C, :] * cw_ref[j:j + 1, :]
    conv = _silu(conv)
    xb_ref[0:halo, :] = xb_ref[C:C + halo, :]

    g_all, beta_all = _gdn_gates(ps_ref[...], alog_ref, dtb_ref)
    tri = (_iota2((C, C), 0) >= _iota2((C, C), 1)).astype(F32)
    gc_all = _dot_exact(tri, g_all)
    gr_all = gc_all.T
    r_i = _iota2((C, C), 0)
    c_i = _iota2((C, C), 1)
    incl = r_i >= c_i
    strict = r_i > c_i
    g_out = go_ref[...]

    for h in range(GDN_HEADS):
        q = _l2norm(conv[:, h * dk:(h + 1) * dk]) * (dk ** -0.5)
        k = _l2norm(conv[:, kw + h * dk: kw + (h + 1) * dk])
        v = conv[:, 2 * kw + h * dv: 2 * kw + (h + 1) * dv]
        z = pd_ref[:, conv_dim + h * dv: conv_dim + (h + 1) * dv]
        beta = beta_all[:, DB_LANE + h: DB_LANE + h + 1]
        gc = gc_all[:, DA_LANE + h: DA_LANE + h + 1]
        gr = gr_all[DA_LANE + h: DA_LANE + h + 1, :]
        g_last = gc[C - 1:C, :]
        decay = jnp.where(incl, jnp.exp(jnp.minimum(gc - gr, 0.0)), 0.0)

        kb = k * beta
        kbf = k.astype(BF16)
        low = jnp.where(strict, _dot_nt(kb.astype(BF16), kbf) * decay, 0.0)
        t_inv = _inv_unit_lower(low).astype(BF16)
        rhs = jnp.concatenate([v * beta, kb * jnp.exp(gc)], axis=-1).astype(BF16)
        sol = _dot(t_inv, rhs)
        u_base, w = sol[:, 0:dv], sol[:, dv:dv + dk]
        qk = (_dot_nt(q.astype(BF16), kbf) * decay).astype(BF16)
        q_dec = (q * jnp.exp(gc)).astype(BF16)
        k_dec = (k * jnp.exp(g_last - gc)).astype(BF16)

        s = s_ref[h]
        sb = s.astype(BF16)
        u = (u_base - _dot(w.astype(BF16), sb)).astype(BF16)
        o = _dot(q_dec, sb) + _dot(qk, u)
        s_ref[h] = jnp.exp(g_last) * s + _dot_tn(k_dec, u)

        o_ref[:, h * dv:(h + 1) * dv] = (_rms(o, g_out) * _silu(z)).astype(o_ref.dtype)

    @pl.when(c == pl.num_programs(1) - 1)
    def _():
        st_ref[0] = s_ref[...]


def gdn_prompt(pd, ps, cw_3, alog_3, dtb_3, go_3, layer, o_all, *, batch, seq, col0):
    m, pd_cols = pd.shape
    dv = go_3.shape[2]
    dk = dv
    width = GDN_HEADS * dv
    conv_dim = cw_3.shape[2]
    nc = seq // CHUNK
    row = lambda b, c: (b * nc + c, 0)
    cb = col0 // width
    lane_spec = pl.BlockSpec((None, 1, SMALL_COLS), lambda b, c: (layer, 0, 0))
    return pl.pallas_call(
        functools.partial(_gdn_prompt_kernel, dk=dk, dv=dv),
        grid=(batch, nc),
        in_specs=[pl.BlockSpec((CHUNK, pd_cols), row),
                  pl.BlockSpec((CHUNK, SMALL_COLS), row),
                  pl.BlockSpec((None, CONV_W, conv_dim), lambda b, c: (layer, 0, 0)),
                  lane_spec, lane_spec,
                  pl.BlockSpec((None, 1, dv), lambda b, c: (layer, 0, 0)),
                  pl.BlockSpec(memory_space=pl.ANY)],
        out_specs=[pl.BlockSpec((CHUNK, width), lambda b, c: (b * nc + c, cb)),
                   pl.BlockSpec((1, GDN_HEADS, dk, dv), lambda b, c: (b, 0, 0, 0))],
        out_shape=[jax.ShapeDtypeStruct(o_all.shape, BF16),
                   jax.ShapeDtypeStruct((batch, GDN_HEADS, dk, dv), F32)],
        scratch_shapes=[pltpu.VMEM((CHUNK + 8, conv_dim), F32),
                        pltpu.VMEM((GDN_HEADS, dk, dv), F32)],
        input_output_aliases={6: 0},
        compiler_params=_cparams(2),
        name="gdn_prompt",
    )(pd, ps, cw_3, alog_3, dtb_3, go_3, o_all)


def _gdn_sample_kernel(pd_ref, ps_ref, cw_ref, alog_ref, dtb_ref, go_ref, cs_ref, st_ref,
                       o_in_ref, cs_in_ref, st_in_ref, o_ref, cso_ref, sto_ref, *, dk, dv):
    del o_in_ref, cs_in_ref, st_in_ref
    kw = GDN_HEADS * dk
    conv_dim = 2 * kw + GDN_HEADS * dv
    new = pd_ref[:, 0:conv_dim]
    conv = new * cw_ref[CONV_W - 1:CONV_W, :]
    for j in range(CONV_W - 1):
        conv = conv + cs_ref[j] * cw_ref[j:j + 1, :]
    conv = _silu(conv)
    for j in range(CONV_W - 2):
        cso_ref[j] = cs_ref[j + 1]
    cso_ref[CONV_W - 2] = new

    g_all, beta_all = _gdn_gates(ps_ref[...], alog_ref, dtb_ref)
    a_all = jnp.exp(g_all)
    g_out = go_ref[...]
    for h in range(GDN_HEADS):
        q_c = _columns(_l2norm(conv[:, h * dk:(h + 1) * dk]) * (dk ** -0.5))
        k_c = _columns(_l2norm(conv[:, kw + h * dk: kw + (h + 1) * dk]))
        v = conv[:, 2 * kw + h * dv: 2 * kw + (h + 1) * dv]
        z = pd_ref[:, conv_dim + h * dv: conv_dim + (h + 1) * dv]
        rows = []
        for s in range(MIX_ROWS):
            a = a_all[s:s + 1, DA_LANE + h: DA_LANE + h + 1]
            beta = beta_all[s:s + 1, DB_LANE + h: DB_LANE + h + 1]
            kc = k_c[:, s:s + 1]
            st = st_ref[s, h] * a
            u = beta * (v[s:s + 1, :] - jnp.sum(kc * st, axis=0, keepdims=True))
            st = st + kc * u
            sto_ref[s, h] = st
            rows.append(jnp.sum(q_c[:, s:s + 1] * st, axis=0, keepdims=True))
        o = jnp.concatenate(rows, axis=0)
        o_ref[:, h * dv:(h + 1) * dv] = (_rms(o, g_out) * _silu(z)).astype(o_ref.dtype)


def gdn_sample(pd, ps, cw_3, alog_3, dtb_3, go_3, conv3, state5, layer, o_all, cs_all, st_all,
               *, row0, n_samples, col0):
    m, pd_cols = pd.shape
    dv = go_3.shape[2]
    dk = dv
    width = GDN_HEADS * dv
    conv_dim = cw_3.shape[2]
    blk0 = row0 // MIX_ROWS
    row = lambda i: (blk0 + i, 0)
    cb = col0 // width
    lane_spec = pl.BlockSpec((None, 1, SMALL_COLS), lambda i: (layer, 0, 0))
    cs_spec = pl.BlockSpec((None, CONV_W - 1, MIX_ROWS, conv_dim), lambda i: (layer, 0, i, 0))
    st_spec = pl.BlockSpec((None, MIX_ROWS, GDN_HEADS, dk, dv), lambda i: (layer, i, 0, 0, 0))
    any_spec = pl.BlockSpec(memory_space=pl.ANY)
    in_specs = [pl.BlockSpec((MIX_ROWS, pd_cols), row),
                pl.BlockSpec((MIX_ROWS, SMALL_COLS), row),
                pl.BlockSpec((None, CONV_W, conv_dim), lambda i: (layer, 0, 0)),
                lane_spec, lane_spec,
                pl.BlockSpec((None, 1, dv), lambda i: (layer, 0, 0)),
                cs_spec, st_spec, any_spec]
    args = [pd, ps, cw_3, alog_3, dtb_3, go_3, conv3, state5, o_all]
    aliases = {8: 0}
    kern = functools.partial(_gdn_sample_kernel, dk=dk, dv=dv)
    if st_all is not None:
        in_specs += [any_spec, any_spec]
        args += [cs_all, st_all]
        aliases[9] = 1
        aliases[10] = 2
    else:
        kern = functools.partial(_no_state_in, functools.partial(_no_state_in, kern, 9), 9)
    return pl.pallas_call(
        kern,
        grid=(n_samples // MIX_ROWS,),
        in_specs=in_specs,
        out_specs=[pl.BlockSpec((MIX_ROWS, width), lambda i: (blk0 + i, cb)), cs_spec, st_spec],
        out_shape=[jax.ShapeDtypeStruct(o_all.shape, BF16),
                   jax.ShapeDtypeStruct(conv3.shape, F32),
                   jax.ShapeDtypeStruct(state5.shape, F32)],
        input_output_aliases=aliases,
        compiler_params=_cparams(1),
        name="gdn_sample",
    )(*args)


def _row_tile(m, cap):
    best = 16
    for t in range(16, cap + 1, 16):
        if m % t == 0:
            best = t
    return best


def kernel(x_prompt, x_sample, mem_prompt, cache_mem_k, cache_mem_v, state_gla, state_gdn, state_conv, g_mix, w_in, gla_w_gate2, gla_b_gate, gla_g_out, gdn_conv_w, gdn_a_log, gdn_dt_bias, gdn_g_out, w_out, g_xattn, g_mem, xa_w_q, xa_w_k, xa_w_v, xa_w_o, g_ffn, ffn_w_gate, ffn_w_up, ffn_w_down, g_final):
    batch, seq, d = x_prompt.shape
    n_s = x_sample.shape[0]
    depth = w_in.shape[0]
    n_mem = mem_prompt.shape[1]
    mp = batch * seq
    m = mp + n_s
    gla_kw = gla_w_gate2.shape[2]
    gla_dv = gla_g_out.shape[1]
    gla_width = GLA_HEADS * gla_dv
    gdn_dv = gdn_g_out.shape[1]
    gdn_width = GDN_HEADS * gdn_dv
    conv_dim = gdn_conv_w.shape[2]
    n_gla = 2 * gla_kw + 2 * gla_width
    c_code = n_gla
    c_conv = c_code + GLA_GATE_RANK
    n_gdn = conv_dim + gdn_width
    c_da = c_conv + n_gdn

    tm = _row_tile(m, 1088)
    tm_mem = _row_tile(batch * n_mem, 1024)

    row3 = lambda p: p.reshape(depth, 1, p.shape[-1])
    lane_row = lambda p: jnp.pad(p, ((0, 0), (DA_LANE, SMALL_COLS - DA_LANE - GDN_HEADS))).reshape(depth, 1, SMALL_COLS)
    g_mix3, g_xattn3, g_mem3, g_ffn3 = row3(g_mix), row3(g_xattn), row3(g_mem), row3(g_ffn)
    g_final3 = g_final.reshape(1, 1, d)
    bg3, gla_go3, gdn_go3 = row3(gla_b_gate), row3(gla_g_out), row3(gdn_g_out)
    alog3, dtb3 = lane_row(gdn_a_log), lane_row(gdn_dt_bias)

    w_in_t = jnp.swapaxes(w_in, 1, 2)
    w_down = ffn_w_down.astype(BF16)

    conv3 = jnp.swapaxes(state_conv, 1, 2)
    mem2 = mem_prompt.reshape(batch * n_mem, d)

    x = jnp.concatenate([x_prompt.reshape(mp, d), x_sample.reshape(n_s, d)], axis=0)

    p_gla, p_gdn, p_conv, p_mk, p_mv = [], [], [], [], []
    s_gla_all = s_gdn_all = s_conv_all = None
    for l in range(depth):
        h = rmsnorm_rows(x, g_mix3, l, BF16, tm)
        pg = matmul(h, w_in_t, l, tm=tm, tn=1024, col0=0, n_cols=n_gla, transposed=True, name="proj_gla")
        pd = matmul(h, w_in_t, l, tm=tm, tn=1024, col0=c_conv, n_cols=n_gdn, transposed=True, name="proj_gdn")
        ps = proj_small(h, w_in_t, l, tm=tm, row_a=c_code, row_b=c_da)

        o, sa = gla_prompt(pg, ps, gla_w_gate2, bg3, gla_go3, l, batch=batch, seq=seq, d_model=d)
        o, sb = gdn_prompt(pd, ps, gdn_conv_w, alog3, dtb3, gdn_go3, l, o, batch=batch, seq=seq, col0=gla_width)
        o, s_gla_all = gla_sample(pg, ps, gla_w_gate2, bg3, gla_go3, state_gla, l, o, s_gla_all,
                                  row0=mp, n_samples=n_s)
        o, s_conv_all, s_gdn_all = gdn_sample(pd, ps, gdn_conv_w, alog3, dtb3, gdn_go3, conv3, state_gdn, l,
                                              o, s_conv_all, s_gdn_all, row0=mp, n_samples=n_s, col0=gla_width)
        p_gla.append(sa)
        p_gdn.append(sb)
        p_conv.append(pd[:mp, :conv_dim].reshape(batch, seq, conv_dim)[:, seq - (CONV_W - 1):, :])
        x = matmul(o, w_out, l, tm=tm, tn=512, res=x, name="w_out")

        mem_n = rmsnorm_rows(mem2, g_mem3, l, BF16, tm_mem)
        mk = matmul(mem_n, xa_w_k, l, tm=tm_mem, tn=1024, name="mem_k")
        mv = matmul(mem_n, xa_w_v, l, tm=tm_mem, tn=1024, name="mem_v")
        p_mk.append(mk)
        p_mv.append(mv)
        h = rmsnorm_rows(x, g_xattn3, l, BF16, tm)
        qx = matmul(h, xa_w_q, l, tm=tm, tn=1024, out_dtype=BF16, name="xa_q")
        ox = xattn_prompt(qx, mk, mv, batch=batch, seq=seq, n_mem=n_mem, tq=min(seq, 1024))
        ox = xattn_sample(qx, cache_mem_k, cache_mem_v, l, ox, row0=mp, n_samples=n_s)
        x = matmul(ox, xa_w_o, l, tm=tm, tn=512, res=x, name="xa_o")

        h = rmsnorm_rows(x, g_ffn3, l, BF16, tm)
        ff = swiglu(h, ffn_w_gate, ffn_w_up, l, tm=tm, tn=512)
        x = matmul(ff, w_down, l, tm=tm, tn=512, res=x, name="ffn_down")

    y_prompt = rmsnorm_rows(x, g_final3, 0, F32, _row_tile(mp, 1024), row0=0, n_rows=mp)
    y_sample = rmsnorm_rows(x, g_final3, 0, F32, n_s, row0=mp, n_rows=n_s)
    xa_shape = (depth, batch, n_mem, XA_HEADS, d // XA_HEADS)
    return (y_prompt.reshape(batch, seq, d),
            y_sample.reshape(n_s, 1, d),
            jnp.stack(p_gla),
            jnp.stack(p_gdn),
            jnp.stack(p_conv),
            jnp.stack(p_mk).reshape(xa_shape),
            jnp.stack(p_mv).reshape(xa_shape),
            s_gla_all,
            s_gdn_all,
            jnp.swapaxes(s_conv_all, 1, 2))
```

```python
import functools

import jax
import jax.numpy as jnp
from jax import lax
from jax.experimental import pallas as pl
from jax.experimental.pallas import tpu as pltpu

F32 = jnp.float32
BF16 = jnp.bfloat16

EPS = 1e-6
GLA_HEADS = 4
GLA_GATE_RANK = 16
GLA_GATE_TEMP = 16.0
GDN_HEADS = 8
CONV_W = 4
XA_HEADS = 4

V7X_VMEM_LIMIT_BYTES = 56 * 1024 * 1024
LANES = 128
CHUNK = 128
SUB = 16
SMALL_COLS = LANES
DA_LANE = 16
DB_LANE = 24


def _cparams(n_axes):
    return pltpu.CompilerParams(
        dimension_semantics=("arbitrary",) * n_axes,
        vmem_limit_bytes=V7X_VMEM_LIMIT_BYTES)


def _dot(a, b):
    return jnp.dot(a, b, preferred_element_type=F32)


def _dot_nt(a, b):
    return lax.dot_general(a, b, (((1,), (1,)), ((), ())), preferred_element_type=F32)


def _dot_tn(a, b):
    return lax.dot_general(a, b, (((0,), (0,)), ((), ())), preferred_element_type=F32)


def _dot_exact(a, b):
    return jnp.dot(a, b, preferred_element_type=F32, precision=lax.Precision.HIGHEST)


def _silu(x):
    return x * (1.0 / (1.0 + jnp.exp(-x)))


def _sigmoid(x):
    return 1.0 / (1.0 + jnp.exp(-x))


def _softplus(x):
    return jnp.maximum(x, 0.0) + jnp.log1p(jnp.exp(-jnp.abs(x)))


def _log_sigmoid(x):
    return jnp.minimum(x, 0.0) - jnp.log1p(jnp.exp(-jnp.abs(x)))


def _rms(x, g):
    return x * lax.rsqrt(jnp.mean(x * x, axis=-1, keepdims=True) + EPS) * g


def _iota2(shape, axis):
    return lax.broadcasted_iota(jnp.int32, shape, axis)


def _rmsnorm_kernel(x_ref, g_ref, o_ref):
    o_ref[...] = _rms(x_ref[...], g_ref[...]).astype(o_ref.dtype)


def rmsnorm_rows(x, g3, layer, out_dtype, tm, row0=0, n_rows=None):
    m, d = x.shape
    n_rows = m if n_rows is None else n_rows
    blk0 = row0 // tm
    return pl.pallas_call(
        _rmsnorm_kernel,
        grid=(n_rows // tm,),
        in_specs=[pl.BlockSpec((tm, d), lambda i: (blk0 + i, 0)),
                  pl.BlockSpec((None, 1, d), lambda i: (layer, 0, 0))],
        out_specs=pl.BlockSpec((tm, d), lambda i: (i, 0)),
        out_shape=jax.ShapeDtypeStruct((n_rows, d), out_dtype),
        compiler_params=_cparams(1),
        name="rmsnorm",
    )(x, g3)


CAST_ROWS = 256


def _stage_weight(w_ref, wb_ref):
    k = w_ref.shape[0]

    def body(c, carry):
        r = pl.multiple_of(c * CAST_ROWS, CAST_ROWS)
        wb_ref[pl.ds(r, CAST_ROWS), :] = w_ref[pl.ds(r, CAST_ROWS), :].astype(BF16)
        return carry

    lax.fori_loop(0, k // CAST_ROWS, body, 0)


def _mm_kernel(*refs, has_res, stage, transposed):
    a_ref, w_ref = refs[0], refs[1]
    if transposed:
        w_ref = w_ref.at[0]
    pos = 2
    res_ref = None
    if has_res:
        res_ref = refs[pos]
        pos += 1
    o_ref = refs[pos]
    if stage:
        wb_ref = refs[pos + 1]

        @pl.when(pl.program_id(1) == 0)
        def _():
            _stage_weight(w_ref, wb_ref)

        w = wb_ref[...]
    else:
        w = w_ref[...]
    acc = _dot_nt(a_ref[...], w) if transposed else _dot(a_ref[...], w)
    if has_res:
        acc = res_ref[...] + acc
    o_ref[...] = acc.astype(o_ref.dtype)


def matmul(a, w3, layer, *, tm, tn, col0=0, n_cols=None, res=None, out_dtype=F32, transposed=False,
           name="matmul"):
    m, k = a.shape
    n_total = w3.shape[1] if transposed else w3.shape[2]
    n_cols = n_total if n_cols is None else n_cols
    assert m % tm == 0 and n_cols % tn == 0
    stage = w3.dtype != BF16
    assert not stage or (tn if transposed else k) % CAST_ROWS == 0
    if transposed:
        assert col0 % 8 == 0
        w_spec = pl.BlockSpec((pl.Element(1), pl.Element(tn), pl.Element(k)),
                              lambda j, i: (layer, pl.multiple_of(col0 + j * tn, 8), 0))
        wb_shape = (tn, k)
    else:
        assert col0 % tn == 0
        cb0 = col0 // tn
        w_spec = pl.BlockSpec((None, k, tn), lambda j, i: (layer, 0, cb0 + j))
        wb_shape = (k, tn)
    in_specs = [pl.BlockSpec((tm, k), lambda j, i: (i, 0)), w_spec]
    args = [a, w3]
    if res is not None:
        in_specs.append(pl.BlockSpec((tm, tn), lambda j, i: (i, j)))
        args.append(res)
    return pl.pallas_call(
        functools.partial(_mm_kernel, has_res=res is not None, stage=stage, transposed=transposed),
        grid=(n_cols // tn, m // tm),
        in_specs=in_specs,
        out_specs=pl.BlockSpec((tm, tn), lambda j, i: (i, j)),
        out_shape=jax.ShapeDtypeStruct((m, n_cols), out_dtype),
        scratch_shapes=[pltpu.VMEM(wb_shape, BF16)] if stage else [],
        compiler_params=_cparams(2),
        name=name,
    )(*args)


SMALL_PIECE = 16


def _proj_small_kernel(a_ref, wa_ref, wb_ref, o_ref, w_ref):
    @pl.when(pl.program_id(0) == 0)
    def _():
        w_ref[...] = jnp.zeros_like(w_ref)
        w_ref[0:SMALL_PIECE, :] = wa_ref[...].astype(BF16)
        w_ref[SMALL_PIECE:2 * SMALL_PIECE, :] = wb_ref[...].astype(BF16)

    o_ref[...] = _dot_nt(a_ref[...], w_ref[...])


def proj_small(a, wt3, layer, *, tm, row_a, row_b):
    m, k = a.shape
    assert row_a % SMALL_PIECE == 0 and row_b % SMALL_PIECE == 0
    piece = lambda r: pl.BlockSpec((None, SMALL_PIECE, k), lambda i: (layer, r // SMALL_PIECE, 0))
    return pl.pallas_call(
        _proj_small_kernel,
        grid=(m // tm,),
        in_specs=[pl.BlockSpec((tm, k), lambda i: (i, 0)), piece(row_a), piece(row_b)],
        out_specs=pl.BlockSpec((tm, SMALL_COLS), lambda i: (i, 0)),
        out_shape=jax.ShapeDtypeStruct((m, SMALL_COLS), F32),
        scratch_shapes=[pltpu.VMEM((SMALL_COLS, k), BF16)],
        compiler_params=_cparams(1),
        name="proj_small",
    )(a, wt3, wt3)


def _swiglu_kernel(a_ref, wg_ref, wu_ref, o_ref, wgb_ref, wub_ref):
    @pl.when(pl.program_id(1) == 0)
    def _():
        _stage_weight(wg_ref, wgb_ref)
        _stage_weight(wu_ref, wub_ref)

    a = a_ref[...]
    gate = _dot(a, wgb_ref[...])
    up = _dot(a, wub_ref[...])
    o_ref[...] = (_silu(gate) * up).astype(o_ref.dtype)


def swiglu(a, wg3, wu3, layer, *, tm, tn):
    m, k = a.shape
    f = wg3.shape[2]
    assert m % tm == 0 and f % tn == 0 and k % CAST_ROWS == 0
    wspec = pl.BlockSpec((None, k, tn), lambda j, i: (layer, 0, j))
    return pl.pallas_call(
        _swiglu_kernel,
        grid=(f // tn, m // tm),
        in_specs=[pl.BlockSpec((tm, k), lambda j, i: (i, 0)), wspec, wspec],
        out_specs=pl.BlockSpec((tm, tn), lambda j, i: (i, j)),
        out_shape=jax.ShapeDtypeStruct((m, f), BF16),
        scratch_shapes=[pltpu.VMEM((k, tn), BF16), pltpu.VMEM((k, tn), BF16)],
        compiler_params=_cparams(2),
        name="swiglu",
    )(a, wg3, wu3)


def _xattn_prompt_kernel(q_ref, k_ref, v_ref, o_ref, *, scale, hd):
    heads = [slice(h * hd, (h + 1) * hd) for h in range(XA_HEADS)]
    s = [_dot_nt(q_ref[:, c], k_ref[:, c].astype(BF16)) * scale for c in heads]
    e = [jnp.exp(x - jnp.max(x, axis=-1, keepdims=True)) for x in s]
    p = [(x / jnp.sum(x, axis=-1, keepdims=True)).astype(BF16) for x in e]
    for x, c in zip(p, heads):
        o_ref[:, c] = _dot(x, v_ref[:, c].astype(BF16)).astype(o_ref.dtype)


def xattn_prompt(qx, mk, mv, *, batch, seq, n_mem, tq):
    m, d = qx.shape
    nt = seq // tq
    qspec = pl.BlockSpec((tq, d), lambda b, t: (b * nt + t, 0))
    kvspec = pl.BlockSpec((n_mem, d), lambda b, t: (b, 0))
    return pl.pallas_call(
        functools.partial(_xattn_prompt_kernel, scale=(d // XA_HEADS) ** -0.5, hd=d // XA_HEADS),
        grid=(batch, nt),
        in_specs=[qspec, kvspec, kvspec],
        out_specs=qspec,
        out_shape=jax.ShapeDtypeStruct((m, d), BF16),
        compiler_params=_cparams(2),
        name="xattn_prompt",
    )(qx, mk, mv)


XS_ROWS = 16
XS_STEP = 2


def _xattn_sample_kernel(q_ref, k_ref, v_ref, o_in_ref, o_ref, qf_ref, of_ref, *, scale, hd):
    del o_in_ref
    j = pl.program_id(1)

    @pl.when(j == 0)
    def _():
        qf_ref[...] = q_ref[...].astype(F32)

    nj = hd // LANES
    per_tok = nj * XA_HEADS
    half = per_tok // 2
    n_mem = k_ref.shape[1] // per_tok
    lanes = lambda h, jj: slice(h * hd + jj * LANES, h * hd + (jj + 1) * LANES)
    for t in range(XS_STEP):
        r = j * XS_STEP + t
        q_row = qf_ref[pl.ds(r, 1), :]
        q_tok = jnp.concatenate([q_row[:, lanes(h, jj)] for jj in range(nj) for h in range(XA_HEADS)], axis=0)
        prod = k_ref[t].reshape(n_mem, per_tok, LANES) * q_tok[None]
        part = prod[:, 0:half, :] + prod[:, half:per_tok, :]
        part = part + pltpu.roll(part, XA_HEADS, axis=1)
        s = jnp.sum(part, axis=-1, keepdims=True) * scale
        e = jnp.exp(s - jnp.max(s, axis=0, keepdims=True))
        p = e / jnp.sum(e, axis=0, keepdims=True)
        v3 = v_ref[t].reshape(n_mem, per_tok, LANES)
        acc = [jnp.sum(v3[:, g * half:(g + 1) * half, :] * p, axis=0) for g in range(2)]
        pieces = []
        for h in range(XA_HEADS):
            for jj in range(nj):
                row = (jj % 2) * XA_HEADS + h
                pieces.append(acc[jj // 2][row:row + 1, :])
        of_ref[pl.ds(r, 1), :] = jnp.concatenate(pieces, axis=1)

    @pl.when(j == pl.num_programs(1) - 1)
    def _():
        o_ref[...] = of_ref[...].astype(o_ref.dtype)


def cache_token_rows(cache):
    depth, s, n_mem, heads, hd = cache.shape
    nj = hd // LANES
    c6 = cache.reshape(depth, s, n_mem, heads, nj, LANES)
    return jnp.transpose(c6, (0, 1, 2, 4, 3, 5)).reshape(depth, s, n_mem * nj * heads, LANES)


def xattn_sample(qx, cache_k4, cache_v4, layer, o_all, *, row0, n_samples):
    m, d = qx.shape
    tok_rows = cache_k4.shape[2]
    hd = d // XA_HEADS
    assert (hd // LANES) * XA_HEADS == 16 and XA_HEADS == 4
    blk0 = row0 // XS_ROWS
    inner = XS_ROWS // XS_STEP
    qspec = pl.BlockSpec((XS_ROWS, d), lambda i, j: (blk0 + i, 0))
    cspec = pl.BlockSpec((None, XS_STEP, tok_rows, LANES), lambda i, j: (layer, i * inner + j, 0, 0))
    return pl.pallas_call(
        functools.partial(_xattn_sample_kernel, scale=hd ** -0.5, hd=hd),
        grid=(n_samples // XS_ROWS, inner),
        in_specs=[qspec, cspec, cspec, pl.BlockSpec(memory_space=pl.ANY)],
        out_specs=qspec,
        out_shape=jax.ShapeDtypeStruct((m, d), BF16),
        scratch_shapes=[pltpu.VMEM((XS_ROWS, d), F32), pltpu.VMEM((XS_ROWS, d), F32)],
        input_output_aliases={3: 0},
        compiler_params=_cparams(2),
        name="xattn_sample",
    )(qx, cache_k4, cache_v4, o_all)


def _gla_log_decay(code, w2_ref, bg_ref):
    pre = _dot_exact(code, w2_ref[...]) + bg_ref[...]
    return _log_sigmoid(pre) * (1.0 / GLA_GATE_TEMP)


def _gla_prompt_kernel(pg_ref, ps_ref, w2_ref, bg_ref, go_ref, o_ref, st_ref, stt_ref, *, dk, dv):
    c = pl.program_id(1)
    C = CHUNK
    kw = GLA_HEADS * dk

    @pl.when(c == 0)
    def _():
        stt_ref[...] = jnp.zeros_like(stt_ref)

    la = _gla_log_decay(ps_ref[:, 0:GLA_GATE_RANK], w2_ref, bg_ref)
    tri = (_iota2((C, C), 0) >= _iota2((C, C), 1)).astype(F32)
    b_all = _dot_exact(tri, la)
    row_c = _iota2((C, dk), 0)
    row_s = _iota2((SUB, dk), 0)
    lane_a = _iota2((SUB, C), 1)
    ones_k = jnp.ones((dk, LANES), BF16)
    g_out = go_ref[...]

    heads = range(GLA_HEADS)
    col = lambda base, width, h: slice(base + h * width, base + (h + 1) * width)
    q = [pg_ref[:, col(0, dk, h)] * (dk ** -0.5) for h in heads]
    k = [pg_ref[:, col(kw, dk, h)] for h in heads]
    vb = [pg_ref[:, col(2 * kw, dv, h)].astype(BF16) for h in heads]
    b = [b_all[:, col(0, dk, h)] for h in heads]
    b_last = [bh[C - 1:C, :] for bh in b]

    s_t = [stt_ref[h] for h in heads]
    o = [_dot_nt((q[h] * jnp.exp(b[h])).astype(BF16), s_t[h].astype(BF16)) for h in heads]
    upd = [_dot_tn(vb[h], (k[h] * jnp.exp(b_last[h] - b[h])).astype(BF16)) for h in heads]
    for h in heads:
        stt_ref[h] = s_t[h] * jnp.exp(b_last[h]) + upd[h]

    a_rows = [[] for _ in heads]
    for i in range(C // SUB):
        lo = i * SUB
        sums = []
        for h in heads:
            qi, ki, bi = q[h][lo:lo + SUB], k[h][lo:lo + SUB], b[h][lo:lo + SUB]
            zs = []
            for s in range(SUB):
                dec = jnp.exp(jnp.minimum(bi - bi[s:s + 1, :], 0.0))
                zs.append(jnp.where(row_s >= s, qi * ki[s:s + 1, :] * dec, 0.0))
            sums.append(_dot(jnp.concatenate(zs, axis=0).astype(BF16), ones_k))
        for h in heads:
            a_i = jnp.zeros((SUB, C), F32)
            for s in range(SUB):
                a_i = jnp.where(lane_a == lo + s, sums[h][s * SUB:(s + 1) * SUB, 0:C], a_i)
            if i > 0:
                b_ref = b[h][lo - 1:lo, :]
                q_rel = (q[h][lo:lo + SUB] * jnp.exp(b[h][lo:lo + SUB] - b_ref)).astype(BF16)
                k_rel = jnp.where(row_c < lo, k[h] * jnp.exp(jnp.minimum(b_ref - b[h], 0.0)), 0.0).astype(BF16)
                a_i = a_i + _dot_nt(q_rel, k_rel)
            a_rows[h].append(a_i)
    o = [o[h] + _dot(jnp.concatenate(a_rows[h], axis=0).astype(BF16), vb[h]) for h in heads]

    for h in heads:
        r = pg_ref[:, col(2 * kw + GLA_HEADS * dv, dv, h)]
        o_ref[:, col(0, dv, h)] = (_rms(o[h], g_out) * _silu(r)).astype(o_ref.dtype)

    @pl.when(c == pl.num_programs(1) - 1)
    def _():
        for h in range(GLA_HEADS):
            st_ref[0, h] = stt_ref[h].T


def gla_prompt(pg, ps, w2_3, bg_3, go_3, layer, *, batch, seq, d_model):
    m, pg_cols = pg.shape
    kw = w2_3.shape[2]
    dk = kw // GLA_HEADS
    dv = go_3.shape[2]
    width = GLA_HEADS * dv
    nc = seq // CHUNK
    row = lambda b, c: (b * nc + c, 0)
    return pl.pallas_call(
        functools.partial(_gla_prompt_kernel, dk=dk, dv=dv),
        grid=(batch, nc),
        in_specs=[pl.BlockSpec((CHUNK, pg_cols), row),
                  pl.BlockSpec((CHUNK, SMALL_COLS), row),
                  pl.BlockSpec((None, GLA_GATE_RANK, kw), lambda b, c: (layer, 0, 0)),
                  pl.BlockSpec((None, 1, kw), lambda b, c: (layer, 0, 0)),
                  pl.BlockSpec((None, 1, dv), lambda b, c: (layer, 0, 0))],
        out_specs=[pl.BlockSpec((CHUNK, width), row),
                   pl.BlockSpec((1, GLA_HEADS, dk, dv), lambda b, c: (b, 0, 0, 0))],
        out_shape=[jax.ShapeDtypeStruct((m, d_model), BF16),
                   jax.ShapeDtypeStruct((batch, GLA_HEADS, dk, dv), F32)],
        scratch_shapes=[pltpu.VMEM((GLA_HEADS, dv, dk), F32)],
        compiler_params=_cparams(2),
        name="gla_prompt",
    )(pg, ps, w2_3, bg_3, go_3)


MIX_ROWS = 16


def _columns(x):
    pad = jnp.zeros((LANES - MIX_ROWS, x.shape[1]), F32)
    return jnp.concatenate([x, pad], axis=0).T


def _gla_sample_kernel(pg_ref, ps_ref, w2_ref, bg_ref, go_ref, st_ref, o_in_ref, st_in_ref,
                       o_ref, sto_ref, *, dk, dv):
    del o_in_ref, st_in_ref
    kw = GLA_HEADS * dk
    decay = jnp.exp(_gla_log_decay(ps_ref[:, 0:GLA_GATE_RANK], w2_ref, bg_ref))
    g_out = go_ref[...]
    for h in range(GLA_HEADS):
        q_c = _columns(pg_ref[:, h * dk:(h + 1) * dk] * (dk ** -0.5))
        k_c = _columns(pg_ref[:, kw + h * dk: kw + (h + 1) * dk])
        a_c = _columns(decay[:, h * dk:(h + 1) * dk])
        v = pg_ref[:, 2 * kw + h * dv: 2 * kw + (h + 1) * dv]
        r = pg_ref[:, 2 * kw + GLA_HEADS * dv + h * dv: 2 * kw + GLA_HEADS * dv + (h + 1) * dv]
        rows = []
        for s in range(MIX_ROWS):
            st = st_ref[s, h] * a_c[:, s:s + 1] + k_c[:, s:s + 1] * v[s:s + 1, :]
            sto_ref[s, h] = st
            rows.append(jnp.sum(q_c[:, s:s + 1] * st, axis=0, keepdims=True))
        o = jnp.concatenate(rows, axis=0)
        o_ref[:, h * dv:(h + 1) * dv] = (_rms(o, g_out) * _silu(r)).astype(o_ref.dtype)


def gla_sample(pg, ps, w2_3, bg_3, go_3, state5, layer, o_all, st_all, *, row0, n_samples):
    m, pg_cols = pg.shape
    kw = w2_3.shape[2]
    dk = kw // GLA_HEADS
    dv = go_3.shape[2]
    width = GLA_HEADS * dv
    blk0 = row0 // MIX_ROWS
    row = lambda i: (blk0 + i, 0)
    st_spec = pl.BlockSpec((None, MIX_ROWS, GLA_HEADS, dk, dv), lambda i: (layer, i, 0, 0, 0))
    any_spec = pl.BlockSpec(memory_space=pl.ANY)
    in_specs = [pl.BlockSpec((MIX_ROWS, pg_cols), row),
                pl.BlockSpec((MIX_ROWS, SMALL_COLS), row),
                pl.BlockSpec((None, GLA_GATE_RANK, kw), lambda i: (layer, 0, 0)),
                pl.BlockSpec((None, 1, kw), lambda i: (layer, 0, 0)),
                pl.BlockSpec((None, 1, dv), lambda i: (layer, 0, 0)),
                st_spec, any_spec]
    args = [pg, ps, w2_3, bg_3, go_3, state5, o_all]
    aliases = {6: 0}
    kern = functools.partial(_gla_sample_kernel, dk=dk, dv=dv)
    if st_all is not None:
        in_specs.append(any_spec)
        args.append(st_all)
        aliases[7] = 1
    else:
        kern = functools.partial(_no_state_in, kern, 7)
    return pl.pallas_call(
        kern,
        grid=(n_samples // MIX_ROWS,),
        in_specs=in_specs,
        out_specs=[pl.BlockSpec((MIX_ROWS, width), row), st_spec],
        out_shape=[jax.ShapeDtypeStruct(o_all.shape, BF16),
                   jax.ShapeDtypeStruct(state5.shape, F32)],
        input_output_aliases=aliases,
        compiler_params=_cparams(1),
        name="gla_sample",
    )(*args)


def _no_state_in(kern, pos, *refs):
    return kern(*refs[:pos], None, *refs[pos:])


def _gdn_gates(ps, alog_ref, dtb_ref):
    g = -jnp.exp(alog_ref[...]) * _softplus(ps + dtb_ref[...])
    return g, _sigmoid(ps)


def _l2norm(x):
    return x * lax.rsqrt(jnp.sum(x * x, axis=-1, keepdims=True) + EPS)


def _inv_unit_lower(lows):
    n = lows[0].shape[0]
    r = _iota2((n, n), 0)
    c = _iota2((n, n), 1)
    eye = (r == c).astype(F32)
    base = 8
    same = lambda size: jnp.right_shift(r, size.bit_length() - 1) == jnp.right_shift(c, size.bit_length() - 1)
    bf = lambda xs: [x.astype(BF16) for x in xs]
    in_base = same(base)
    l8 = bf([jnp.where(in_base, low, 0.0) for low in lows])
    x = [eye - a.astype(F32) for a in l8]
    p = [_dot(a, a) for a in l8]
    for step in range(2):
        pb = bf(p)
        x = [xi + _dot(xb, pi) for xi, xb, pi in zip(x, bf(x), pb)]
        if step == 0:
            p = [_dot(pi, pi) for pi in pb]
    m = base
    while m < n:
        in_2m, in_m = same(2 * m), same(m)
        off = bf([jnp.where(in_2m, jnp.where(in_m, 0.0, low), 0.0) for low in lows])
        xb = bf(x)
        y = bf([_dot(oi, xi) for oi, xi in zip(off, xb)])
        x = [xi - _dot(xbi, yi) for xi, xbi, yi in zip(x, xb, y)]
        m *= 2
    return x


def _gdn_prompt_kernel(pd_ref, ps_ref, cw_ref, alog_ref, dtb_ref, go_ref, o_in_ref, o_ref, st_ref,
                       xb_ref, s_ref, *, dk, dv):
    del o_in_ref
    c = pl.program_id(1)
    C = CHUNK
    kw = GDN_HEADS * dk
    conv_dim = 2 * kw + GDN_HEADS * dv
    halo = 8

    @pl.when(c == 0)
    def _():
        xb_ref[0:halo, :] = jnp.zeros((halo, conv_dim), F32)
        s_ref[...] = jnp.zeros_like(s_ref)

    xb_ref[halo:halo + C, :] = pd_ref[:, 0:conv_dim]
    conv = xb_ref[halo - (CONV_W - 1):halo - (CONV_W - 1) + C, :] * cw_ref[0:1, :]
    for j in range(1, CONV_W):
        lo = halo - (CONV_W - 1) + j
        conv = conv + xb_ref[lo:lo + C, :] * cw_ref[j:j + 1, :]
    conv = _silu(conv)
    xb_ref[0:halo, :] = xb_ref[C:C + halo, :]

    g_all, beta_all = _gdn_gates(ps_ref[...], alog_ref, dtb_ref)
    tri = (_iota2((C, C), 0) >= _iota2((C, C), 1)).astype(F32)
    gc_all = _dot_exact(tri, g_all)
    gr_all = gc_all.T
    r_i = _iota2((C, C), 0)
    c_i = _iota2((C, C), 1)
    incl = r_i >= c_i
    strict = r_i > c_i
    g_out = go_ref[...]

    heads = range(GDN_HEADS)
    col = lambda base, width, h: slice(base + h * width, base + (h + 1) * width)
    bf = lambda xs: [x.astype(BF16) for x in xs]
    q = [_l2norm(conv[:, col(0, dk, h)]) * (dk ** -0.5) for h in heads]
    k = [_l2norm(conv[:, col(kw, dk, h)]) for h in heads]
    v = [conv[:, col(2 * kw, dv, h)] for h in heads]
    beta = [beta_all[:, DB_LANE + h: DB_LANE + h + 1] for h in heads]
    gc = [gc_all[:, DA_LANE + h: DA_LANE + h + 1] for h in heads]
    g_last = [g[C - 1:C, :] for g in gc]
    decay = [jnp.where(incl, jnp.exp(jnp.minimum(gc[h] - gr_all[DA_LANE + h: DA_LANE + h + 1, :], 0.0)), 0.0)
             for h in heads]
    kb = [k[h] * beta[h] for h in heads]
    kbf, qbf = bf(k), bf(q)
    kk = [_dot_nt(a, b) for a, b in zip(bf(kb), kbf)]
    qk = [_dot_nt(a, b) for a, b in zip(qbf, kbf)]
    t_inv = bf(_inv_unit_lower([jnp.where(strict, kk[h] * decay[h], 0.0) for h in heads]))
    rhs = bf([jnp.concatenate([v[h] * beta[h], kb[h] * jnp.exp(gc[h])], axis=-1) for h in heads])
    sol = [_dot(t, x) for t, x in zip(t_inv, rhs)]

    s = [s_ref[h] for h in heads]
    sb = bf(s)
    ws = [_dot(sol[h][:, dv:dv + dk].astype(BF16), sb[h]) for h in heads]
    o = [_dot((q[h] * jnp.exp(gc[h])).astype(BF16), sb[h]) for h in heads]
    u = bf([sol[h][:, 0:dv] - ws[h] for h in heads])
    o = [o[h] + _dot((qk[h] * decay[h]).astype(BF16), u[h]) for h in heads]
    s_new = [_dot_tn((k[h] * jnp.exp(g_last[h] - gc[h])).astype(BF16), u[h]) for h in heads]
    for h in heads:
        s_ref[h] = jnp.exp(g_last[h]) * s[h] + s_new[h]
        z = pd_ref[:, col(conv_dim, dv, h)]
        o_ref[:, col(0, dv, h)] = (_rms(o[h], g_out) * _silu(z)).astype(o_ref.dtype)

    @pl.when(c == pl.num_programs(1) - 1)
    def _():
        st_ref[0] = s_ref[...]


def gdn_prompt(pd, ps, cw_3, alog_3, dtb_3, go_3, layer, o_all, *, batch, seq, col0):
    m, pd_cols = pd.shape
    dv = go_3.shape[2]
    dk = dv
    width = GDN_HEADS * dv
    conv_dim = cw_3.shape[2]
    nc = seq // CHUNK
    row = lambda b, c: (b * nc + c, 0)
    cb = col0 // width
    lane_spec = pl.BlockSpec((None, 1, SMALL_COLS), lambda b, c: (layer, 0, 0))
    return pl.pallas_call(
        functools.partial(_gdn_prompt_kernel, dk=dk, dv=dv),
        grid=(batch, nc),
        in_specs=[pl.BlockSpec((CHUNK, pd_cols), row),
                  pl.BlockSpec((CHUNK, SMALL_COLS), row),
                  pl.BlockSpec((None, CONV_W, conv_dim), lambda b, c: (layer, 0, 0)),
                  lane_spec, lane_spec,
                  pl.BlockSpec((None, 1, dv), lambda b, c: (layer, 0, 0)),
                  pl.BlockSpec(memory_space=pl.ANY)],
        out_specs=[pl.BlockSpec((CHUNK, width), lambda b, c: (b * nc + c, cb)),
                   pl.BlockSpec((1, GDN_HEADS, dk, dv), lambda b, c: (b, 0, 0, 0))],
        out_shape=[jax.ShapeDtypeStruct(o_all.shape, BF16),
                   jax.ShapeDtypeStruct((batch, GDN_HEADS, dk, dv), F32)],
        scratch_shapes=[pltpu.VMEM((CHUNK + 8, conv_dim), F32),
                        pltpu.VMEM((GDN_HEADS, dk, dv), F32)],
        input_output_aliases={6: 0},
        compiler_params=_cparams(2),
        name="gdn_prompt",
    )(pd, ps, cw_3, alog_3, dtb_3, go_3, o_all)


def _gdn_sample_kernel(pd_ref, ps_ref, cw_ref, alog_ref, dtb_ref, go_ref, cs_ref, st_ref,
                       o_in_ref, cs_in_ref, st_in_ref, o_ref, cso_ref, sto_ref, *, dk, dv):
    del o_in_ref, cs_in_ref, st_in_ref
    kw = GDN_HEADS * dk
    conv_dim = 2 * kw + GDN_HEADS * dv
    new = pd_ref[:, 0:conv_dim]
    conv = new * cw_ref[CONV_W - 1:CONV_W, :]
    for j in range(CONV_W - 1):
        conv = conv + cs_ref[j] * cw_ref[j:j + 1, :]
    conv = _silu(conv)
    for j in range(CONV_W - 2):
        cso_ref[j] = cs_ref[j + 1]
    cso_ref[CONV_W - 2] = new

    g_all, beta_all = _gdn_gates(ps_ref[...], alog_ref, dtb_ref)
    a_all = jnp.exp(g_all)
    g_out = go_ref[...]
    for h in range(GDN_HEADS):
        q_c = _columns(_l2norm(conv[:, h * dk:(h + 1) * dk]) * (dk ** -0.5))
        k_c = _columns(_l2norm(conv[:, kw + h * dk: kw + (h + 1) * dk]))
        v = conv[:, 2 * kw + h * dv: 2 * kw + (h + 1) * dv]
        z = pd_ref[:, conv_dim + h * dv: conv_dim + (h + 1) * dv]
        rows = []
        for s in range(MIX_ROWS):
            a = a_all[s:s + 1, DA_LANE + h: DA_LANE + h + 1]
            beta = beta_all[s:s + 1, DB_LANE + h: DB_LANE + h + 1]
            kc = k_c[:, s:s + 1]
            st = st_ref[s, h] * a
            u = beta * (v[s:s + 1, :] - jnp.sum(kc * st, axis=0, keepdims=True))
            st = st + kc * u
            sto_ref[s, h] = st
            rows.append(jnp.sum(q_c[:, s:s + 1] * st, axis=0, keepdims=True))
        o = jnp.concatenate(rows, axis=0)
        o_ref[:, h * dv:(h + 1) * dv] = (_rms(o, g_out) * _silu(z)).astype(o_ref.dtype)


def gdn_sample(pd, ps, cw_3, alog_3, dtb_3, go_3, conv3, state5, layer, o_all, cs_all, st_all,
               *, row0, n_samples, col0):
    m, pd_cols = pd.shape
    dv = go_3.shape[2]
    dk = dv
    width = GDN_HEADS * dv
    conv_dim = cw_3.shape[2]
    blk0 = row0 // MIX_ROWS
    row = lambda i: (blk0 + i, 0)
    cb = col0 // width
    lane_spec = pl.BlockSpec((None, 1, SMALL_COLS), lambda i: (layer, 0, 0))
    cs_spec = pl.BlockSpec((None, CONV_W - 1, MIX_ROWS, conv_dim), lambda i: (layer, 0, i, 0))
    st_spec = pl.BlockSpec((None, MIX_ROWS, GDN_HEADS, dk, dv), lambda i: (layer, i, 0, 0, 0))
    any_spec = pl.BlockSpec(memory_space=pl.ANY)
    in_specs = [pl.BlockSpec((MIX_ROWS, pd_cols), row),
                pl.BlockSpec((MIX_ROWS, SMALL_COLS), row),
                pl.BlockSpec((None, CONV_W, conv_dim), lambda i: (layer, 0, 0)),
                lane_spec, lane_spec,
                pl.BlockSpec((None, 1, dv), lambda i: (layer, 0, 0)),
                cs_spec, st_spec, any_spec]
    args = [pd, ps, cw_3, alog_3, dtb_3, go_3, conv3, state5, o_all]
    aliases = {8: 0}
    kern = functools.partial(_gdn_sample_kernel, dk=dk, dv=dv)
    if st_all is not None:
        in_specs += [any_spec, any_spec]
        args += [cs_all, st_all]
        aliases[9] = 1
        aliases[10] = 2
    else:
        kern = functools.partial(_no_state_in, functools.partial(_no_state_in, kern, 9), 9)
    return pl.pallas_call(
        kern,
        grid=(n_samples // MIX_ROWS,),
        in_specs=in_specs,
        out_specs=[pl.BlockSpec((MIX_ROWS, width), lambda i: (blk0 + i, cb)), cs_spec, st_spec],
        out_shape=[jax.ShapeDtypeStruct(o_all.shape, BF16),
                   jax.ShapeDtypeStruct(conv3.shape, F32),
                   jax.ShapeDtypeStruct(state5.shape, F32)],
        input_output_aliases=aliases,
        compiler_params=_cparams(1),
        name="gdn_sample",
    )(*args)


def _row_tile(m, cap):
    best = 16
    for t in range(16, cap + 1, 16):
        if m % t == 0:
            best = t
    return best


def kernel(x_prompt, x_sample, mem_prompt, cache_mem_k, cache_mem_v, state_gla, state_gdn, state_conv, g_mix, w_in, gla_w_gate2, gla_b_gate, gla_g_out, gdn_conv_w, gdn_a_log, gdn_dt_bias, gdn_g_out, w_out, g_xattn, g_mem, xa_w_q, xa_w_k, xa_w_v, xa_w_o, g_ffn, ffn_w_gate, ffn_w_up, ffn_w_down, g_final):
    batch, seq, d = x_prompt.shape
    n_s = x_sample.shape[0]
    depth = w_in.shape[0]
    n_mem = mem_prompt.shape[1]
    mp = batch * seq
    m = mp + n_s
    gla_kw = gla_w_gate2.shape[2]
    gla_dv = gla_g_out.shape[1]
    gla_width = GLA_HEADS * gla_dv
    gdn_dv = gdn_g_out.shape[1]
    gdn_width = GDN_HEADS * gdn_dv
    conv_dim = gdn_conv_w.shape[2]
    n_gla = 2 * gla_kw + 2 * gla_width
    c_code = n_gla
    c_conv = c_code + GLA_GATE_RANK
    n_gdn = conv_dim + gdn_width
    c_da = c_conv + n_gdn

    tm = _row_tile(m, 1088)
    tm_mem = _row_tile(batch * n_mem, 1024)

    row3 = lambda p: p.reshape(depth, 1, p.shape[-1])
    lane_row = lambda p: jnp.pad(p, ((0, 0), (DA_LANE, SMALL_COLS - DA_LANE - GDN_HEADS))).reshape(depth, 1, SMALL_COLS)
    g_mix3, g_xattn3, g_mem3, g_ffn3 = row3(g_mix), row3(g_xattn), row3(g_mem), row3(g_ffn)
    g_final3 = g_final.reshape(1, 1, d)
    bg3, gla_go3, gdn_go3 = row3(gla_b_gate), row3(gla_g_out), row3(gdn_g_out)
    alog3, dtb3 = lane_row(gdn_a_log), lane_row(gdn_dt_bias)

    w_in_t = jnp.swapaxes(w_in, 1, 2)
    w_down = ffn_w_down.astype(BF16)

    conv3 = jnp.swapaxes(state_conv, 1, 2)
    mem2 = mem_prompt.reshape(batch * n_mem, d)
    cache_k4 = cache_token_rows(cache_mem_k)
    cache_v4 = cache_token_rows(cache_mem_v)

    x = jnp.concatenate([x_prompt.reshape(mp, d), x_sample.reshape(n_s, d)], axis=0)

    p_gla, p_gdn, p_conv, p_mk, p_mv = [], [], [], [], []
    s_gla_all = s_gdn_all = s_conv_all = None
    for l in range(depth):
        h = rmsnorm_rows(x, g_mix3, l, BF16, tm)
        pg = matmul(h, w_in_t, l, tm=tm, tn=1024, col0=0, n_cols=n_gla, transposed=True, name="proj_gla")
        pd = matmul(h, w_in_t, l, tm=tm, tn=1024, col0=c_conv, n_cols=n_gdn, transposed=True, name="proj_gdn")
        ps = proj_small(h, w_in_t, l, tm=tm, row_a=c_code, row_b=c_da)

        o, sa = gla_prompt(pg, ps, gla_w_gate2, bg3, gla_go3, l, batch=batch, seq=seq, d_model=d)
        o, sb = gdn_prompt(pd, ps, gdn_conv_w, alog3, dtb3, gdn_go3, l, o, batch=batch, seq=seq, col0=gla_width)
        o, s_gla_all = gla_sample(pg, ps, gla_w_gate2, bg3, gla_go3, state_gla, l, o, s_gla_all,
                                  row0=mp, n_samples=n_s)
        o, s_conv_all, s_gdn_all = gdn_sample(pd, ps, gdn_conv_w, alog3, dtb3, gdn_go3, conv3, state_gdn, l,
                                              o, s_conv_all, s_gdn_all, row0=mp, n_samples=n_s, col0=gla_width)
        p_gla.append(sa)
        p_gdn.append(sb)
        p_conv.append(jnp.stack([pd[(b + 1) * seq - (CONV_W - 1):(b + 1) * seq, :conv_dim] for b in range(batch)]))
        x = matmul(o, w_out, l, tm=tm, tn=1024, res=x, name="w_out")

        mem_n = rmsnorm_rows(mem2, g_mem3, l, BF16, tm_mem)
        mk = matmul(mem_n, xa_w_k, l, tm=tm_mem, tn=1024, name="mem_k")
        mv = matmul(mem_n, xa_w_v, l, tm=tm_mem, tn=1024, name="mem_v")
        p_mk.append(mk)
        p_mv.append(mv)
        h = rmsnorm_rows(x, g_xattn3, l, BF16, tm)
        qx = matmul(h, xa_w_q, l, tm=tm, tn=1024, out_dtype=BF16, name="xa_q")
        ox = xattn_prompt(qx, mk, mv, batch=batch, seq=seq, n_mem=n_mem, tq=min(seq, 1024))
        ox = xattn_sample(qx, cache_k4, cache_v4, l, ox, row0=mp, n_samples=n_s)
        x = matmul(ox, xa_w_o, l, tm=tm, tn=1024, res=x, name="xa_o")

        h = rmsnorm_rows(x, g_ffn3, l, BF16, tm)
        ff = swiglu(h, ffn_w_gate, ffn_w_up, l, tm=tm, tn=512)
        x = matmul(ff, w_down, l, tm=tm, tn=512, res=x, name="ffn_down")

    y_prompt = rmsnorm_rows(x, g_final3, 0, F32, _row_tile(mp, 1024), row0=0, n_rows=mp)
    y_sample = rmsnorm_rows(x, g_final3, 0, F32, n_s, row0=mp, n_rows=n_s)
    xa_shape = (depth, batch, n_mem, XA_HEADS, d // XA_HEADS)
    return (y_prompt.reshape(batch, seq, d),
            y_sample.reshape(n_s, 1, d),
            jnp.stack(p_gla),
            jnp.stack(p_gdn),
            jnp.stack(p_conv),
            jnp.stack(p_mk).reshape(xa_shape),
            jnp.stack(p_mv).reshape(xa_shape),
            s_gla_all,
            s_gdn_all,
            jnp.swapaxes(s_conv_all, 1, 2))
```

```python
import functools

import jax
import jax.numpy as jnp
from jax import lax
from jax.experimental import pallas as pl
from jax.experimental.pallas import tpu as pltpu

F32 = jnp.float32
BF16 = jnp.bfloat16

EPS = 1e-6
GLA_HEADS = 4
GLA_GATE_RANK = 16
GLA_GATE_TEMP = 16.0
GDN_HEADS = 8
CONV_W = 4
XA_HEADS = 4

V7X_VMEM_LIMIT_BYTES = 56 * 1024 * 1024
LANES = 128
CHUNK = 128
SUB = 8
SMALL_COLS = LANES
DA_LANE = 16
DB_LANE = 24


def _cparams(n_axes):
    return pltpu.CompilerParams(
        dimension_semantics=("arbitrary",) * n_axes,
        vmem_limit_bytes=V7X_VMEM_LIMIT_BYTES)


def _dot(a, b):
    return jnp.dot(a, b, preferred_element_type=F32)


def _dot_nt(a, b):
    return lax.dot_general(a, b, (((1,), (1,)), ((), ())), preferred_element_type=F32)


def _dot_tn(a, b):
    return lax.dot_general(a, b, (((0,), (0,)), ((), ())), preferred_element_type=F32)


def _dot_exact(a, b):
    return jnp.dot(a, b, preferred_element_type=F32, precision=lax.Precision.HIGHEST)


def _silu(x):
    return x * (1.0 / (1.0 + jnp.exp(-x)))


def _sigmoid(x):
    return 1.0 / (1.0 + jnp.exp(-x))


def _softplus(x):
    return jnp.maximum(x, 0.0) + jnp.log1p(jnp.exp(-jnp.abs(x)))


def _log_sigmoid(x):
    return jnp.minimum(x, 0.0) - jnp.log1p(jnp.exp(-jnp.abs(x)))


def _rms(x, g):
    return x * lax.rsqrt(jnp.mean(x * x, axis=-1, keepdims=True) + EPS) * g


def _iota2(shape, axis):
    return lax.broadcasted_iota(jnp.int32, shape, axis)


def _rmsnorm_kernel(x_ref, g_ref, o_ref):
    o_ref[...] = _rms(x_ref[...], g_ref[...]).astype(o_ref.dtype)


def rmsnorm_rows(x, g3, layer, out_dtype, tm, row0=0, n_rows=None):
    m, d = x.shape
    n_rows = m if n_rows is None else n_rows
    blk0 = row0 // tm
    return pl.pallas_call(
        _rmsnorm_kernel,
        grid=(n_rows // tm,),
        in_specs=[pl.BlockSpec((tm, d), lambda i: (blk0 + i, 0)),
                  pl.BlockSpec((None, 1, d), lambda i: (layer, 0, 0))],
        out_specs=pl.BlockSpec((tm, d), lambda i: (i, 0)),
        out_shape=jax.ShapeDtypeStruct((n_rows, d), out_dtype),
        compiler_params=_cparams(1),
        name="rmsnorm",
    )(x, g3)


CAST_ROWS = 256


def _stage_weight(w_ref, wb_ref, gain_ref=None, gain_is_row=False):
    rows = w_ref.shape[0]

    def body(c, carry):
        r = pl.multiple_of(c * CAST_ROWS, CAST_ROWS)
        blk = w_ref[pl.ds(r, CAST_ROWS), :]
        if gain_ref is not None:
            blk = blk * (gain_ref[...] if gain_is_row else gain_ref[pl.ds(r, CAST_ROWS), :])
        wb_ref[pl.ds(r, CAST_ROWS), :] = blk.astype(BF16)
        return carry

    lax.fori_loop(0, rows // CAST_ROWS, body, 0)


def _row_rsqrt(ss_ref, d):
    tot = ss_ref[:, 0:1]
    for j in range(1, ss_ref.shape[1] // LANES):
        tot = tot + ss_ref[:, j * LANES:j * LANES + 1]
    return lax.rsqrt(tot * (1.0 / d) + EPS)


def _sum_squares(x):
    return jnp.broadcast_to(jnp.sum(x * x, axis=-1, keepdims=True), (x.shape[0], LANES))


def _mm_kernel(*refs, has_res, stage, transposed, has_norm, emit):
    it = iter(refs)
    a_ref, w_ref = next(it), next(it)
    if transposed:
        w_ref = w_ref.at[0]
    ss_ref, gain_ref = (next(it), next(it)) if has_norm else (None, None)
    res_ref = next(it) if has_res else None
    o_ref = next(it)
    ob_ref, sso_ref = (next(it), next(it)) if emit else (None, None)
    if stage:
        wb_ref = next(it)

        @pl.when(pl.program_id(1) == 0)
        def _():
            _stage_weight(w_ref, wb_ref, gain_ref, gain_is_row=transposed)

        w = wb_ref[...]
    else:
        w = w_ref[...]
    acc = _dot_nt(a_ref[...], w) if transposed else _dot(a_ref[...], w)
    if has_norm:
        acc = acc * _row_rsqrt(ss_ref, a_ref.shape[1])
    if has_res:
        acc = res_ref[...] + acc
    o_ref[...] = acc.astype(o_ref.dtype)
    if emit:
        ob_ref[...] = acc.astype(BF16)
        sso_ref[...] = _sum_squares(acc)


def matmul(a, w3, layer, *, tm, tn, col0=0, n_cols=None, res=None, out_dtype=F32, transposed=False,
           norm=None, emit_norm_inputs=False, name="matmul"):
    m, k = a.shape
    n_total = w3.shape[1] if transposed else w3.shape[2]
    n_cols = n_total if n_cols is None else n_cols
    n_tiles = n_cols // tn
    assert m % tm == 0 and n_cols % tn == 0
    stage = w3.dtype != BF16
    assert not stage or (tn if transposed else k) % CAST_ROWS == 0
    assert norm is None or stage
    if transposed:
        assert col0 % 8 == 0
        w_spec = pl.BlockSpec((pl.Element(1), pl.Element(tn), pl.Element(k)),
                              lambda j, i: (layer, pl.multiple_of(col0 + j * tn, 8), 0))
        wb_shape = (tn, k)
    else:
        assert col0 % tn == 0
        cb0 = col0 // tn
        w_spec = pl.BlockSpec((None, k, tn), lambda j, i: (layer, 0, cb0 + j))
        wb_shape = (k, tn)
    in_specs = [pl.BlockSpec((tm, k), lambda j, i: (i, 0)), w_spec]
    args = [a, w3]
    if norm is not None:
        ss, gain3 = norm
        in_specs.append(pl.BlockSpec((tm, ss.shape[1]), lambda j, i: (i, 0)))
        in_specs.append(pl.BlockSpec((None,) + gain3.shape[1:], lambda j, i: (layer, 0, 0)))
        args += [ss, gain3]
    if res is not None:
        in_specs.append(pl.BlockSpec((tm, tn), lambda j, i: (i, j)))
        args.append(res)
    tile = pl.BlockSpec((tm, tn), lambda j, i: (i, j))
    out_specs, out_shape = tile, jax.ShapeDtypeStruct((m, n_cols), out_dtype)
    if emit_norm_inputs:
        out_specs = [tile, tile, pl.BlockSpec((tm, LANES), lambda j, i: (i, j))]
        out_shape = [out_shape, jax.ShapeDtypeStruct((m, n_cols), BF16),
                     jax.ShapeDtypeStruct((m, LANES * n_tiles), F32)]
    return pl.pallas_call(
        functools.partial(_mm_kernel, has_res=res is not None, stage=stage, transposed=transposed,
                          has_norm=norm is not None, emit=emit_norm_inputs),
        grid=(n_tiles, m // tm),
        in_specs=in_specs,
        out_specs=out_specs,
        out_shape=out_shape,
        scratch_shapes=[pltpu.VMEM(wb_shape, BF16)] if stage else [],
        compiler_params=_cparams(2),
        name=name,
    )(*args)


def _norm_inputs_kernel(x_ref, xb_ref, ss_ref):
    x = x_ref[...]
    xb_ref[...] = x.astype(BF16)
    ss_ref[...] = _sum_squares(x)


def norm_inputs(x, tm):
    m, d = x.shape
    return pl.pallas_call(
        _norm_inputs_kernel,
        grid=(m // tm,),
        in_specs=[pl.BlockSpec((tm, d), lambda i: (i, 0))],
        out_specs=[pl.BlockSpec((tm, d), lambda i: (i, 0)), pl.BlockSpec((tm, LANES), lambda i: (i, 0))],
        out_shape=[jax.ShapeDtypeStruct((m, d), BF16), jax.ShapeDtypeStruct((m, LANES), F32)],
        compiler_params=_cparams(1),
        name="norm_inputs",
    )(x)


SMALL_PIECE = 16


def _proj_small_kernel(a_ref, wa_ref, wb_ref, ss_ref, gain_ref, o_ref, w_ref):
    @pl.when(pl.program_id(0) == 0)
    def _():
        w_ref[...] = jnp.zeros_like(w_ref)
        w_ref[0:SMALL_PIECE, :] = (wa_ref[...] * gain_ref[...]).astype(BF16)
        w_ref[SMALL_PIECE:2 * SMALL_PIECE, :] = (wb_ref[...] * gain_ref[...]).astype(BF16)

    o_ref[...] = _dot_nt(a_ref[...], w_ref[...]) * _row_rsqrt(ss_ref, a_ref.shape[1])


def proj_small(a, wt3, layer, *, tm, row_a, row_b, norm):
    m, k = a.shape
    ss, gain3 = norm
    assert row_a % SMALL_PIECE == 0 and row_b % SMALL_PIECE == 0
    piece = lambda r: pl.BlockSpec((None, SMALL_PIECE, k), lambda i: (layer, r // SMALL_PIECE, 0))
    return pl.pallas_call(
        _proj_small_kernel,
        grid=(m // tm,),
        in_specs=[pl.BlockSpec((tm, k), lambda i: (i, 0)), piece(row_a), piece(row_b),
                  pl.BlockSpec((tm, ss.shape[1]), lambda i: (i, 0)),
                  pl.BlockSpec((None, 1, k), lambda i: (layer, 0, 0))],
        out_specs=pl.BlockSpec((tm, SMALL_COLS), lambda i: (i, 0)),
        out_shape=jax.ShapeDtypeStruct((m, SMALL_COLS), F32),
        scratch_shapes=[pltpu.VMEM((SMALL_COLS, k), BF16)],
        compiler_params=_cparams(1),
        name="proj_small",
    )(a, wt3, wt3, ss, gain3)


def _swiglu_kernel(a_ref, wg_ref, wu_ref, ss_ref, gain_ref, o_ref, wgb_ref, wub_ref):
    @pl.when(pl.program_id(1) == 0)
    def _():
        _stage_weight(wg_ref, wgb_ref, gain_ref)
        _stage_weight(wu_ref, wub_ref, gain_ref)

    a = a_ref[...]
    r = _row_rsqrt(ss_ref, a_ref.shape[1])
    gate = _dot(a, wgb_ref[...]) * r
    up = _dot(a, wub_ref[...]) * r
    o_ref[...] = (_silu(gate) * up).astype(o_ref.dtype)


def swiglu(a, wg3, wu3, layer, *, tm, tn, norm):
    m, k = a.shape
    f = wg3.shape[2]
    ss, gain3 = norm
    assert m % tm == 0 and f % tn == 0 and k % CAST_ROWS == 0
    wspec = pl.BlockSpec((None, k, tn), lambda j, i: (layer, 0, j))
    return pl.pallas_call(
        _swiglu_kernel,
        grid=(f // tn, m // tm),
        in_specs=[pl.BlockSpec((tm, k), lambda j, i: (i, 0)), wspec, wspec,
                  pl.BlockSpec((tm, ss.shape[1]), lambda j, i: (i, 0)),
                  pl.BlockSpec((None, k, 1), lambda j, i: (layer, 0, 0))],
        out_specs=pl.BlockSpec((tm, tn), lambda j, i: (i, j)),
        out_shape=jax.ShapeDtypeStruct((m, f), BF16),
        scratch_shapes=[pltpu.VMEM((k, tn), BF16), pltpu.VMEM((k, tn), BF16)],
        compiler_params=_cparams(2),
        name="swiglu",
    )(a, wg3, wu3, ss, gain3)


def _xattn_prompt_kernel(q_ref, k_ref, v_ref, o_ref, *, scale, hd):
    heads = [slice(h * hd, (h + 1) * hd) for h in range(XA_HEADS)]
    s = [_dot_nt(q_ref[:, c], k_ref[:, c].astype(BF16)) * scale for c in heads]
    e = [jnp.exp(x - jnp.max(x, axis=-1, keepdims=True)) for x in s]
    p = [(x / jnp.sum(x, axis=-1, keepdims=True)).astype(BF16) for x in e]
    for x, c in zip(p, heads):
        o_ref[:, c] = _dot(x, v_ref[:, c].astype(BF16)).astype(o_ref.dtype)


def xattn_prompt(qx, mk, mv, *, batch, seq, n_mem, tq):
    m, d = qx.shape
    nt = seq // tq
    qspec = pl.BlockSpec((tq, d), lambda b, t: (b * nt + t, 0))
    kvspec = pl.BlockSpec((n_mem, d), lambda b, t: (b, 0))
    return pl.pallas_call(
        functools.partial(_xattn_prompt_kernel, scale=(d // XA_HEADS) ** -0.5, hd=d // XA_HEADS),
        grid=(batch, nt),
        in_specs=[qspec, kvspec, kvspec],
        out_specs=qspec,
        out_shape=jax.ShapeDtypeStruct((m, d), BF16),
        compiler_params=_cparams(2),
        name="xattn_prompt",
    )(qx, mk, mv)


XS_ROWS = 16
XS_STEP = 2


def _xattn_sample_kernel(q_ref, k_ref, v_ref, o_in_ref, o_ref, qf_ref, of_ref, *, scale, hd):
    del o_in_ref
    j = pl.program_id(1)

    @pl.when(j == 0)
    def _():
        qf_ref[...] = q_ref[...].astype(F32)

    nj = hd // LANES
    per_tok = nj * XA_HEADS
    half = per_tok // 2
    n_mem = k_ref.shape[1] // per_tok
    lanes = lambda h, jj: slice(h * hd + jj * LANES, h * hd + (jj + 1) * LANES)
    for t in range(XS_STEP):
        r = j * XS_STEP + t
        q_row = qf_ref[pl.ds(r, 1), :]
        q_tok = jnp.concatenate([q_row[:, lanes(h, jj)] for jj in range(nj) for h in range(XA_HEADS)], axis=0)
        prod = k_ref[t].reshape(n_mem, per_tok, LANES) * q_tok[None]
        part = prod[:, 0:half, :] + prod[:, half:per_tok, :]
        part = part + pltpu.roll(part, XA_HEADS, axis=1)
        s = jnp.sum(part, axis=-1, keepdims=True) * scale
        e = jnp.exp(s - jnp.max(s, axis=0, keepdims=True))
        p = e / jnp.sum(e, axis=0, keepdims=True)
        v3 = v_ref[t].reshape(n_mem, per_tok, LANES)
        acc = [jnp.sum(v3[:, g * half:(g + 1) * half, :] * p, axis=0) for g in range(2)]
        pieces = []
        for h in range(XA_HEADS):
            for jj in range(nj):
                row = (jj % 2) * XA_HEADS + h
                pieces.append(acc[jj // 2][row:row + 1, :])
        of_ref[pl.ds(r, 1), :] = jnp.concatenate(pieces, axis=1)

    @pl.when(j == pl.num_programs(1) - 1)
    def _():
        o_ref[...] = of_ref[...].astype(o_ref.dtype)


def cache_token_rows(cache):
    depth, s, n_mem, heads, hd = cache.shape
    nj = hd // LANES
    c6 = cache.reshape(depth, s, n_mem, heads, nj, LANES)
    return jnp.transpose(c6, (0, 1, 2, 4, 3, 5)).reshape(depth, s, n_mem * nj * heads, LANES)


def xattn_sample(qx, cache_k4, cache_v4, layer, o_all, *, row0, n_samples):
    m, d = qx.shape
    tok_rows = cache_k4.shape[2]
    hd = d // XA_HEADS
    assert (hd // LANES) * XA_HEADS == 16 and XA_HEADS == 4
    blk0 = row0 // XS_ROWS
    inner = XS_ROWS // XS_STEP
    qspec = pl.BlockSpec((XS_ROWS, d), lambda i, j: (blk0 + i, 0))
    cspec = pl.BlockSpec((None, XS_STEP, tok_rows, LANES), lambda i, j: (layer, i * inner + j, 0, 0))
    return pl.pallas_call(
        functools.partial(_xattn_sample_kernel, scale=hd ** -0.5, hd=hd),
        grid=(n_samples // XS_ROWS, inner),
        in_specs=[qspec, cspec, cspec, pl.BlockSpec(memory_space=pl.ANY)],
        out_specs=qspec,
        out_shape=jax.ShapeDtypeStruct((m, d), BF16),
        scratch_shapes=[pltpu.VMEM((XS_ROWS, d), F32), pltpu.VMEM((XS_ROWS, d), F32)],
        input_output_aliases={3: 0},
        compiler_params=_cparams(2),
        name="xattn_sample",
    )(qx, cache_k4, cache_v4, o_all)


def _gla_log_decay(code, w2_ref, bg_ref):
    pre = _dot_exact(code, w2_ref[...]) + bg_ref[...]
    return _log_sigmoid(pre) * (1.0 / GLA_GATE_TEMP)


def _gla_prompt_kernel(pg_ref, ps_ref, w2_ref, bg_ref, go_ref, o_ref, st_ref, stt_ref, *, dk, dv):
    c = pl.program_id(1)
    C = CHUNK
    kw = GLA_HEADS * dk

    @pl.when(c == 0)
    def _():
        stt_ref[...] = jnp.zeros_like(stt_ref)

    la = _gla_log_decay(ps_ref[:, 0:GLA_GATE_RANK], w2_ref, bg_ref)
    tri = (_iota2((C, C), 0) >= _iota2((C, C), 1)).astype(F32)
    b_all = _dot_exact(tri, la)
    row_s = _iota2((SUB, dk), 0)
    lane_a = _iota2((SUB, C), 1)
    ones_k = jnp.ones((dk, LANES), BF16)
    g_out = go_ref[...]

    heads = range(GLA_HEADS)
    col = lambda base, width, h: slice(base + h * width, base + (h + 1) * width)
    q = [pg_ref[:, col(0, dk, h)] * (dk ** -0.5) for h in heads]
    k = [pg_ref[:, col(kw, dk, h)] for h in heads]
    vb = [pg_ref[:, col(2 * kw, dv, h)].astype(BF16) for h in heads]
    b = [b_all[:, col(0, dk, h)] for h in heads]
    b_last = [bh[C - 1:C, :] for bh in b]

    s_t = [stt_ref[h] for h in heads]
    o = [_dot_nt((q[h] * jnp.exp(b[h])).astype(BF16), s_t[h].astype(BF16)) for h in heads]
    upd = [_dot_tn(vb[h], (k[h] * jnp.exp(b_last[h] - b[h])).astype(BF16)) for h in heads]
    for h in heads:
        stt_ref[h] = s_t[h] * jnp.exp(b_last[h]) + upd[h]

    n_sub = C // SUB
    earlier = [[None] for _ in heads]
    for i in range(1, n_sub):
        lo = i * SUB
        for h in heads:
            b_ref = b[h][lo - 1:lo, :]
            q_rel = (q[h][lo:lo + SUB] * jnp.exp(b[h][lo:lo + SUB] - b_ref)).astype(BF16)
            k_rel = jnp.concatenate([k[h][0:lo] * jnp.exp(b_ref - b[h][0:lo]), jnp.zeros((C - lo, dk), F32)],
                                    axis=0).astype(BF16)
            earlier[h].append(_dot_nt(q_rel, k_rel))
    sums = []
    for h in heads:
        zs = []
        for i in range(n_sub):
            lo = i * SUB
            qi, ki, bi = q[h][lo:lo + SUB], k[h][lo:lo + SUB], b[h][lo:lo + SUB]
            for s in range(SUB):
                dec = jnp.exp(jnp.minimum(bi - bi[s:s + 1, :], 0.0))
                zs.append(jnp.where(row_s >= s, qi * ki[s:s + 1, :] * dec, 0.0))
        sums.append(_dot(jnp.concatenate(zs, axis=0).astype(BF16), ones_k))
    a_rows = [[] for _ in heads]
    for i in range(n_sub):
        lo = i * SUB
        for h in heads:
            a_i = jnp.zeros((SUB, C), F32) if i == 0 else earlier[h][i]
            for s in range(SUB):
                r0 = (i * SUB + s) * SUB
                a_i = jnp.where(lane_a == lo + s, sums[h][r0:r0 + SUB, 0:C], a_i)
            a_rows[h].append(a_i)
    o = [o[h] + _dot(jnp.concatenate(a_rows[h], axis=0).astype(BF16), vb[h]) for h in heads]

    for h in heads:
        r = pg_ref[:, col(2 * kw + GLA_HEADS * dv, dv, h)]
        o_ref[:, col(0, dv, h)] = (_rms(o[h], g_out) * _silu(r)).astype(o_ref.dtype)

    @pl.when(c == pl.num_programs(1) - 1)
    def _():
        for h in range(GLA_HEADS):
            st_ref[0, h] = stt_ref[h].T


def gla_prompt(pg, ps, w2_3, bg_3, go_3, layer, *, batch, seq, d_model):
    m, pg_cols = pg.shape
    kw = w2_3.shape[2]
    dk = kw // GLA_HEADS
    dv = go_3.shape[2]
    width = GLA_HEADS * dv
    nc = seq // CHUNK
    row = lambda b, c: (b * nc + c, 0)
    return pl.pallas_call(
        functools.partial(_gla_prompt_kernel, dk=dk, dv=dv),
        grid=(batch, nc),
        in_specs=[pl.BlockSpec((CHUNK, pg_cols), row),
                  pl.BlockSpec((CHUNK, SMALL_COLS), row),
                  pl.BlockSpec((None, GLA_GATE_RANK, kw), lambda b, c: (layer, 0, 0)),
                  pl.BlockSpec((None, 1, kw), lambda b, c: (layer, 0, 0)),
                  pl.BlockSpec((None, 1, dv), lambda b, c: (layer, 0, 0))],
        out_specs=[pl.BlockSpec((CHUNK, width), row),
                   pl.BlockSpec((1, GLA_HEADS, dk, dv), lambda b, c: (b, 0, 0, 0))],
        out_shape=[jax.ShapeDtypeStruct((m, d_model), BF16),
                   jax.ShapeDtypeStruct((batch, GLA_HEADS, dk, dv), F32)],
        scratch_shapes=[pltpu.VMEM((GLA_HEADS, dv, dk), F32)],
        compiler_params=_cparams(2),
        name="gla_prompt",
    )(pg, ps, w2_3, bg_3, go_3)


MIX_ROWS = 16


def _columns(x):
    pad = jnp.zeros((LANES - MIX_ROWS, x.shape[1]), F32)
    return jnp.concatenate([x, pad], axis=0).T


def _gla_sample_kernel(pg_ref, ps_ref, w2_ref, bg_ref, go_ref, st_ref, o_in_ref, st_in_ref,
                       o_ref, sto_ref, *, dk, dv):
    del o_in_ref, st_in_ref
    kw = GLA_HEADS * dk
    decay = jnp.exp(_gla_log_decay(ps_ref[:, 0:GLA_GATE_RANK], w2_ref, bg_ref))
    g_out = go_ref[...]
    for h in range(GLA_HEADS):
        q_c = _columns(pg_ref[:, h * dk:(h + 1) * dk] * (dk ** -0.5))
        k_c = _columns(pg_ref[:, kw + h * dk: kw + (h + 1) * dk])
        a_c = _columns(decay[:, h * dk:(h + 1) * dk])
        v = pg_ref[:, 2 * kw + h * dv: 2 * kw + (h + 1) * dv]
        r = pg_ref[:, 2 * kw + GLA_HEADS * dv + h * dv: 2 * kw + GLA_HEADS * dv + (h + 1) * dv]
        rows = []
        for s in range(MIX_ROWS):
            st = st_ref[s, h] * a_c[:, s:s + 1] + k_c[:, s:s + 1] * v[s:s + 1, :]
            sto_ref[s, h] = st
            rows.append(jnp.sum(q_c[:, s:s + 1] * st, axis=0, keepdims=True))
        o = jnp.concatenate(rows, axis=0)
        o_ref[:, h * dv:(h + 1) * dv] = (_rms(o, g_out) * _silu(r)).astype(o_ref.dtype)


def gla_sample(pg, ps, w2_3, bg_3, go_3, state5, layer, o_all, st_all, *, row0, n_samples):
    m, pg_cols = pg.shape
    kw = w2_3.shape[2]
    dk = kw // GLA_HEADS
    dv = go_3.shape[2]
    width = GLA_HEADS * dv
    blk0 = row0 // MIX_ROWS
    row = lambda i: (blk0 + i, 0)
    st_spec = pl.BlockSpec((None, MIX_ROWS, GLA_HEADS, dk, dv), lambda i: (layer, i, 0, 0, 0))
    any_spec = pl.BlockSpec(memory_space=pl.ANY)
    in_specs = [pl.BlockSpec((MIX_ROWS, pg_cols), row),
                pl.BlockSpec((MIX_ROWS, SMALL_COLS), row),
                pl.BlockSpec((None, GLA_GATE_RANK, kw), lambda i: (layer, 0, 0)),
                pl.BlockSpec((None, 1, kw), lambda i: (layer, 0, 0)),
                pl.BlockSpec((None, 1, dv), lambda i: (layer, 0, 0)),
                st_spec, any_spec]
    args = [pg, ps, w2_3, bg_3, go_3, state5, o_all]
    aliases = {6: 0}
    kern = functools.partial(_gla_sample_kernel, dk=dk, dv=dv)
    if st_all is not None:
        in_specs.append(any_spec)
        args.append(st_all)
        aliases[7] = 1
    else:
        kern = functools.partial(_no_state_in, kern, 7)
    return pl.pallas_call(
        kern,
        grid=(n_samples // MIX_ROWS,),
        in_specs=in_specs,
        out_specs=[pl.BlockSpec((MIX_ROWS, width), row), st_spec],
        out_shape=[jax.ShapeDtypeStruct(o_all.shape, BF16),
                   jax.ShapeDtypeStruct(state5.shape, F32)],
        input_output_aliases=aliases,
        compiler_params=_cparams(1),
        name="gla_sample",
    )(*args)


def _no_state_in(kern, pos, *refs):
    return kern(*refs[:pos], None, *refs[pos:])


def _gdn_gates(ps, alog_ref, dtb_ref):
    g = -jnp.exp(alog_ref[...]) * _softplus(ps + dtb_ref[...])
    return g, _sigmoid(ps)


def _l2norm(x):
    return x * lax.rsqrt(jnp.sum(x * x, axis=-1, keepdims=True) + EPS)


def _inv_unit_lower(lows):
    n = lows[0].shape[0]
    r = _iota2((n, n), 0)
    c = _iota2((n, n), 1)
    eye = (r == c).astype(F32)
    base = 8
    same = lambda size: jnp.right_shift(r, size.bit_length() - 1) == jnp.right_shift(c, size.bit_length() - 1)
    bf = lambda xs: [x.astype(BF16) for x in xs]
    in_base = same(base)
    l8 = bf([jnp.where(in_base, low, 0.0) for low in lows])
    x = [eye - a.astype(F32) for a in l8]
    p = [_dot(a, a) for a in l8]
    for step in range(2):
        pb = bf(p)
        x = [xi + _dot(xb, pi) for xi, xb, pi in zip(x, bf(x), pb)]
        if step == 0:
            p = [_dot(pi, pi) for pi in pb]
    m = base
    while m < n:
        in_2m, in_m = same(2 * m), same(m)
        off = bf([jnp.where(in_2m, jnp.where(in_m, 0.0, low), 0.0) for low in lows])
        xb = bf(x)
        y = bf([_dot(oi, xi) for oi, xi in zip(off, xb)])
        x = [xi - _dot(xbi, yi) for xi, xbi, yi in zip(x, xb, y)]
        m *= 2
    return x


def _gdn_prompt_kernel(pd_ref, ps_ref, cw_ref, alog_ref, dtb_ref, go_ref, o_in_ref, o_ref, st_ref,
                       xb_ref, s_ref, *, dk, dv):
    del o_in_ref
    c = pl.program_id(1)
    C = CHUNK
    kw = GDN_HEADS * dk
    conv_dim = 2 * kw + GDN_HEADS * dv
    halo = 8

    @pl.when(c == 0)
    def _():
        xb_ref[0:halo, :] = jnp.zeros((halo, conv_dim), F32)
        s_ref[...] = jnp.zeros_like(s_ref)

    xb_ref[halo:halo + C, :] = pd_ref[:, 0:conv_dim]
    window = xb_ref[...]
    conv = window[halo:halo + C] * cw_ref[CONV_W - 1:CONV_W, :]
    for back in range(1, CONV_W):
        tap = pltpu.roll(window, back, axis=0)[halo:halo + C]
        conv = conv + tap * cw_ref[CONV_W - 1 - back:CONV_W - back, :]
    conv = _silu(conv)
    xb_ref[0:halo, :] = window[C:C + halo]

    g_all, beta_all = _gdn_gates(ps_ref[...], alog_ref, dtb_ref)
    tri = (_iota2((C, C), 0) >= _iota2((C, C), 1)).astype(F32)
    gc_all = _dot_exact(tri, g_all)
    gr_all = gc_all.T
    r_i = _iota2((C, C), 0)
    c_i = _iota2((C, C), 1)
    incl = r_i >= c_i
    strict = r_i > c_i
    g_out = go_ref[...]

    heads = range(GDN_HEADS)
    col = lambda base, width, h: slice(base + h * width, base + (h + 1) * width)
    bf = lambda xs: [x.astype(BF16) for x in xs]
    q = [_l2norm(conv[:, col(0, dk, h)]) * (dk ** -0.5) for h in heads]
    k = [_l2norm(conv[:, col(kw, dk, h)]) for h in heads]
    v = [conv[:, col(2 * kw, dv, h)] for h in heads]
    beta = [beta_all[:, DB_LANE + h: DB_LANE + h + 1] for h in heads]
    gc = [gc_all[:, DA_LANE + h: DA_LANE + h + 1] for h in heads]
    g_last = [g[C - 1:C, :] for g in gc]
    decay = [jnp.where(incl, jnp.exp(jnp.minimum(gc[h] - gr_all[DA_LANE + h: DA_LANE + h + 1, :], 0.0)), 0.0)
             for h in heads]
    kb = [k[h] * beta[h] for h in heads]
    kbf, qbf = bf(k), bf(q)
    kk = [_dot_nt(a, b) for a, b in zip(bf(kb), kbf)]
    qk = [_dot_nt(a, b) for a, b in zip(qbf, kbf)]
    t_inv = bf(_inv_unit_lower([jnp.where(strict, kk[h] * decay[h], 0.0) for h in heads]))
    rhs = bf([jnp.concatenate([v[h] * beta[h], kb[h] * jnp.exp(gc[h])], axis=-1) for h in heads])
    sol = [_dot(t, x) for t, x in zip(t_inv, rhs)]

    s = [s_ref[h] for h in heads]
    sb = bf(s)
    ws = [_dot(sol[h][:, dv:dv + dk].astype(BF16), sb[h]) for h in heads]
    o = [_dot((q[h] * jnp.exp(gc[h])).astype(BF16), sb[h]) for h in heads]
    u = bf([sol[h][:, 0:dv] - ws[h] for h in heads])
    o = [o[h] + _dot((qk[h] * decay[h]).astype(BF16), u[h]) for h in heads]
    s_new = [_dot_tn((k[h] * jnp.exp(g_last[h] - gc[h])).astype(BF16), u[h]) for h in heads]
    for h in heads:
        s_ref[h] = jnp.exp(g_last[h]) * s[h] + s_new[h]
        z = pd_ref[:, col(conv_dim, dv, h)]
        o_ref[:, col(0, dv, h)] = (_rms(o[h], g_out) * _silu(z)).astype(o_ref.dtype)

    @pl.when(c == pl.num_programs(1) - 1)
    def _():
        st_ref[0] = s_ref[...]


def gdn_prompt(pd, ps, cw_3, alog_3, dtb_3, go_3, layer, o_all, *, batch, seq, col0):
    m, pd_cols = pd.shape
    dv = go_3.shape[2]
    dk = dv
    width = GDN_HEADS * dv
    conv_dim = cw_3.shape[2]
    nc = seq // CHUNK
    row = lambda b, c: (b * nc + c, 0)
    cb = col0 // width
    lane_spec = pl.BlockSpec((None, 1, SMALL_COLS), lambda b, c: (layer, 0, 0))
    return pl.pallas_call(
        functools.partial(_gdn_prompt_kernel, dk=dk, dv=dv),
        grid=(batch, nc),
        in_specs=[pl.BlockSpec((CHUNK, pd_cols), row),
                  pl.BlockSpec((CHUNK, SMALL_COLS), row),
                  pl.BlockSpec((None, CONV_W, conv_dim), lambda b, c: (layer, 0, 0)),
                  lane_spec, lane_spec,
                  pl.BlockSpec((None, 1, dv), lambda b, c: (layer, 0, 0)),
                  pl.BlockSpec(memory_space=pl.ANY)],
        out_specs=[pl.BlockSpec((CHUNK, width), lambda b, c: (b * nc + c, cb)),
                   pl.BlockSpec((1, GDN_HEADS, dk, dv), lambda b, c: (b, 0, 0, 0))],
        out_shape=[jax.ShapeDtypeStruct(o_all.shape, BF16),
                   jax.ShapeDtypeStruct((batch, GDN_HEADS, dk, dv), F32)],
        scratch_shapes=[pltpu.VMEM((CHUNK + 8, conv_dim), F32),
                        pltpu.VMEM((GDN_HEADS, dk, dv), F32)],
        input_output_aliases={6: 0},
        compiler_params=_cparams(2),
        name="gdn_prompt",
    )(pd, ps, cw_3, alog_3, dtb_3, go_3, o_all)


def _gdn_sample_kernel(pd_ref, ps_ref, cw_ref, alog_ref, dtb_ref, go_ref, cs_ref, st_ref,
                       o_in_ref, cs_in_ref, st_in_ref, o_ref, cso_ref, sto_ref, *, dk, dv):
    del o_in_ref, cs_in_ref, st_in_ref
    kw = GDN_HEADS * dk
    conv_dim = 2 * kw + GDN_HEADS * dv
    new = pd_ref[:, 0:conv_dim]
    conv = new * cw_ref[CONV_W - 1:CONV_W, :]
    for j in range(CONV_W - 1):
        conv = conv + cs_ref[j] * cw_ref[j:j + 1, :]
    conv = _silu(conv)
    for j in range(CONV_W - 2):
        cso_ref[j] = cs_ref[j + 1]
    cso_ref[CONV_W - 2] = new

    g_all, beta_all = _gdn_gates(ps_ref[...], alog_ref, dtb_ref)
    a_all = jnp.exp(g_all)
    g_out = go_ref[...]
    for h in range(GDN_HEADS):
        q_c = _columns(_l2norm(conv[:, h * dk:(h + 1) * dk]) * (dk ** -0.5))
        k_c = _columns(_l2norm(conv[:, kw + h * dk: kw + (h + 1) * dk]))
        v = conv[:, 2 * kw + h * dv: 2 * kw + (h + 1) * dv]
        z = pd_ref[:, conv_dim + h * dv: conv_dim + (h + 1) * dv]
        rows = []
        for s in range(MIX_ROWS):
            a = a_all[s:s + 1, DA_LANE + h: DA_LANE + h + 1]
            beta = beta_all[s:s + 1, DB_LANE + h: DB_LANE + h + 1]
            kc = k_c[:, s:s + 1]
            st = st_ref[s, h] * a
            u = beta * (v[s:s + 1, :] - jnp.sum(kc * st, axis=0, keepdims=True))
            st = st + kc * u
            sto_ref[s, h] = st
            rows.append(jnp.sum(q_c[:, s:s + 1] * st, axis=0, keepdims=True))
        o = jnp.concatenate(rows, axis=0)
        o_ref[:, h * dv:(h + 1) * dv] = (_rms(o, g_out) * _silu(z)).astype(o_ref.dtype)


def gdn_sample(pd, ps, cw_3, alog_3, dtb_3, go_3, conv3, state5, layer, o_all, cs_all, st_all,
               *, row0, n_samples, col0):
    m, pd_cols = pd.shape
    dv = go_3.shape[2]
    dk = dv
    width = GDN_HEADS * dv
    conv_dim = cw_3.shape[2]
    blk0 = row0 // MIX_ROWS
    row = lambda i: (blk0 + i, 0)
    cb = col0 // width
    lane_spec = pl.BlockSpec((None, 1, SMALL_COLS), lambda i: (layer, 0, 0))
    cs_spec = pl.BlockSpec((None, CONV_W - 1, MIX_ROWS, conv_dim), lambda i: (layer, 0, i, 0))
    st_spec = pl.BlockSpec((None, MIX_ROWS, GDN_HEADS, dk, dv), lambda i: (layer, i, 0, 0, 0))
    any_spec = pl.BlockSpec(memory_space=pl.ANY)
    in_specs = [pl.BlockSpec((MIX_ROWS, pd_cols), row),
                pl.BlockSpec((MIX_ROWS, SMALL_COLS), row),
                pl.BlockSpec((None, CONV_W, conv_dim), lambda i: (layer, 0, 0)),
                lane_spec, lane_spec,
                pl.BlockSpec((None, 1, dv), lambda i: (layer, 0, 0)),
                cs_spec, st_spec, any_spec]
    args = [pd, ps, cw_3, alog_3, dtb_3, go_3, conv3, state5, o_all]
    aliases = {8: 0}
    kern = functools.partial(_gdn_sample_kernel, dk=dk, dv=dv)
    if st_all is not None:
        in_specs += [any_spec, any_spec]
        args += [cs_all, st_all]
        aliases[9] = 1
        aliases[10] = 2
    else:
        kern = functools.partial(_no_state_in, functools.partial(_no_state_in, kern, 9), 9)
    return pl.pallas_call(
        kern,
        grid=(n_samples // MIX_ROWS,),
        in_specs=in_specs,
        out_specs=[pl.BlockSpec((MIX_ROWS, width), lambda i: (blk0 + i, cb)), cs_spec, st_spec],
        out_shape=[jax.ShapeDtypeStruct(o_all.shape, BF16),
                   jax.ShapeDtypeStruct(conv3.shape, F32),
                   jax.ShapeDtypeStruct(state5.shape, F32)],
        input_output_aliases=aliases,
        compiler_params=_cparams(1),
        name="gdn_sample",
    )(*args)


def _row_tile(m, cap):
    best = 16
    for t in range(16, cap + 1, 16):
        if m % t == 0:
            best = t
    return best


def kernel(x_prompt, x_sample, mem_prompt, cache_mem_k, cache_mem_v, state_gla, state_gdn, state_conv, g_mix, w_in, gla_w_gate2, gla_b_gate, gla_g_out, gdn_conv_w, gdn_a_log, gdn_dt_bias, gdn_g_out, w_out, g_xattn, g_mem, xa_w_q, xa_w_k, xa_w_v, xa_w_o, g_ffn, ffn_w_gate, ffn_w_up, ffn_w_down, g_final):
    batch, seq, d = x_prompt.shape
    n_s = x_sample.shape[0]
    depth = w_in.shape[0]
    n_mem = mem_prompt.shape[1]
    mp = batch * seq
    m = mp + n_s
    gla_kw = gla_w_gate2.shape[2]
    gla_dv = gla_g_out.shape[1]
    gla_width = GLA_HEADS * gla_dv
    gdn_dv = gdn_g_out.shape[1]
    gdn_width = GDN_HEADS * gdn_dv
    conv_dim = gdn_conv_w.shape[2]
    n_gla = 2 * gla_kw + 2 * gla_width
    c_code = n_gla
    c_conv = c_code + GLA_GATE_RANK
    n_gdn = conv_dim + gdn_width
    c_da = c_conv + n_gdn

    tm = _row_tile(m, 1088)
    tm_mem = _row_tile(batch * n_mem, 1024)

    row3 = lambda p: p.reshape(depth, 1, p.shape[-1])
    lane_row = lambda p: jnp.pad(p, ((0, 0), (DA_LANE, SMALL_COLS - DA_LANE - GDN_HEADS))).reshape(depth, 1, SMALL_COLS)
    g_mix3, g_mem3 = row3(g_mix), row3(g_mem)
    g_xattn_col, g_ffn_col = g_xattn.reshape(depth, d, 1), g_ffn.reshape(depth, d, 1)
    g_final3 = g_final.reshape(1, 1, d)
    bg3, gla_go3, gdn_go3 = row3(gla_b_gate), row3(gla_g_out), row3(gdn_g_out)
    alog3, dtb3 = lane_row(gdn_a_log), lane_row(gdn_dt_bias)

    w_in_t = jnp.swapaxes(w_in, 1, 2)
    w_down = ffn_w_down.astype(BF16)

    conv3 = jnp.swapaxes(state_conv, 1, 2)
    mem2 = mem_prompt.reshape(batch * n_mem, d)
    cache_k4 = cache_token_rows(cache_mem_k)
    cache_v4 = cache_token_rows(cache_mem_v)

    x = jnp.concatenate([x_prompt.reshape(mp, d), x_sample.reshape(n_s, d)], axis=0)
    xb, ss = norm_inputs(x, tm)

    p_gla, p_gdn, p_conv, p_mk, p_mv = [], [], [], [], []
    s_gla_all = s_gdn_all = s_conv_all = None
    for l in range(depth):
        mix = (ss, g_mix3)
        pg = matmul(xb, w_in_t, l, tm=tm, tn=1024, col0=0, n_cols=n_gla, transposed=True, norm=mix, name="proj_gla")
        pd = matmul(xb, w_in_t, l, tm=tm, tn=1024, col0=c_conv, n_cols=n_gdn, transposed=True, norm=mix,
                    name="proj_gdn")
        ps = proj_small(xb, w_in_t, l, tm=tm, row_a=c_code, row_b=c_da, norm=mix)

        o, sa = gla_prompt(pg, ps, gla_w_gate2, bg3, gla_go3, l, batch=batch, seq=seq, d_model=d)
        o, sb = gdn_prompt(pd, ps, gdn_conv_w, alog3, dtb3, gdn_go3, l, o, batch=batch, seq=seq, col0=gla_width)
        o, s_gla_all = gla_sample(pg, ps, gla_w_gate2, bg3, gla_go3, state_gla, l, o, s_gla_all,
                                  row0=mp, n_samples=n_s)
        o, s_conv_all, s_gdn_all = gdn_sample(pd, ps, gdn_conv_w, alog3, dtb3, gdn_go3, conv3, state_gdn, l,
                                              o, s_conv_all, s_gdn_all, row0=mp, n_samples=n_s, col0=gla_width)
        p_gla.append(sa)
        p_gdn.append(sb)
        p_conv.append(jnp.stack([pd[(b + 1) * seq - (CONV_W - 1):(b + 1) * seq, :conv_dim] for b in range(batch)]))
        x, xb, ss = matmul(o, w_out, l, tm=tm, tn=1024, res=x, emit_norm_inputs=True, name="w_out")

        mem_n = rmsnorm_rows(mem2, g_mem3, l, BF16, tm_mem)
        mk = matmul(mem_n, xa_w_k, l, tm=tm_mem, tn=1024, name="mem_k")
        mv = matmul(mem_n, xa_w_v, l, tm=tm_mem, tn=1024, name="mem_v")
        p_mk.append(mk)
        p_mv.append(mv)
        qx = matmul(xb, xa_w_q, l, tm=tm, tn=1024, out_dtype=BF16, norm=(ss, g_xattn_col), name="xa_q")
        ox = xattn_prompt(qx, mk, mv, batch=batch, seq=seq, n_mem=n_mem, tq=min(seq, 1024))
        ox = xattn_sample(qx, cache_k4, cache_v4, l, ox, row0=mp, n_samples=n_s)
        x, xb, ss = matmul(ox, xa_w_o, l, tm=tm, tn=1024, res=x, emit_norm_inputs=True, name="xa_o")

        ff = swiglu(xb, ffn_w_gate, ffn_w_up, l, tm=tm, tn=512, norm=(ss, g_ffn_col))
        if l + 1 < depth:
            x, xb, ss = matmul(ff, w_down, l, tm=tm, tn=512, res=x, emit_norm_inputs=True, name="ffn_down")
        else:
            x = matmul(ff, w_down, l, tm=tm, tn=512, res=x, name="ffn_down")

    y_prompt = rmsnorm_rows(x, g_final3, 0, F32, _row_tile(mp, 1024), row0=0, n_rows=mp)
    y_sample = rmsnorm_rows(x, g_final3, 0, F32, n_s, row0=mp, n_rows=n_s)
    xa_shape = (depth, batch, n_mem, XA_HEADS, d // XA_HEADS)
    return (y_prompt.reshape(batch, seq, d),
            y_sample.reshape(n_s, 1, d),
            jnp.stack(p_gla),
            jnp.stack(p_gdn),
            jnp.stack(p_conv),
            jnp.stack(p_mk).reshape(xa_shape),
            jnp.stack(p_mv).reshape(xa_shape),
            s_gla_all,
            s_gdn_all,
            jnp.swapaxes(s_conv_all, 1, 2))
```

```python
import functools

import jax
import jax.numpy as jnp
from jax import lax
from jax.experimental import pallas as pl
from jax.experimental.pallas import tpu as pltpu

F32 = jnp.float32
BF16 = jnp.bfloat16

EPS = 1e-6
GLA_HEADS = 4
GLA_GATE_RANK = 16
GLA_GATE_TEMP = 16.0
GDN_HEADS = 8
CONV_W = 4
XA_HEADS = 4

V7X_VMEM_LIMIT_BYTES = 56 * 1024 * 1024
LANES = 128
CHUNK = 128
SUB = 8
SMALL_COLS = LANES
DA_LANE = 16
DB_LANE = 24


def _cparams(n_axes):
    return pltpu.CompilerParams(
        dimension_semantics=("arbitrary",) * n_axes,
        vmem_limit_bytes=V7X_VMEM_LIMIT_BYTES)


def _dot(a, b):
    return jnp.dot(a, b, preferred_element_type=F32)


def _dot_nt(a, b):
    return lax.dot_general(a, b, (((1,), (1,)), ((), ())), preferred_element_type=F32)


def _dot_tn(a, b):
    return lax.dot_general(a, b, (((0,), (0,)), ((), ())), preferred_element_type=F32)


def _dot_exact(a, b):
    return jnp.dot(a, b, preferred_element_type=F32, precision=lax.Precision.HIGHEST)


def _silu(x):
    return x * (1.0 / (1.0 + jnp.exp(-x)))


def _sigmoid(x):
    return 1.0 / (1.0 + jnp.exp(-x))


def _softplus(x):
    return jnp.maximum(x, 0.0) + jnp.log1p(jnp.exp(-jnp.abs(x)))


def _log_sigmoid(x):
    return jnp.minimum(x, 0.0) - jnp.log1p(jnp.exp(-jnp.abs(x)))


def _rms(x, g):
    return x * lax.rsqrt(jnp.mean(x * x, axis=-1, keepdims=True) + EPS) * g


def _iota2(shape, axis):
    return lax.broadcasted_iota(jnp.int32, shape, axis)


def _rmsnorm_kernel(x_ref, g_ref, o_ref):
    o_ref[...] = _rms(x_ref[...], g_ref[...]).astype(o_ref.dtype)


def rmsnorm_rows(x, g3, layer, out_dtype, tm, row0=0, n_rows=None):
    m, d = x.shape
    n_rows = m if n_rows is None else n_rows
    blk0 = row0 // tm
    return pl.pallas_call(
        _rmsnorm_kernel,
        grid=(n_rows // tm,),
        in_specs=[pl.BlockSpec((tm, d), lambda i: (blk0 + i, 0)),
                  pl.BlockSpec((None, 1, d), lambda i: (layer, 0, 0))],
        out_specs=pl.BlockSpec((tm, d), lambda i: (i, 0)),
        out_shape=jax.ShapeDtypeStruct((n_rows, d), out_dtype),
        compiler_params=_cparams(1),
        name="rmsnorm",
    )(x, g3)


CAST_ROWS = 256


def _stage_weight(w_ref, wb_ref, gain_ref=None, gain_is_row=False):
    for r in range(0, w_ref.shape[0], CAST_ROWS):
        blk = w_ref[r:r + CAST_ROWS, :]
        if gain_ref is not None:
            blk = blk * (gain_ref[...] if gain_is_row else gain_ref[r:r + CAST_ROWS, :])
        wb_ref[r:r + CAST_ROWS, :] = blk.astype(BF16)


def _row_rsqrt(ss_ref, d):
    tot = ss_ref[:, 0:1]
    for j in range(1, ss_ref.shape[1] // LANES):
        tot = tot + ss_ref[:, j * LANES:j * LANES + 1]
    return lax.rsqrt(tot * (1.0 / d) + EPS)


def _sum_squares(x):
    return jnp.broadcast_to(jnp.sum(x * x, axis=-1, keepdims=True), (x.shape[0], LANES))


def _mm_kernel(*refs, has_res, stage, transposed, has_norm, emit):
    it = iter(refs)
    a_ref, w_ref = next(it), next(it)
    if transposed:
        w_ref = w_ref.at[0]
    ss_ref, gain_ref = (next(it), next(it)) if has_norm else (None, None)
    res_ref = next(it) if has_res else None
    o_ref = next(it)
    ob_ref, sso_ref = (next(it), next(it)) if emit else (None, None)
    if stage:
        wb_ref = next(it)

        @pl.when(pl.program_id(1) == 0)
        def _():
            _stage_weight(w_ref, wb_ref, gain_ref, gain_is_row=transposed)

        w = wb_ref[...]
    else:
        w = w_ref[...]
    acc = _dot_nt(a_ref[...], w) if transposed else _dot(a_ref[...], w)
    if has_norm:
        acc = acc * _row_rsqrt(ss_ref, a_ref.shape[1])
    if has_res:
        acc = res_ref[...] + acc
    o_ref[...] = acc.astype(o_ref.dtype)
    if emit:
        ob_ref[...] = acc.astype(BF16)
        sso_ref[...] = _sum_squares(acc)


def matmul(a, w3, layer, *, tm, tn, col0=0, n_cols=None, res=None, out_dtype=F32, transposed=False,
           norm=None, emit_norm_inputs=False, name="matmul"):
    m, k = a.shape
    n_total = w3.shape[1] if transposed else w3.shape[2]
    n_cols = n_total if n_cols is None else n_cols
    n_tiles = n_cols // tn
    assert m % tm == 0 and n_cols % tn == 0
    stage = w3.dtype != BF16
    assert not stage or (tn if transposed else k) % CAST_ROWS == 0
    assert norm is None or stage
    if transposed:
        assert col0 % 8 == 0
        w_spec = pl.BlockSpec((pl.Element(1), pl.Element(tn), pl.Element(k)),
                              lambda j, i: (layer, pl.multiple_of(col0 + j * tn, 8), 0))
        wb_shape = (tn, k)
    else:
        assert col0 % tn == 0
        cb0 = col0 // tn
        w_spec = pl.BlockSpec((None, k, tn), lambda j, i: (layer, 0, cb0 + j))
        wb_shape = (k, tn)
    in_specs = [pl.BlockSpec((tm, k), lambda j, i: (i, 0)), w_spec]
    args = [a, w3]
    if norm is not None:
        ss, gain3 = norm
        in_specs.append(pl.BlockSpec((tm, ss.shape[1]), lambda j, i: (i, 0)))
        in_specs.append(pl.BlockSpec((None,) + gain3.shape[1:], lambda j, i: (layer, 0, 0)))
        args += [ss, gain3]
    if res is not None:
        in_specs.append(pl.BlockSpec((tm, tn), lambda j, i: (i, j)))
        args.append(res)
    tile = pl.BlockSpec((tm, tn), lambda j, i: (i, j))
    out_specs, out_shape = tile, jax.ShapeDtypeStruct((m, n_cols), out_dtype)
    if emit_norm_inputs:
        out_specs = [tile, tile, pl.BlockSpec((tm, LANES), lambda j, i: (i, j))]
        out_shape = [out_shape, jax.ShapeDtypeStruct((m, n_cols), BF16),
                     jax.ShapeDtypeStruct((m, LANES * n_tiles), F32)]
    return pl.pallas_call(
        functools.partial(_mm_kernel, has_res=res is not None, stage=stage, transposed=transposed,
                          has_norm=norm is not None, emit=emit_norm_inputs),
        grid=(n_tiles, m // tm),
        in_specs=in_specs,
        out_specs=out_specs,
        out_shape=out_shape,
        scratch_shapes=[pltpu.VMEM(wb_shape, BF16)] if stage else [],
        compiler_params=_cparams(2),
        name=name,
    )(*args)


def _norm_inputs_kernel(x_ref, xb_ref, ss_ref):
    x = x_ref[...]
    xb_ref[...] = x.astype(BF16)
    ss_ref[...] = _sum_squares(x)


def norm_inputs(x, tm):
    m, d = x.shape
    return pl.pallas_call(
        _norm_inputs_kernel,
        grid=(m // tm,),
        in_specs=[pl.BlockSpec((tm, d), lambda i: (i, 0))],
        out_specs=[pl.BlockSpec((tm, d), lambda i: (i, 0)), pl.BlockSpec((tm, LANES), lambda i: (i, 0))],
        out_shape=[jax.ShapeDtypeStruct((m, d), BF16), jax.ShapeDtypeStruct((m, LANES), F32)],
        compiler_params=_cparams(1),
        name="norm_inputs",
    )(x)


SMALL_PIECE = 16


def _proj_small_kernel(a_ref, wa_ref, wb_ref, ss_ref, gain_ref, o_ref, w_ref):
    @pl.when(pl.program_id(0) == 0)
    def _():
        w_ref[...] = jnp.zeros_like(w_ref)
        w_ref[0:SMALL_PIECE, :] = (wa_ref[...] * gain_ref[...]).astype(BF16)
        w_ref[SMALL_PIECE:2 * SMALL_PIECE, :] = (wb_ref[...] * gain_ref[...]).astype(BF16)

    o_ref[...] = _dot_nt(a_ref[...], w_ref[...]) * _row_rsqrt(ss_ref, a_ref.shape[1])


def proj_small(a, wt3, layer, *, tm, row_a, row_b, norm):
    m, k = a.shape
    ss, gain3 = norm
    assert row_a % SMALL_PIECE == 0 and row_b % SMALL_PIECE == 0
    piece = lambda r: pl.BlockSpec((None, SMALL_PIECE, k), lambda i: (layer, r // SMALL_PIECE, 0))
    return pl.pallas_call(
        _proj_small_kernel,
        grid=(m // tm,),
        in_specs=[pl.BlockSpec((tm, k), lambda i: (i, 0)), piece(row_a), piece(row_b),
                  pl.BlockSpec((tm, ss.shape[1]), lambda i: (i, 0)),
                  pl.BlockSpec((None, 1, k), lambda i: (layer, 0, 0))],
        out_specs=pl.BlockSpec((tm, SMALL_COLS), lambda i: (i, 0)),
        out_shape=jax.ShapeDtypeStruct((m, SMALL_COLS), F32),
        scratch_shapes=[pltpu.VMEM((SMALL_COLS, k), BF16)],
        compiler_params=_cparams(1),
        name="proj_small",
    )(a, wt3, wt3, ss, gain3)


def _swiglu_kernel(a_ref, wg_ref, wu_ref, ss_ref, gain_ref, wd_ref, o_ref, wdb_ref, wgb_ref, wub_ref):
    @pl.when(pl.program_id(1) == 0)
    def _():
        _stage_weight(wg_ref, wgb_ref, gain_ref)
        _stage_weight(wu_ref, wub_ref, gain_ref)

    a = a_ref[...]
    r = _row_rsqrt(ss_ref, a_ref.shape[1])
    gate = _dot(a, wgb_ref[...]) * r
    up = _dot(a, wub_ref[...]) * r
    o_ref[...] = (_silu(gate) * up).astype(o_ref.dtype)
    wdb_ref[...] = wd_ref[...].astype(BF16)


def swiglu(a, wg3, wu3, wd3, layer, *, tm, tn, norm):
    m, k = a.shape
    f = wg3.shape[2]
    d_out = wd3.shape[2]
    ss, gain3 = norm
    n_j, n_i = f // tn, m // tm
    assert m % tm == 0 and f % tn == 0 and k % CAST_ROWS == 0 and f % (n_j * n_i * 16) == 0
    wd_rows = f // (n_j * n_i)
    wspec = pl.BlockSpec((None, k, tn), lambda j, i: (layer, 0, j))
    return pl.pallas_call(
        _swiglu_kernel,
        grid=(n_j, n_i),
        in_specs=[pl.BlockSpec((tm, k), lambda j, i: (i, 0)), wspec, wspec,
                  pl.BlockSpec((tm, ss.shape[1]), lambda j, i: (i, 0)),
                  pl.BlockSpec((None, k, 1), lambda j, i: (layer, 0, 0)),
                  pl.BlockSpec((None, wd_rows, d_out), lambda j, i: (layer, j * n_i + i, 0))],
        out_specs=[pl.BlockSpec((tm, tn), lambda j, i: (i, j)),
                   pl.BlockSpec((wd_rows, d_out), lambda j, i: (j * n_i + i, 0))],
        out_shape=[jax.ShapeDtypeStruct((m, f), BF16), jax.ShapeDtypeStruct((f, d_out), BF16)],
        scratch_shapes=[pltpu.VMEM((k, tn), BF16), pltpu.VMEM((k, tn), BF16)],
        compiler_params=_cparams(2),
        name="swiglu",
    )(a, wg3, wu3, ss, gain3, wd3)


def _xattn_prompt_kernel(q_ref, k_ref, v_ref, o_ref, *, scale, hd):
    heads = [slice(h * hd, (h + 1) * hd) for h in range(XA_HEADS)]
    s = [_dot_nt(q_ref[:, c], k_ref[:, c].astype(BF16)) * scale for c in heads]
    e = [jnp.exp(x - jnp.max(x, axis=-1, keepdims=True)) for x in s]
    p = [(x / jnp.sum(x, axis=-1, keepdims=True)).astype(BF16) for x in e]
    for x, c in zip(p, heads):
        o_ref[:, c] = _dot(x, v_ref[:, c].astype(BF16)).astype(o_ref.dtype)


def xattn_prompt(qx, mk, mv, *, batch, seq, n_mem, tq):
    m, d = qx.shape
    nt = seq // tq
    qspec = pl.BlockSpec((tq, d), lambda b, t: (b * nt + t, 0))
    kvspec = pl.BlockSpec((n_mem, d), lambda b, t: (b, 0))
    return pl.pallas_call(
        functools.partial(_xattn_prompt_kernel, scale=(d // XA_HEADS) ** -0.5, hd=d // XA_HEADS),
        grid=(batch, nt),
        in_specs=[qspec, kvspec, kvspec],
        out_specs=qspec,
        out_shape=jax.ShapeDtypeStruct((m, d), BF16),
        compiler_params=_cparams(2),
        name="xattn_prompt",
    )(qx, mk, mv)


XS_ROWS = 16
XS_STEP = 2


def _xattn_sample_kernel(q_ref, k_ref, v_ref, o_in_ref, o_ref, qf_ref, of_ref, *, scale, hd):
    del o_in_ref
    j = pl.program_id(1)

    @pl.when(j == 0)
    def _():
        qf_ref[...] = q_ref[...].astype(F32)

    nj = hd // LANES
    per_tok = nj * XA_HEADS
    half = per_tok // 2
    n_mem = k_ref.shape[1] // per_tok
    lanes = lambda h, jj: slice(h * hd + jj * LANES, h * hd + (jj + 1) * LANES)
    for t in range(XS_STEP):
        r = j * XS_STEP + t
        q_row = qf_ref[pl.ds(r, 1), :]
        q_tok = jnp.concatenate([q_row[:, lanes(h, jj)] for jj in range(nj) for h in range(XA_HEADS)], axis=0)
        prod = k_ref[t].reshape(n_mem, per_tok, LANES) * q_tok[None]
        part = prod[:, 0:half, :] + prod[:, half:per_tok, :]
        part = part + pltpu.roll(part, XA_HEADS, axis=1)
        s = jnp.sum(part, axis=-1, keepdims=True) * scale
        e = jnp.exp(s - jnp.max(s, axis=0, keepdims=True))
        p = e / jnp.sum(e, axis=0, keepdims=True)
        v3 = v_ref[t].reshape(n_mem, per_tok, LANES)
        acc = [jnp.sum(v3[:, g * half:(g + 1) * half, :] * p, axis=0) for g in range(2)]
        pieces = []
        for h in range(XA_HEADS):
            for jj in range(nj):
                row = (jj % 2) * XA_HEADS + h
                pieces.append(acc[jj // 2][row:row + 1, :])
        of_ref[pl.ds(r, 1), :] = jnp.concatenate(pieces, axis=1)

    @pl.when(j == pl.num_programs(1) - 1)
    def _():
        o_ref[...] = of_ref[...].astype(o_ref.dtype)


def cache_token_rows(cache):
    depth, s, n_mem, heads, hd = cache.shape
    nj = hd // LANES
    c6 = cache.reshape(depth, s, n_mem, heads, nj, LANES)
    return jnp.transpose(c6, (0, 1, 2, 4, 3, 5)).reshape(depth, s, n_mem * nj * heads, LANES)


def xattn_sample(qx, cache_k4, cache_v4, layer, o_all, *, row0, n_samples):
    m, d = qx.shape
    tok_rows = cache_k4.shape[2]
    hd = d // XA_HEADS
    assert (hd // LANES) * XA_HEADS == 16 and XA_HEADS == 4
    blk0 = row0 // XS_ROWS
    inner = XS_ROWS // XS_STEP
    qspec = pl.BlockSpec((XS_ROWS, d), lambda i, j: (blk0 + i, 0))
    cspec = pl.BlockSpec((None, XS_STEP, tok_rows, LANES), lambda i, j: (layer, i * inner + j, 0, 0))
    return pl.pallas_call(
        functools.partial(_xattn_sample_kernel, scale=hd ** -0.5, hd=hd),
        grid=(n_samples // XS_ROWS, inner),
        in_specs=[qspec, cspec, cspec, pl.BlockSpec(memory_space=pl.ANY)],
        out_specs=qspec,
        out_shape=jax.ShapeDtypeStruct((m, d), BF16),
        scratch_shapes=[pltpu.VMEM((XS_ROWS, d), F32), pltpu.VMEM((XS_ROWS, d), F32)],
        input_output_aliases={3: 0},
        compiler_params=_cparams(2),
        name="xattn_sample",
    )(qx, cache_k4, cache_v4, o_all)


def _gla_log_decay(code, w2_ref, bg_ref):
    pre = _dot_exact(code, w2_ref[...]) + bg_ref[...]
    return _log_sigmoid(pre) * (1.0 / GLA_GATE_TEMP)


def _gla_prompt_kernel(pg_ref, ps_ref, psn_ref, w2_ref, bg_ref, go_ref, o_ref, st_ref, stt_ref, bsc_ref, *, dk, dv):
    c = pl.program_id(1)
    C = CHUNK
    kw = GLA_HEADS * dk

    def chunk_log_decay(small_ref):
        la = _gla_log_decay(small_ref[:, 0:GLA_GATE_RANK], w2_ref, bg_ref)
        tri = (_iota2((C, C), 0) >= _iota2((C, C), 1)).astype(F32)
        return _dot_exact(tri, la)

    @pl.when(c == 0)
    def _():
        stt_ref[...] = jnp.zeros_like(stt_ref)
        bsc_ref[...] = chunk_log_decay(ps_ref)

    b_all = bsc_ref[...]
    row_s = _iota2((SUB, dk), 0)
    lane_a = _iota2((SUB, C), 1)
    ones_k = jnp.ones((dk, LANES), BF16)
    g_out = go_ref[...]

    heads = range(GLA_HEADS)
    col = lambda base, width, h: slice(base + h * width, base + (h + 1) * width)
    q = [pg_ref[:, col(0, dk, h)] * (dk ** -0.5) for h in heads]
    k = [pg_ref[:, col(kw, dk, h)] for h in heads]
    vb = [pg_ref[:, col(2 * kw, dv, h)].astype(BF16) for h in heads]
    b = [b_all[:, col(0, dk, h)] for h in heads]
    b_last = [bh[C - 1:C, :] for bh in b]

    s_t = [stt_ref[h] for h in heads]
    o = [_dot_nt((q[h] * jnp.exp(b[h])).astype(BF16), s_t[h].astype(BF16)) for h in heads]
    upd = [_dot_tn(vb[h], (k[h] * jnp.exp(b_last[h] - b[h])).astype(BF16)) for h in heads]
    for h in heads:
        stt_ref[h] = s_t[h] * jnp.exp(b_last[h]) + upd[h]

    n_sub = C // SUB
    earlier = [[None] for _ in heads]
    for i in range(1, n_sub):
        lo = i * SUB
        for h in heads:
            b_ref = b[h][lo - 1:lo, :]
            q_rel = (q[h][lo:lo + SUB] * jnp.exp(b[h][lo:lo + SUB] - b_ref)).astype(BF16)
            k_rel = jnp.concatenate([k[h][0:lo] * jnp.exp(b_ref - b[h][0:lo]), jnp.zeros((C - lo, dk), F32)],
                                    axis=0).astype(BF16)
            earlier[h].append(_dot_nt(q_rel, k_rel))
    sums = []
    for h in heads:
        zs = []
        for i in range(n_sub):
            lo = i * SUB
            qi, ki, bi = q[h][lo:lo + SUB], k[h][lo:lo + SUB], b[h][lo:lo + SUB]
            for s in range(SUB):
                dec = jnp.exp(jnp.minimum(bi - bi[s:s + 1, :], 0.0))
                zs.append(jnp.where(row_s >= s, qi * ki[s:s + 1, :] * dec, 0.0))
        sums.append(_dot(jnp.concatenate(zs, axis=0).astype(BF16), ones_k))
    a_rows = [[] for _ in heads]
    for i in range(n_sub):
        lo = i * SUB
        for h in heads:
            a_i = jnp.zeros((SUB, C), F32) if i == 0 else earlier[h][i]
            for s in range(SUB):
                r0 = (i * SUB + s) * SUB
                a_i = jnp.where(lane_a == lo + s, sums[h][r0:r0 + SUB, 0:C], a_i)
            a_rows[h].append(a_i)
    o = [o[h] + _dot(jnp.concatenate(a_rows[h], axis=0).astype(BF16), vb[h]) for h in heads]

    for h in heads:
        r = pg_ref[:, col(2 * kw + GLA_HEADS * dv, dv, h)]
        o_ref[:, col(0, dv, h)] = (_rms(o[h], g_out) * _silu(r)).astype(o_ref.dtype)

    bsc_ref[...] = chunk_log_decay(psn_ref)

    @pl.when(c == pl.num_programs(1) - 1)
    def _():
        for h in range(GLA_HEADS):
            st_ref[0, h] = stt_ref[h].T


def gla_prompt(pg, ps, w2_3, bg_3, go_3, layer, *, batch, seq, d_model):
    m, pg_cols = pg.shape
    kw = w2_3.shape[2]
    dk = kw // GLA_HEADS
    dv = go_3.shape[2]
    width = GLA_HEADS * dv
    nc = seq // CHUNK
    row = lambda b, c: (b * nc + c, 0)
    next_row = lambda b, c: (b * nc + jnp.minimum(c + 1, nc - 1), 0)
    return pl.pallas_call(
        functools.partial(_gla_prompt_kernel, dk=dk, dv=dv),
        grid=(batch, nc),
        in_specs=[pl.BlockSpec((CHUNK, pg_cols), row),
                  pl.BlockSpec((CHUNK, SMALL_COLS), row),
                  pl.BlockSpec((CHUNK, SMALL_COLS), next_row),
                  pl.BlockSpec((None, GLA_GATE_RANK, kw), lambda b, c: (layer, 0, 0)),
                  pl.BlockSpec((None, 1, kw), lambda b, c: (layer, 0, 0)),
                  pl.BlockSpec((None, 1, dv), lambda b, c: (layer, 0, 0))],
        out_specs=[pl.BlockSpec((CHUNK, width), row),
                   pl.BlockSpec((1, GLA_HEADS, dk, dv), lambda b, c: (b, 0, 0, 0))],
        out_shape=[jax.ShapeDtypeStruct((m, d_model), BF16),
                   jax.ShapeDtypeStruct((batch, GLA_HEADS, dk, dv), F32)],
        scratch_shapes=[pltpu.VMEM((GLA_HEADS, dv, dk), F32), pltpu.VMEM((CHUNK, kw), F32)],
        compiler_params=_cparams(2),
        name="gla_prompt",
    )(pg, ps, ps, w2_3, bg_3, go_3)


MIX_ROWS = 16


def _column_tiles(xs):
    parts = []
    for x in xs:
        head = x.astype(BF16).astype(F32)
        both = jnp.concatenate([head, x - head, jnp.zeros((LANES - 2 * MIX_ROWS, x.shape[1]), F32)], axis=0)
        parts.append(both.T[:, 0:2 * MIX_ROWS].astype(BF16))
    lhs = jnp.concatenate(parts, axis=0)
    sel = jnp.bitwise_and(_iota2((2 * MIX_ROWS, LANES), 0), MIX_ROWS - 1)

    def tiles(s):
        out = _dot(lhs, jnp.where(sel == s, 1.0, 0.0).astype(BF16))
        return [out[j * LANES:(j + 1) * LANES] for j in range(len(xs))]

    return tiles


def _gla_sample_kernel(pg_ref, ps_ref, w2_ref, bg_ref, go_ref, st_ref, o_in_ref, st_in_ref,
                       o_ref, sto_ref, *, dk, dv):
    del o_in_ref, st_in_ref
    kw = GLA_HEADS * dk
    decay = jnp.exp(_gla_log_decay(ps_ref[:, 0:GLA_GATE_RANK], w2_ref, bg_ref))
    g_out = go_ref[...]
    wide = lambda t: jnp.concatenate([t] * (dv // LANES), axis=1)
    for h in range(GLA_HEADS):
        tiles = _column_tiles([pg_ref[:, h * dk:(h + 1) * dk] * (dk ** -0.5),
                               pg_ref[:, kw + h * dk: kw + (h + 1) * dk],
                               decay[:, h * dk:(h + 1) * dk]])
        v = pg_ref[:, 2 * kw + h * dv: 2 * kw + (h + 1) * dv]
        r = pg_ref[:, 2 * kw + GLA_HEADS * dv + h * dv: 2 * kw + GLA_HEADS * dv + (h + 1) * dv]
        rows = []
        for s in range(MIX_ROWS):
            q_t, k_t, a_t = tiles(s)
            st = st_ref[s, h] * wide(a_t) + wide(k_t) * v[s:s + 1, :]
            sto_ref[s, h] = st
            rows.append(jnp.sum(wide(q_t) * st, axis=0, keepdims=True))
        o = jnp.concatenate(rows, axis=0)
        o_ref[:, h * dv:(h + 1) * dv] = (_rms(o, g_out) * _silu(r)).astype(o_ref.dtype)


def gla_sample(pg, ps, w2_3, bg_3, go_3, state5, layer, o_all, st_all, *, row0, n_samples):
    m, pg_cols = pg.shape
    kw = w2_3.shape[2]
    dk = kw // GLA_HEADS
    dv = go_3.shape[2]
    width = GLA_HEADS * dv
    blk0 = row0 // MIX_ROWS
    row = lambda i: (blk0 + i, 0)
    st_spec = pl.BlockSpec((None, MIX_ROWS, GLA_HEADS, dk, dv), lambda i: (layer, i, 0, 0, 0))
    any_spec = pl.BlockSpec(memory_space=pl.ANY)
    in_specs = [pl.BlockSpec((MIX_ROWS, pg_cols), row),
                pl.BlockSpec((MIX_ROWS, SMALL_COLS), row),
                pl.BlockSpec((None, GLA_GATE_RANK, kw), lambda i: (layer, 0, 0)),
                pl.BlockSpec((None, 1, kw), lambda i: (layer, 0, 0)),
                pl.BlockSpec((None, 1, dv), lambda i: (layer, 0, 0)),
                st_spec, any_spec]
    args = [pg, ps, w2_3, bg_3, go_3, state5, o_all]
    aliases = {6: 0}
    kern = functools.partial(_gla_sample_kernel, dk=dk, dv=dv)
    if st_all is not None:
        in_specs.append(any_spec)
        args.append(st_all)
        aliases[7] = 1
    else:
        kern = functools.partial(_no_state_in, kern, 7)
    return pl.pallas_call(
        kern,
        grid=(n_samples // MIX_ROWS,),
        in_specs=in_specs,
        out_specs=[pl.BlockSpec((MIX_ROWS, width), row), st_spec],
        out_shape=[jax.ShapeDtypeStruct(o_all.shape, BF16),
                   jax.ShapeDtypeStruct(state5.shape, F32)],
        input_output_aliases=aliases,
        compiler_params=_cparams(1),
        name="gla_sample",
    )(*args)


def _no_state_in(kern, pos, *refs):
    return kern(*refs[:pos], None, *refs[pos:])


def _gdn_gates(ps, alog_ref, dtb_ref):
    g = -jnp.exp(alog_ref[...]) * _softplus(ps + dtb_ref[...])
    return g, _sigmoid(ps)


def _l2norm(x):
    return x * lax.rsqrt(jnp.sum(x * x, axis=-1, keepdims=True) + EPS)


def _inv_unit_lower(lows):
    n = lows[0].shape[0]
    r = _iota2((n, n), 0)
    c = _iota2((n, n), 1)
    eye = (r == c).astype(F32)
    base = 8
    same = lambda size: jnp.right_shift(r, size.bit_length() - 1) == jnp.right_shift(c, size.bit_length() - 1)
    bf = lambda xs: [x.astype(BF16) for x in xs]
    in_base = same(base)
    l8 = bf([jnp.where(in_base, low, 0.0) for low in lows])
    x = [eye - a.astype(F32) for a in l8]
    p = [_dot(a, a) for a in l8]
    for step in range(2):
        pb = bf(p)
        x = [xi + _dot(xb, pi) for xi, xb, pi in zip(x, bf(x), pb)]
        if step == 0:
            p = [_dot(pi, pi) for pi in pb]
    m = base
    while m < n:
        in_2m, in_m = same(2 * m), same(m)
        off = bf([jnp.where(in_2m, jnp.where(in_m, 0.0, low), 0.0) for low in lows])
        xb = bf(x)
        y = bf([_dot(oi, xi) for oi, xi in zip(off, xb)])
        x = [xi - _dot(xbi, yi) for xi, xbi, yi in zip(x, xb, y)]
        m *= 2
    return x


def _gdn_prompt_kernel(pd_ref, ps_ref, psn_ref, cw_ref, alog_ref, dtb_ref, go_ref, o_in_ref, o_ref, st_ref,
                       xb_ref, s_ref, gsc_ref, *, dk, dv):
    del o_in_ref
    c = pl.program_id(1)
    C = CHUNK
    kw = GDN_HEADS * dk
    conv_dim = 2 * kw + GDN_HEADS * dv
    halo = 8

    def chunk_log_decay(small_ref):
        g_all, _ = _gdn_gates(small_ref[...], alog_ref, dtb_ref)
        tri = (_iota2((C, C), 0) >= _iota2((C, C), 1)).astype(F32)
        return _dot_exact(tri, g_all)

    @pl.when(c == 0)
    def _():
        xb_ref[0:halo, :] = jnp.zeros((halo, conv_dim), F32)
        s_ref[...] = jnp.zeros_like(s_ref)
        gsc_ref[...] = chunk_log_decay(ps_ref)

    xb_ref[halo:halo + C, :] = pd_ref[:, 0:conv_dim]
    window = xb_ref[...]
    conv = window[halo:halo + C] * cw_ref[CONV_W - 1:CONV_W, :]
    for back in range(1, CONV_W):
        tap = pltpu.roll(window, back, axis=0)[halo:halo + C]
        conv = conv + tap * cw_ref[CONV_W - 1 - back:CONV_W - back, :]
    conv = _silu(conv)
    xb_ref[0:halo, :] = window[C:C + halo]

    beta_all = _sigmoid(ps_ref[...])
    gc_all = gsc_ref[...]
    gr_all = gc_all.T
    r_i = _iota2((C, C), 0)
    c_i = _iota2((C, C), 1)
    incl = r_i >= c_i
    strict = r_i > c_i
    g_out = go_ref[...]

    heads = range(GDN_HEADS)
    col = lambda base, width, h: slice(base + h * width, base + (h + 1) * width)
    bf = lambda xs: [x.astype(BF16) for x in xs]
    q = [_l2norm(conv[:, col(0, dk, h)]) * (dk ** -0.5) for h in heads]
    k = [_l2norm(conv[:, col(kw, dk, h)]) for h in heads]
    v = [conv[:, col(2 * kw, dv, h)] for h in heads]
    beta = [beta_all[:, DB_LANE + h: DB_LANE + h + 1] for h in heads]
    gc = [gc_all[:, DA_LANE + h: DA_LANE + h + 1] for h in heads]
    g_last = [g[C - 1:C, :] for g in gc]
    decay = [jnp.where(incl, jnp.exp(jnp.minimum(gc[h] - gr_all[DA_LANE + h: DA_LANE + h + 1, :], 0.0)), 0.0)
             for h in heads]
    kb = [k[h] * beta[h] for h in heads]
    kbf, qbf = bf(k), bf(q)
    kk = [_dot_nt(a, b) for a, b in zip(bf(kb), kbf)]
    qk = [_dot_nt(a, b) for a, b in zip(qbf, kbf)]
    t_inv = bf(_inv_unit_lower([jnp.where(strict, kk[h] * decay[h], 0.0) for h in heads]))
    rhs = bf([jnp.concatenate([v[h] * beta[h], kb[h] * jnp.exp(gc[h])], axis=-1) for h in heads])
    sol = [_dot(t, x) for t, x in zip(t_inv, rhs)]

    s = [s_ref[h] for h in heads]
    sb = bf(s)
    ws = [_dot(sol[h][:, dv:dv + dk].astype(BF16), sb[h]) for h in heads]
    o = [_dot((q[h] * jnp.exp(gc[h])).astype(BF16), sb[h]) for h in heads]
    u = bf([sol[h][:, 0:dv] - ws[h] for h in heads])
    o = [o[h] + _dot((qk[h] * decay[h]).astype(BF16), u[h]) for h in heads]
    s_new = [_dot_tn((k[h] * jnp.exp(g_last[h] - gc[h])).astype(BF16), u[h]) for h in heads]
    for h in heads:
        s_ref[h] = jnp.exp(g_last[h]) * s[h] + s_new[h]
        z = pd_ref[:, col(conv_dim, dv, h)]
        o_ref[:, col(0, dv, h)] = (_rms(o[h], g_out) * _silu(z)).astype(o_ref.dtype)

    gsc_ref[...] = chunk_log_decay(psn_ref)

    @pl.when(c == pl.num_programs(1) - 1)
    def _():
        st_ref[0] = s_ref[...]


def gdn_prompt(pd, ps, cw_3, alog_3, dtb_3, go_3, layer, o_all, *, batch, seq, col0):
    m, pd_cols = pd.shape
    dv = go_3.shape[2]
    dk = dv
    width = GDN_HEADS * dv
    conv_dim = cw_3.shape[2]
    nc = seq // CHUNK
    row = lambda b, c: (b * nc + c, 0)
    cb = col0 // width
    lane_spec = pl.BlockSpec((None, 1, SMALL_COLS), lambda b, c: (layer, 0, 0))
    return pl.pallas_call(
        functools.partial(_gdn_prompt_kernel, dk=dk, dv=dv),
        grid=(batch, nc),
        in_specs=[pl.BlockSpec((CHUNK, pd_cols), row),
                  pl.BlockSpec((CHUNK, SMALL_COLS), row),
                  pl.BlockSpec((CHUNK, SMALL_COLS), lambda b, c: (b * nc + jnp.minimum(c + 1, nc - 1), 0)),
                  pl.BlockSpec((None, CONV_W, conv_dim), lambda b, c: (layer, 0, 0)),
                  lane_spec, lane_spec,
                  pl.BlockSpec((None, 1, dv), lambda b, c: (layer, 0, 0)),
                  pl.BlockSpec(memory_space=pl.ANY)],
        out_specs=[pl.BlockSpec((CHUNK, width), lambda b, c: (b * nc + c, cb)),
                   pl.BlockSpec((1, GDN_HEADS, dk, dv), lambda b, c: (b, 0, 0, 0))],
        out_shape=[jax.ShapeDtypeStruct(o_all.shape, BF16),
                   jax.ShapeDtypeStruct((batch, GDN_HEADS, dk, dv), F32)],
        scratch_shapes=[pltpu.VMEM((CHUNK + 8, conv_dim), F32),
                        pltpu.VMEM((GDN_HEADS, dk, dv), F32),
                        pltpu.VMEM((CHUNK, SMALL_COLS), F32)],
        input_output_aliases={7: 0},
        compiler_params=_cparams(2),
        name="gdn_prompt",
    )(pd, ps, ps, cw_3, alog_3, dtb_3, go_3, o_all)


def _gdn_sample_kernel(pd_ref, ps_ref, cw_ref, alog_ref, dtb_ref, go_ref, cs_ref, st_ref,
                       o_in_ref, cs_in_ref, st_in_ref, o_ref, cso_ref, sto_ref, *, dk, dv):
    del o_in_ref, cs_in_ref, st_in_ref
    kw = GDN_HEADS * dk
    conv_dim = 2 * kw + GDN_HEADS * dv
    new = pd_ref[:, 0:conv_dim]
    conv = new * cw_ref[CONV_W - 1:CONV_W, :]
    for j in range(CONV_W - 1):
        conv = conv + cs_ref[j] * cw_ref[j:j + 1, :]
    conv = _silu(conv)
    for j in range(CONV_W - 2):
        cso_ref[j] = cs_ref[j + 1]
    cso_ref[CONV_W - 2] = new

    g_all, beta_all = _gdn_gates(ps_ref[...], alog_ref, dtb_ref)
    a_all = jnp.exp(g_all)
    g_out = go_ref[...]
    for h in range(GDN_HEADS):
        tiles = _column_tiles([_l2norm(conv[:, h * dk:(h + 1) * dk]) * (dk ** -0.5),
                               _l2norm(conv[:, kw + h * dk: kw + (h + 1) * dk])])
        v = conv[:, 2 * kw + h * dv: 2 * kw + (h + 1) * dv]
        z = pd_ref[:, conv_dim + h * dv: conv_dim + (h + 1) * dv]
        rows = []
        for s in range(MIX_ROWS):
            a = a_all[s:s + 1, DA_LANE + h: DA_LANE + h + 1]
            beta = beta_all[s:s + 1, DB_LANE + h: DB_LANE + h + 1]
            q_t, k_t = tiles(s)
            st = st_ref[s, h] * a
            u = beta * (v[s:s + 1, :] - jnp.sum(k_t * st, axis=0, keepdims=True))
            st = st + k_t * u
            sto_ref[s, h] = st
            rows.append(jnp.sum(q_t * st, axis=0, keepdims=True))
        o = jnp.concatenate(rows, axis=0)
        o_ref[:, h * dv:(h + 1) * dv] = (_rms(o, g_out) * _silu(z)).astype(o_ref.dtype)


def gdn_sample(pd, ps, cw_3, alog_3, dtb_3, go_3, conv3, state5, layer, o_all, cs_all, st_all,
               *, row0, n_samples, col0):
    m, pd_cols = pd.shape
    dv = go_3.shape[2]
    dk = dv
    width = GDN_HEADS * dv
    conv_dim = cw_3.shape[2]
    blk0 = row0 // MIX_ROWS
    row = lambda i: (blk0 + i, 0)
    cb = col0 // width
    lane_spec = pl.BlockSpec((None, 1, SMALL_COLS), lambda i: (layer, 0, 0))
    cs_spec = pl.BlockSpec((None, CONV_W - 1, MIX_ROWS, conv_dim), lambda i: (layer, 0, i, 0))
    st_spec = pl.BlockSpec((None, MIX_ROWS, GDN_HEADS, dk, dv), lambda i: (layer, i, 0, 0, 0))
    any_spec = pl.BlockSpec(memory_space=pl.ANY)
    in_specs = [pl.BlockSpec((MIX_ROWS, pd_cols), row),
                pl.BlockSpec((MIX_ROWS, SMALL_COLS), row),
                pl.BlockSpec((None, CONV_W, conv_dim), lambda i: (layer, 0, 0)),
                lane_spec, lane_spec,
                pl.BlockSpec((None, 1, dv), lambda i: (layer, 0, 0)),
                cs_spec, st_spec, any_spec]
    args = [pd, ps, cw_3, alog_3, dtb_3, go_3, conv3, state5, o_all]
    aliases = {8: 0}
    kern = functools.partial(_gdn_sample_kernel, dk=dk, dv=dv)
    if st_all is not None:
        in_specs += [any_spec, any_spec]
        args += [cs_all, st_all]
        aliases[9] = 1
        aliases[10] = 2
    else:
        kern = functools.partial(_no_state_in, functools.partial(_no_state_in, kern, 9), 9)
    return pl.pallas_call(
        kern,
        grid=(n_samples // MIX_ROWS,),
        in_specs=in_specs,
        out_specs=[pl.BlockSpec((MIX_ROWS, width), lambda i: (blk0 + i, cb)), cs_spec, st_spec],
        out_shape=[jax.ShapeDtypeStruct(o_all.shape, BF16),
                   jax.ShapeDtypeStruct(conv3.shape, F32),
                   jax.ShapeDtypeStruct(state5.shape, F32)],
        input_output_aliases=aliases,
        compiler_params=_cparams(1),
        name="gdn_sample",
    )(*args)


def _row_tile(m, cap):
    best = 16
    for t in range(16, cap + 1, 16):
        if m % t == 0:
            best = t
    return best


def kernel(x_prompt, x_sample, mem_prompt, cache_mem_k, cache_mem_v, state_gla, state_gdn, state_conv, g_mix, w_in, gla_w_gate2, gla_b_gate, gla_g_out, gdn_conv_w, gdn_a_log, gdn_dt_bias, gdn_g_out, w_out, g_xattn, g_mem, xa_w_q, xa_w_k, xa_w_v, xa_w_o, g_ffn, ffn_w_gate, ffn_w_up, ffn_w_down, g_final):
    batch, seq, d = x_prompt.shape
    n_s = x_sample.shape[0]
    depth = w_in.shape[0]
    n_mem = mem_prompt.shape[1]
    mp = batch * seq
    m = mp + n_s
    gla_kw = gla_w_gate2.shape[2]
    gla_dv = gla_g_out.shape[1]
    gla_width = GLA_HEADS * gla_dv
    gdn_dv = gdn_g_out.shape[1]
    gdn_width = GDN_HEADS * gdn_dv
    conv_dim = gdn_conv_w.shape[2]
    n_gla = 2 * gla_kw + 2 * gla_width
    c_code = n_gla
    c_conv = c_code + GLA_GATE_RANK
    n_gdn = conv_dim + gdn_width
    c_da = c_conv + n_gdn

    tm = _row_tile(m, 1088)
    tm_mem = _row_tile(batch * n_mem, 1024)

    row3 = lambda p: p.reshape(depth, 1, p.shape[-1])
    lane_row = lambda p: jnp.pad(p, ((0, 0), (DA_LANE, SMALL_COLS - DA_LANE - GDN_HEADS))).reshape(depth, 1, SMALL_COLS)
    g_mix3, g_mem3 = row3(g_mix), row3(g_mem)
    g_xattn_col, g_ffn_col = g_xattn.reshape(depth, d, 1), g_ffn.reshape(depth, d, 1)
    g_final3 = g_final.reshape(1, 1, d)
    bg3, gla_go3, gdn_go3 = row3(gla_b_gate), row3(gla_g_out), row3(gdn_g_out)
    alog3, dtb3 = lane_row(gdn_a_log), lane_row(gdn_dt_bias)

    w_in_t = jnp.swapaxes(w_in, 1, 2)

    conv3 = jnp.swapaxes(state_conv, 1, 2)
    mem2 = mem_prompt.reshape(batch * n_mem, d)
    cache_k4 = cache_token_rows(cache_mem_k)
    cache_v4 = cache_token_rows(cache_mem_v)

    x = jnp.concatenate([x_prompt.reshape(mp, d), x_sample.reshape(n_s, d)], axis=0)
    xb, ss = norm_inputs(x, tm)

    p_gla, p_gdn, p_conv, p_mk, p_mv = [], [], [], [], []
    s_gla_all = s_gdn_all = s_conv_all = None
    for l in range(depth):
        mix = (ss, g_mix3)
        pg = matmul(xb, w_in_t, l, tm=tm, tn=1024, col0=0, n_cols=n_gla, transposed=True, norm=mix, name="proj_gla")
        pd = matmul(xb, w_in_t, l, tm=tm, tn=1024, col0=c_conv, n_cols=n_gdn, transposed=True, norm=mix,
                    name="proj_gdn")
        ps = proj_small(xb, w_in_t, l, tm=tm, row_a=c_code, row_b=c_da, norm=mix)

        o, sa = gla_prompt(pg, ps, gla_w_gate2, bg3, gla_go3, l, batch=batch, seq=seq, d_model=d)
        o, sb = gdn_prompt(pd, ps, gdn_conv_w, alog3, dtb3, gdn_go3, l, o, batch=batch, seq=seq, col0=gla_width)
        o, s_gla_all = gla_sample(pg, ps, gla_w_gate2, bg3, gla_go3, state_gla, l, o, s_gla_all,
                                  row0=mp, n_samples=n_s)
        o, s_conv_all, s_gdn_all = gdn_sample(pd, ps, gdn_conv_w, alog3, dtb3, gdn_go3, conv3, state_gdn, l,
                                              o, s_conv_all, s_gdn_all, row0=mp, n_samples=n_s, col0=gla_width)
        p_gla.append(sa)
        p_gdn.append(sb)
        p_conv.append(jnp.stack([pd[(b + 1) * seq - (CONV_W - 1):(b + 1) * seq, :conv_dim] for b in range(batch)]))
        x, xb, ss = matmul(o, w_out, l, tm=tm, tn=1024, res=x, emit_norm_inputs=True, name="w_out")

        mem_n = rmsnorm_rows(mem2, g_mem3, l, BF16, tm_mem)
        mk = matmul(mem_n, xa_w_k, l, tm=tm_mem, tn=1024, name="mem_k")
        mv = matmul(mem_n, xa_w_v, l, tm=tm_mem, tn=1024, name="mem_v")
        p_mk.append(mk)
        p_mv.append(mv)
        qx = matmul(xb, xa_w_q, l, tm=tm, tn=1024, out_dtype=BF16, norm=(ss, g_xattn_col), name="xa_q")
        ox = xattn_prompt(qx, mk, mv, batch=batch, seq=seq, n_mem=n_mem, tq=min(seq, 1024))
        ox = xattn_sample(qx, cache_k4, cache_v4, l, ox, row0=mp, n_samples=n_s)
        x, xb, ss = matmul(ox, xa_w_o, l, tm=tm, tn=1024, res=x, emit_norm_inputs=True, name="xa_o")

        ff, w_down = swiglu(xb, ffn_w_gate, ffn_w_up, ffn_w_down, l, tm=tm, tn=512, norm=(ss, g_ffn_col))
        w_down = w_down[None]
        if l + 1 < depth:
            x, xb, ss = matmul(ff, w_down, 0, tm=tm, tn=512, res=x, emit_norm_inputs=True, name="ffn_down")
        else:
            x = matmul(ff, w_down, 0, tm=tm, tn=512, res=x, name="ffn_down")

    y_prompt = rmsnorm_rows(x, g_final3, 0, F32, _row_tile(mp, 1024), row0=0, n_rows=mp)
    y_sample = rmsnorm_rows(x, g_final3, 0, F32, n_s, row0=mp, n_rows=n_s)
    xa_shape = (depth, batch, n_mem, XA_HEADS, d // XA_HEADS)
    return (y_prompt.reshape(batch, seq, d),
            y_sample.reshape(n_s, 1, d),
            jnp.stack(p_gla),
            jnp.stack(p_gdn),
            jnp.stack(p_conv),
            jnp.stack(p_mk).reshape(xa_shape),
            jnp.stack(p_mv).reshape(xa_shape),
            s_gla_all,
            s_gdn_all,
            jnp.swapaxes(s_conv_all, 1, 2))
```

```python
import functools

import jax
import jax.numpy as jnp
from jax import lax
from jax.experimental import pallas as pl
from jax.experimental.pallas import tpu as pltpu

F32 = jnp.float32
BF16 = jnp.bfloat16

EPS = 1e-6
GLA_HEADS = 4
GLA_GATE_RANK = 16
GLA_GATE_TEMP = 16.0
GDN_HEADS = 8
CONV_W = 4
XA_HEADS = 4

V7X_VMEM_LIMIT_BYTES = 56 * 1024 * 1024
LANES = 128
CHUNK = 128
STEP_CHUNKS = 2
SUB = 8
SMALL_COLS = LANES
DA_LANE = 16
DB_LANE = 24


def _cparams(n_axes):
    return pltpu.CompilerParams(
        dimension_semantics=("arbitrary",) * n_axes,
        vmem_limit_bytes=V7X_VMEM_LIMIT_BYTES)


def _dot(a, b):
    return jnp.dot(a, b, preferred_element_type=F32)


def _dot_nt(a, b):
    return lax.dot_general(a, b, (((1,), (1,)), ((), ())), preferred_element_type=F32)


def _dot_tn(a, b):
    return lax.dot_general(a, b, (((0,), (0,)), ((), ())), preferred_element_type=F32)


def _dot_exact(a, b):
    return jnp.dot(a, b, preferred_element_type=F32, precision=lax.Precision.HIGHEST)


def _silu(x):
    return x * (1.0 / (1.0 + jnp.exp(-x)))


def _sigmoid(x):
    return 1.0 / (1.0 + jnp.exp(-x))


def _softplus(x):
    return jnp.maximum(x, 0.0) + jnp.log1p(jnp.exp(-jnp.abs(x)))


def _log_sigmoid(x):
    return jnp.minimum(x, 0.0) - jnp.log1p(jnp.exp(-jnp.abs(x)))


def _rms(x, g):
    return x * lax.rsqrt(jnp.mean(x * x, axis=-1, keepdims=True) + EPS) * g


def _iota2(shape, axis):
    return lax.broadcasted_iota(jnp.int32, shape, axis)


def _rmsnorm_kernel(x_ref, g_ref, o_ref):
    o_ref[...] = _rms(x_ref[...], g_ref[...]).astype(o_ref.dtype)


def rmsnorm_rows(x, g3, layer, out_dtype, tm, row0=0, n_rows=None):
    m, d = x.shape
    n_rows = m if n_rows is None else n_rows
    blk0 = row0 // tm
    return pl.pallas_call(
        _rmsnorm_kernel,
        grid=(n_rows // tm,),
        in_specs=[pl.BlockSpec((tm, d), lambda i: (blk0 + i, 0)),
                  pl.BlockSpec((None, 1, d), lambda i: (layer, 0, 0))],
        out_specs=pl.BlockSpec((tm, d), lambda i: (i, 0)),
        out_shape=jax.ShapeDtypeStruct((n_rows, d), out_dtype),
        compiler_params=_cparams(1),
        name="rmsnorm",
    )(x, g3)


CAST_ROWS = 256


def _stage_weight(w_ref, wb_ref, gain_ref=None, gain_is_row=False):
    for r in range(0, w_ref.shape[0], CAST_ROWS):
        blk = w_ref[r:r + CAST_ROWS, :]
        if gain_ref is not None:
            blk = blk * (gain_ref[...] if gain_is_row else gain_ref[r:r + CAST_ROWS, :])
        wb_ref[r:r + CAST_ROWS, :] = blk.astype(BF16)


def _row_rsqrt(ss_ref, d):
    tot = ss_ref[:, 0:1]
    for j in range(1, ss_ref.shape[1] // LANES):
        tot = tot + ss_ref[:, j * LANES:j * LANES + 1]
    return lax.rsqrt(tot * (1.0 / d) + EPS)


def _sum_squares(x):
    return jnp.broadcast_to(jnp.sum(x * x, axis=-1, keepdims=True), (x.shape[0], LANES))


def _mm_kernel(*refs, has_res, stage, transposed, has_norm, emit):
    it = iter(refs)
    a_ref, w_ref = next(it), next(it)
    if transposed:
        w_ref = w_ref.at[0]
    ss_ref, gain_ref = (next(it), next(it)) if has_norm else (None, None)
    res_ref = next(it) if has_res else None
    o_ref = next(it)
    ob_ref, sso_ref = (next(it), next(it)) if emit else (None, None)
    if stage:
        wb_ref = next(it)

        @pl.when(pl.program_id(1) == 0)
        def _():
            _stage_weight(w_ref, wb_ref, gain_ref, gain_is_row=transposed)

        w = wb_ref[...]
    else:
        w = w_ref[...]
    acc = _dot_nt(a_ref[...], w) if transposed else _dot(a_ref[...], w)
    if has_norm:
        acc = acc * _row_rsqrt(ss_ref, a_ref.shape[1])
    if has_res:
        acc = res_ref[...] + acc
    o_ref[...] = acc.astype(o_ref.dtype)
    if emit:
        ob_ref[...] = acc.astype(BF16)
        sso_ref[...] = _sum_squares(acc)


def matmul(a, w3, layer, *, tm, tn, col0=0, n_cols=None, res=None, out_dtype=F32, transposed=False,
           norm=None, emit_norm_inputs=False, name="matmul"):
    m, k = a.shape
    n_total = w3.shape[1] if transposed else w3.shape[2]
    n_cols = n_total if n_cols is None else n_cols
    n_tiles = n_cols // tn
    assert m % tm == 0 and n_cols % tn == 0
    stage = w3.dtype != BF16
    assert not stage or (tn if transposed else k) % CAST_ROWS == 0
    assert norm is None or stage
    if transposed:
        assert col0 % 8 == 0
        w_spec = pl.BlockSpec((pl.Element(1), pl.Element(tn), pl.Element(k)),
                              lambda j, i: (layer, pl.multiple_of(col0 + j * tn, 8), 0))
        wb_shape = (tn, k)
    else:
        assert col0 % tn == 0
        cb0 = col0 // tn
        w_spec = pl.BlockSpec((None, k, tn), lambda j, i: (layer, 0, cb0 + j))
        wb_shape = (k, tn)
    in_specs = [pl.BlockSpec((tm, k), lambda j, i: (i, 0)), w_spec]
    args = [a, w3]
    if norm is not None:
        ss, gain3 = norm
        in_specs.append(pl.BlockSpec((tm, ss.shape[1]), lambda j, i: (i, 0)))
        in_specs.append(pl.BlockSpec((None,) + gain3.shape[1:], lambda j, i: (layer, 0, 0)))
        args += [ss, gain3]
    if res is not None:
        in_specs.append(pl.BlockSpec((tm, tn), lambda j, i: (i, j)))
        args.append(res)
    tile = pl.BlockSpec((tm, tn), lambda j, i: (i, j))
    out_specs, out_shape = tile, jax.ShapeDtypeStruct((m, n_cols), out_dtype)
    if emit_norm_inputs:
        out_specs = [tile, tile, pl.BlockSpec((tm, LANES), lambda j, i: (i, j))]
        out_shape = [out_shape, jax.ShapeDtypeStruct((m, n_cols), BF16),
                     jax.ShapeDtypeStruct((m, LANES * n_tiles), F32)]
    return pl.pallas_call(
        functools.partial(_mm_kernel, has_res=res is not None, stage=stage, transposed=transposed,
                          has_norm=norm is not None, emit=emit_norm_inputs),
        grid=(n_tiles, m // tm),
        in_specs=in_specs,
        out_specs=out_specs,
        out_shape=out_shape,
        scratch_shapes=[pltpu.VMEM(wb_shape, BF16)] if stage else [],
        compiler_params=_cparams(2),
        name=name,
    )(*args)


def _norm_inputs_kernel(x_ref, xb_ref, ss_ref):
    x = x_ref[...]
    xb_ref[...] = x.astype(BF16)
    ss_ref[...] = _sum_squares(x)


def norm_inputs(x, tm):
    m, d = x.shape
    return pl.pallas_call(
        _norm_inputs_kernel,
        grid=(m // tm,),
        in_specs=[pl.BlockSpec((tm, d), lambda i: (i, 0))],
        out_specs=[pl.BlockSpec((tm, d), lambda i: (i, 0)), pl.BlockSpec((tm, LANES), lambda i: (i, 0))],
        out_shape=[jax.ShapeDtypeStruct((m, d), BF16), jax.ShapeDtypeStruct((m, LANES), F32)],
        compiler_params=_cparams(1),
        name="norm_inputs",
    )(x)


SMALL_PIECE = 16


def _proj_small_kernel(a_ref, wa_ref, wb_ref, ss_ref, gain_ref, o_ref, w_ref):
    @pl.when(pl.program_id(0) == 0)
    def _():
        w_ref[...] = jnp.zeros_like(w_ref)
        w_ref[0:SMALL_PIECE, :] = (wa_ref[...] * gain_ref[...]).astype(BF16)
        w_ref[SMALL_PIECE:2 * SMALL_PIECE, :] = (wb_ref[...] * gain_ref[...]).astype(BF16)

    o_ref[...] = _dot_nt(a_ref[...], w_ref[...]) * _row_rsqrt(ss_ref, a_ref.shape[1])


def proj_small(a, wt3, layer, *, tm, row_a, row_b, norm):
    m, k = a.shape
    ss, gain3 = norm
    assert row_a % SMALL_PIECE == 0 and row_b % SMALL_PIECE == 0
    piece = lambda r: pl.BlockSpec((None, SMALL_PIECE, k), lambda i: (layer, r // SMALL_PIECE, 0))
    return pl.pallas_call(
        _proj_small_kernel,
        grid=(m // tm,),
        in_specs=[pl.BlockSpec((tm, k), lambda i: (i, 0)), piece(row_a), piece(row_b),
                  pl.BlockSpec((tm, ss.shape[1]), lambda i: (i, 0)),
                  pl.BlockSpec((None, 1, k), lambda i: (layer, 0, 0))],
        out_specs=pl.BlockSpec((tm, SMALL_COLS), lambda i: (i, 0)),
        out_shape=jax.ShapeDtypeStruct((m, SMALL_COLS), F32),
        scratch_shapes=[pltpu.VMEM((SMALL_COLS, k), BF16)],
        compiler_params=_cparams(1),
        name="proj_small",
    )(a, wt3, wt3, ss, gain3)


def _swiglu_kernel(a_ref, wg_ref, wu_ref, ss_ref, gain_ref, wd_ref, o_ref, wdb_ref, wgb_ref, wub_ref):
    @pl.when(pl.program_id(1) == 0)
    def _():
        _stage_weight(wg_ref, wgb_ref, gain_ref)
        _stage_weight(wu_ref, wub_ref, gain_ref)

    a = a_ref[...]
    r = _row_rsqrt(ss_ref, a_ref.shape[1])
    gate = _dot(a, wgb_ref[...]) * r
    up = _dot(a, wub_ref[...]) * r
    o_ref[...] = (_silu(gate) * up).astype(o_ref.dtype)
    wdb_ref[...] = wd_ref[...].astype(BF16)


def swiglu(a, wg3, wu3, wd3, layer, *, tm, tn, norm):
    m, k = a.shape
    f = wg3.shape[2]
    d_out = wd3.shape[2]
    ss, gain3 = norm
    n_j, n_i = f // tn, m // tm
    assert m % tm == 0 and f % tn == 0 and k % CAST_ROWS == 0 and f % (n_j * n_i * 16) == 0
    wd_rows = f // (n_j * n_i)
    wspec = pl.BlockSpec((None, k, tn), lambda j, i: (layer, 0, j))
    return pl.pallas_call(
        _swiglu_kernel,
        grid=(n_j, n_i),
        in_specs=[pl.BlockSpec((tm, k), lambda j, i: (i, 0)), wspec, wspec,
                  pl.BlockSpec((tm, ss.shape[1]), lambda j, i: (i, 0)),
                  pl.BlockSpec((None, k, 1), lambda j, i: (layer, 0, 0)),
                  pl.BlockSpec((None, wd_rows, d_out), lambda j, i: (layer, j * n_i + i, 0))],
        out_specs=[pl.BlockSpec((tm, tn), lambda j, i: (i, j)),
                   pl.BlockSpec((wd_rows, d_out), lambda j, i: (j * n_i + i, 0))],
        out_shape=[jax.ShapeDtypeStruct((m, f), BF16), jax.ShapeDtypeStruct((f, d_out), BF16)],
        scratch_shapes=[pltpu.VMEM((k, tn), BF16), pltpu.VMEM((k, tn), BF16)],
        compiler_params=_cparams(2),
        name="swiglu",
    )(a, wg3, wu3, ss, gain3, wd3)


def _xattn_prompt_kernel(q_ref, k_ref, v_ref, o_ref, *, scale, hd):
    heads = [slice(h * hd, (h + 1) * hd) for h in range(XA_HEADS)]
    s = [_dot_nt(q_ref[:, c], k_ref[:, c].astype(BF16)) * scale for c in heads]
    e = [jnp.exp(x - jnp.max(x, axis=-1, keepdims=True)) for x in s]
    p = [(x / jnp.sum(x, axis=-1, keepdims=True)).astype(BF16) for x in e]
    for x, c in zip(p, heads):
        o_ref[:, c] = _dot(x, v_ref[:, c].astype(BF16)).astype(o_ref.dtype)


def xattn_prompt(qx, mk, mv, *, batch, seq, n_mem, tq):
    m, d = qx.shape
    nt = seq // tq
    qspec = pl.BlockSpec((tq, d), lambda b, t: (b * nt + t, 0))
    kvspec = pl.BlockSpec((n_mem, d), lambda b, t: (b, 0))
    return pl.pallas_call(
        functools.partial(_xattn_prompt_kernel, scale=(d // XA_HEADS) ** -0.5, hd=d // XA_HEADS),
        grid=(batch, nt),
        in_specs=[qspec, kvspec, kvspec],
        out_specs=qspec,
        out_shape=jax.ShapeDtypeStruct((m, d), BF16),
        compiler_params=_cparams(2),
        name="xattn_prompt",
    )(qx, mk, mv)


XS_ROWS = 16
XS_STEP = 2


def _xattn_sample_kernel(q_ref, k_ref, v_ref, o_in_ref, o_ref, qf_ref, of_ref, *, scale, hd):
    del o_in_ref
    j = pl.program_id(1)

    @pl.when(j == 0)
    def _():
        qf_ref[...] = q_ref[...].astype(F32)

    nj = hd // LANES
    per_tok = nj * XA_HEADS
    half = per_tok // 2
    n_mem = k_ref.shape[1] // per_tok
    lanes = lambda h, jj: slice(h * hd + jj * LANES, h * hd + (jj + 1) * LANES)
    for t in range(XS_STEP):
        r = j * XS_STEP + t
        q_row = qf_ref[pl.ds(r, 1), :]
        q_tok = jnp.concatenate([q_row[:, lanes(h, jj)] for jj in range(nj) for h in range(XA_HEADS)], axis=0)
        prod = k_ref[t].reshape(n_mem, per_tok, LANES) * q_tok[None]
        part = prod[:, 0:half, :] + prod[:, half:per_tok, :]
        part = part + pltpu.roll(part, XA_HEADS, axis=1)
        s = jnp.sum(part, axis=-1, keepdims=True) * scale
        e = jnp.exp(s - jnp.max(s, axis=0, keepdims=True))
        p = e / jnp.sum(e, axis=0, keepdims=True)
        v3 = v_ref[t].reshape(n_mem, per_tok, LANES)
        acc = [jnp.sum(v3[:, g * half:(g + 1) * half, :] * p, axis=0) for g in range(2)]
        pieces = []
        for h in range(XA_HEADS):
            for jj in range(nj):
                row = (jj % 2) * XA_HEADS + h
                pieces.append(acc[jj // 2][row:row + 1, :])
        of_ref[pl.ds(r, 1), :] = jnp.concatenate(pieces, axis=1)

    @pl.when(j == pl.num_programs(1) - 1)
    def _():
        o_ref[...] = of_ref[...].astype(o_ref.dtype)


def cache_token_rows(cache):
    depth, s, n_mem, heads, hd = cache.shape
    nj = hd // LANES
    c6 = cache.reshape(depth, s, n_mem, heads, nj, LANES)
    return jnp.transpose(c6, (0, 1, 2, 4, 3, 5)).reshape(depth, s, n_mem * nj * heads, LANES)


def xattn_sample(qx, cache_k4, cache_v4, layer, o_all, *, row0, n_samples):
    m, d = qx.shape
    tok_rows = cache_k4.shape[2]
    hd = d // XA_HEADS
    assert (hd // LANES) * XA_HEADS == 16 and XA_HEADS == 4
    blk0 = row0 // XS_ROWS
    inner = XS_ROWS // XS_STEP
    qspec = pl.BlockSpec((XS_ROWS, d), lambda i, j: (blk0 + i, 0))
    cspec = pl.BlockSpec((None, XS_STEP, tok_rows, LANES), lambda i, j: (layer, i * inner + j, 0, 0))
    return pl.pallas_call(
        functools.partial(_xattn_sample_kernel, scale=hd ** -0.5, hd=hd),
        grid=(n_samples // XS_ROWS, inner),
        in_specs=[qspec, cspec, cspec, pl.BlockSpec(memory_space=pl.ANY)],
        out_specs=qspec,
        out_shape=jax.ShapeDtypeStruct((m, d), BF16),
        scratch_shapes=[pltpu.VMEM((XS_ROWS, d), F32), pltpu.VMEM((XS_ROWS, d), F32)],
        input_output_aliases={3: 0},
        compiler_params=_cparams(2),
        name="xattn_sample",
    )(qx, cache_k4, cache_v4, o_all)


def _gla_log_decay(code, w2_ref, bg_ref):
    pre = _dot_exact(code, w2_ref[...]) + bg_ref[...]
    return _log_sigmoid(pre) * (1.0 / GLA_GATE_TEMP)


def _gla_prompt_kernel(pg_ref, ps_ref, psn_ref, w2_ref, bg_ref, go_ref, o_ref, st_ref, stt_ref, bsc_ref, *, dk, dv):
    c = pl.program_id(1)
    C = CHUNK
    kw = GLA_HEADS * dk

    def chunk_log_decay(small_ref):
        la = _gla_log_decay(small_ref[:, 0:GLA_GATE_RANK], w2_ref, bg_ref)
        tri = (_iota2((C, C), 0) >= _iota2((C, C), 1)).astype(F32)
        return jnp.concatenate([_dot_exact(tri, la[cc * C:(cc + 1) * C]) for cc in range(STEP_CHUNKS)], axis=0)

    @pl.when(c == 0)
    def _():
        stt_ref[...] = jnp.zeros_like(stt_ref)
        bsc_ref[...] = chunk_log_decay(ps_ref)

    b_all = bsc_ref[...]
    row_s = _iota2((SUB, dk), 0)
    lane_a = _iota2((SUB, C), 1)
    ones_k = jnp.ones((dk, LANES), BF16)
    g_out = go_ref[...]

    units = [(cc, h) for cc in range(STEP_CHUNKS) for h in range(GLA_HEADS)]
    heads = range(len(units))
    rows = lambda cc: slice(cc * C, (cc + 1) * C)
    col = lambda base, width, h: slice(base + h * width, base + (h + 1) * width)
    q = [pg_ref[rows(cc), col(0, dk, h)] * (dk ** -0.5) for cc, h in units]
    k = [pg_ref[rows(cc), col(kw, dk, h)] for cc, h in units]
    vb = [pg_ref[rows(cc), col(2 * kw, dv, h)].astype(BF16) for cc, h in units]
    b = [b_all[rows(cc), col(0, dk, h)] for cc, h in units]
    b_last = [bh[C - 1:C, :] for bh in b]

    q_in = [(q[u] * jnp.exp(b[u])).astype(BF16) for u in heads]
    upd = [_dot_tn(vb[u], (k[u] * jnp.exp(b_last[u] - b[u])).astype(BF16)) for u in heads]
    s_t = [stt_ref[h] for h in range(GLA_HEADS)]
    o = [None] * len(units)
    for cc in range(STEP_CHUNKS):
        for h in range(GLA_HEADS):
            u = cc * GLA_HEADS + h
            o[u] = _dot_nt(q_in[u], s_t[h].astype(BF16))
        for h in range(GLA_HEADS):
            u = cc * GLA_HEADS + h
            s_t[h] = s_t[h] * jnp.exp(b_last[u]) + upd[u]
    for h in range(GLA_HEADS):
        stt_ref[h] = s_t[h]

    n_sub = C // SUB
    earlier = [[None] for _ in heads]
    for i in range(1, n_sub):
        lo = i * SUB
        for h in heads:
            b_ref = b[h][lo - 1:lo, :]
            q_rel = (q[h][lo:lo + SUB] * jnp.exp(b[h][lo:lo + SUB] - b_ref)).astype(BF16)
            k_rel = jnp.concatenate([k[h][0:lo] * jnp.exp(b_ref - b[h][0:lo]), jnp.zeros((C - lo, dk), F32)],
                                    axis=0).astype(BF16)
            earlier[h].append(_dot_nt(q_rel, k_rel))
    sums = []
    for h in heads:
        zs = []
        for i in range(n_sub):
            lo = i * SUB
            qi, ki, bi = q[h][lo:lo + SUB], k[h][lo:lo + SUB], b[h][lo:lo + SUB]
            for s in range(SUB):
                dec = jnp.exp(jnp.minimum(bi - bi[s:s + 1, :], 0.0))
                zs.append(jnp.where(row_s >= s, qi * ki[s:s + 1, :] * dec, 0.0))
        sums.append(_dot(jnp.concatenate(zs, axis=0).astype(BF16), ones_k))
    a_rows = [[] for _ in heads]
    for i in range(n_sub):
        lo = i * SUB
        for h in heads:
            a_i = jnp.zeros((SUB, C), F32) if i == 0 else earlier[h][i]
            for s in range(SUB):
                r0 = (i * SUB + s) * SUB
                a_i = jnp.where(lane_a == lo + s, sums[h][r0:r0 + SUB, 0:C], a_i)
            a_rows[h].append(a_i)
    o = [o[h] + _dot(jnp.concatenate(a_rows[h], axis=0).astype(BF16), vb[h]) for h in heads]

    for u, (cc, h) in enumerate(units):
        r = pg_ref[rows(cc), col(2 * kw + GLA_HEADS * dv, dv, h)]
        o_ref[rows(cc), col(0, dv, h)] = (_rms(o[u], g_out) * _silu(r)).astype(o_ref.dtype)

    bsc_ref[...] = chunk_log_decay(psn_ref)

    @pl.when(c == pl.num_programs(1) - 1)
    def _():
        for h in range(GLA_HEADS):
            st_ref[0, h] = stt_ref[h].T


def gla_prompt(pg, ps, w2_3, bg_3, go_3, layer, *, batch, seq, d_model):
    m, pg_cols = pg.shape
    kw = w2_3.shape[2]
    dk = kw // GLA_HEADS
    dv = go_3.shape[2]
    width = GLA_HEADS * dv
    step_rows = STEP_CHUNKS * CHUNK
    assert seq % step_rows == 0
    nc = seq // step_rows
    row = lambda b, c: (b * nc + c, 0)
    next_row = lambda b, c: (b * nc + jnp.minimum(c + 1, nc - 1), 0)
    return pl.pallas_call(
        functools.partial(_gla_prompt_kernel, dk=dk, dv=dv),
        grid=(batch, nc),
        in_specs=[pl.BlockSpec((step_rows, pg_cols), row),
                  pl.BlockSpec((step_rows, SMALL_COLS), row),
                  pl.BlockSpec((step_rows, SMALL_COLS), next_row),
                  pl.BlockSpec((None, GLA_GATE_RANK, kw), lambda b, c: (layer, 0, 0)),
                  pl.BlockSpec((None, 1, kw), lambda b, c: (layer, 0, 0)),
                  pl.BlockSpec((None, 1, dv), lambda b, c: (layer, 0, 0))],
        out_specs=[pl.BlockSpec((step_rows, width), row),
                   pl.BlockSpec((1, GLA_HEADS, dk, dv), lambda b, c: (b, 0, 0, 0))],
        out_shape=[jax.ShapeDtypeStruct((m, d_model), BF16),
                   jax.ShapeDtypeStruct((batch, GLA_HEADS, dk, dv), F32)],
        scratch_shapes=[pltpu.VMEM((GLA_HEADS, dv, dk), F32), pltpu.VMEM((step_rows, kw), F32)],
        compiler_params=_cparams(2),
        name="gla_prompt",
    )(pg, ps, ps, w2_3, bg_3, go_3)


MIX_ROWS = 16


def _column_tiles(xs):
    parts = []
    for x in xs:
        head = x.astype(BF16).astype(F32)
        both = jnp.concatenate([head, x - head, jnp.zeros((LANES - 2 * MIX_ROWS, x.shape[1]), F32)], axis=0)
        parts.append(both.T[:, 0:2 * MIX_ROWS].astype(BF16))
    lhs = jnp.concatenate(parts, axis=0)
    sel = jnp.bitwise_and(_iota2((2 * MIX_ROWS, LANES), 0), MIX_ROWS - 1)

    def tiles(s):
        out = _dot(lhs, jnp.where(sel == s, 1.0, 0.0).astype(BF16))
        return [out[j * LANES:(j + 1) * LANES] for j in range(len(xs))]

    return tiles


def _gla_sample_kernel(pg_ref, ps_ref, w2_ref, bg_ref, go_ref, st_ref, o_in_ref, st_in_ref,
                       o_ref, sto_ref, *, dk, dv):
    del o_in_ref, st_in_ref
    kw = GLA_HEADS * dk
    decay = jnp.exp(_gla_log_decay(ps_ref[:, 0:GLA_GATE_RANK], w2_ref, bg_ref))
    g_out = go_ref[...]
    wide = lambda t: jnp.concatenate([t] * (dv // LANES), axis=1)
    for h in range(GLA_HEADS):
        tiles = _column_tiles([pg_ref[:, h * dk:(h + 1) * dk] * (dk ** -0.5),
                               pg_ref[:, kw + h * dk: kw + (h + 1) * dk],
                               decay[:, h * dk:(h + 1) * dk]])
        v = pg_ref[:, 2 * kw + h * dv: 2 * kw + (h + 1) * dv]
        r = pg_ref[:, 2 * kw + GLA_HEADS * dv + h * dv: 2 * kw + GLA_HEADS * dv + (h + 1) * dv]
        rows = []
        for s in range(MIX_ROWS):
            q_t, k_t, a_t = tiles(s)
            st = st_ref[s, h] * wide(a_t) + wide(k_t) * v[s:s + 1, :]
            sto_ref[s, h] = st
            rows.append(jnp.sum(wide(q_t) * st, axis=0, keepdims=True))
        o = jnp.concatenate(rows, axis=0)
        o_ref[:, h * dv:(h + 1) * dv] = (_rms(o, g_out) * _silu(r)).astype(o_ref.dtype)


def gla_sample(pg, ps, w2_3, bg_3, go_3, state5, layer, o_all, st_all, *, row0, n_samples):
    m, pg_cols = pg.shape
    kw = w2_3.shape[2]
    dk = kw // GLA_HEADS
    dv = go_3.shape[2]
    width = GLA_HEADS * dv
    blk0 = row0 // MIX_ROWS
    row = lambda i: (blk0 + i, 0)
    st_spec = pl.BlockSpec((None, MIX_ROWS, GLA_HEADS, dk, dv), lambda i: (layer, i, 0, 0, 0))
    any_spec = pl.BlockSpec(memory_space=pl.ANY)
    in_specs = [pl.BlockSpec((MIX_ROWS, pg_cols), row),
                pl.BlockSpec((MIX_ROWS, SMALL_COLS), row),
                pl.BlockSpec((None, GLA_GATE_RANK, kw), lambda i: (layer, 0, 0)),
                pl.BlockSpec((None, 1, kw), lambda i: (layer, 0, 0)),
                pl.BlockSpec((None, 1, dv), lambda i: (layer, 0, 0)),
                st_spec, any_spec]
    args = [pg, ps, w2_3, bg_3, go_3, state5, o_all]
    aliases = {6: 0}
    kern = functools.partial(_gla_sample_kernel, dk=dk, dv=dv)
    if st_all is not None:
        in_specs.append(any_spec)
        args.append(st_all)
        aliases[7] = 1
    else:
        kern = functools.partial(_no_state_in, kern, 7)
    return pl.pallas_call(
        kern,
        grid=(n_samples // MIX_ROWS,),
        in_specs=in_specs,
        out_specs=[pl.BlockSpec((MIX_ROWS, width), row), st_spec],
        out_shape=[jax.ShapeDtypeStruct(o_all.shape, BF16),
                   jax.ShapeDtypeStruct(state5.shape, F32)],
        input_output_aliases=aliases,
        compiler_params=_cparams(1),
        name="gla_sample",
    )(*args)


def _no_state_in(kern, pos, *refs):
    return kern(*refs[:pos], None, *refs[pos:])


def _gdn_gates(ps, alog_ref, dtb_ref):
    g = -jnp.exp(alog_ref[...]) * _softplus(ps + dtb_ref[...])
    return g, _sigmoid(ps)


def _l2norm(x):
    return x * lax.rsqrt(jnp.sum(x * x, axis=-1, keepdims=True) + EPS)


def _inv_unit_lower(lows):
    n = lows[0].shape[0]
    r = _iota2((n, n), 0)
    c = _iota2((n, n), 1)
    eye = (r == c).astype(F32)
    base = 8
    same = lambda size: jnp.right_shift(r, size.bit_length() - 1) == jnp.right_shift(c, size.bit_length() - 1)
    bf = lambda xs: [x.astype(BF16) for x in xs]
    in_base = same(base)
    l8 = bf([jnp.where(in_base, low, 0.0) for low in lows])
    x = [eye - a.astype(F32) for a in l8]
    p = [_dot(a, a) for a in l8]
    for step in range(2):
        pb = bf(p)
        x = [xi + _dot(xb, pi) for xi, xb, pi in zip(x, bf(x), pb)]
        if step == 0:
            p = [_dot(pi, pi) for pi in pb]
    m = base
    while m < n:
        in_2m, in_m = same(2 * m), same(m)
        off = bf([jnp.where(in_2m, jnp.where(in_m, 0.0, low), 0.0) for low in lows])
        xb = bf(x)
        y = bf([_dot(oi, xi) for oi, xi in zip(off, xb)])
        x = [xi - _dot(xbi, yi) for xi, xbi, yi in zip(x, xb, y)]
        m *= 2
    return x


def _gdn_prompt_kernel(pd_ref, ps_ref, psn_ref, cw_ref, alog_ref, dtb_ref, go_ref, o_in_ref, o_ref, st_ref,
                       xb_ref, s_ref, gsc_ref, *, dk, dv):
    del o_in_ref
    c = pl.program_id(1)
    C = CHUNK
    R = STEP_CHUNKS * C
    kw = GDN_HEADS * dk
    conv_dim = 2 * kw + GDN_HEADS * dv
    halo = 8

    def chunk_log_decay(small_ref):
        g_all, _ = _gdn_gates(small_ref[...], alog_ref, dtb_ref)
        tri = (_iota2((C, C), 0) >= _iota2((C, C), 1)).astype(F32)
        return jnp.concatenate([_dot_exact(tri, g_all[cc * C:(cc + 1) * C]) for cc in range(STEP_CHUNKS)], axis=0)

    @pl.when(c == 0)
    def _():
        xb_ref[0:halo, :] = jnp.zeros((halo, conv_dim), F32)
        s_ref[...] = jnp.zeros_like(s_ref)
        gsc_ref[...] = chunk_log_decay(ps_ref)

    xb_ref[halo:halo + R, :] = pd_ref[:, 0:conv_dim]
    window = xb_ref[...]
    conv = window[halo:halo + R] * cw_ref[CONV_W - 1:CONV_W, :]
    for back in range(1, CONV_W):
        tap = pltpu.roll(window, back, axis=0)[halo:halo + R]
        conv = conv + tap * cw_ref[CONV_W - 1 - back:CONV_W - back, :]
    conv = _silu(conv)
    xb_ref[0:halo, :] = window[R:R + halo]

    beta_all = _sigmoid(ps_ref[...])
    gc_all = gsc_ref[...]
    gr_all = [gc_all[cc * C:(cc + 1) * C].T for cc in range(STEP_CHUNKS)]
    r_i = _iota2((C, C), 0)
    c_i = _iota2((C, C), 1)
    incl = r_i >= c_i
    strict = r_i > c_i
    g_out = go_ref[...]

    units = [(cc, h) for cc in range(STEP_CHUNKS) for h in range(GDN_HEADS)]
    ids = range(len(units))
    rows = lambda cc: slice(cc * C, (cc + 1) * C)
    col = lambda base, width, h: slice(base + h * width, base + (h + 1) * width)
    bf = lambda xs: [x.astype(BF16) for x in xs]
    q = [_l2norm(conv[rows(cc), col(0, dk, h)]) * (dk ** -0.5) for cc, h in units]
    k = [_l2norm(conv[rows(cc), col(kw, dk, h)]) for cc, h in units]
    v = [conv[rows(cc), col(2 * kw, dv, h)] for cc, h in units]
    beta = [beta_all[rows(cc), DB_LANE + h: DB_LANE + h + 1] for cc, h in units]
    gc = [gc_all[rows(cc), DA_LANE + h: DA_LANE + h + 1] for cc, h in units]
    g_last = [g[C - 1:C, :] for g in gc]
    decay = [jnp.where(incl, jnp.exp(jnp.minimum(gc[u] - gr_all[cc][DA_LANE + h: DA_LANE + h + 1, :], 0.0)), 0.0)
             for u, (cc, h) in enumerate(units)]
    kb = [k[u] * beta[u] for u in ids]
    kbf, qbf = bf(k), bf(q)
    kk = [_dot_nt(a, b) for a, b in zip(bf(kb), kbf)]
    qk = [_dot_nt(a, b) for a, b in zip(qbf, kbf)]
    t_inv = bf(_inv_unit_lower([jnp.where(strict, kk[u] * decay[u], 0.0) for u in ids]))
    rhs = bf([jnp.concatenate([v[u] * beta[u], kb[u] * jnp.exp(gc[u])], axis=-1) for u in ids])
    sol = [_dot(t, x) for t, x in zip(t_inv, rhs)]
    wbf = bf([sol[u][:, dv:dv + dk] for u in ids])
    q_dec = bf([q[u] * jnp.exp(gc[u]) for u in ids])
    k_dec = bf([k[u] * jnp.exp(g_last[u] - gc[u]) for u in ids])
    qkd = bf([qk[u] * decay[u] for u in ids])

    s = [s_ref[h] for h in range(GDN_HEADS)]
    o = [None] * len(units)
    for cc in range(STEP_CHUNKS):
        us = [cc * GDN_HEADS + h for h in range(GDN_HEADS)]
        sb = bf(s)
        ws = [_dot(wbf[u], sb[h]) for h, u in enumerate(us)]
        oc = [_dot(q_dec[u], sb[h]) for h, u in enumerate(us)]
        uu = bf([sol[u][:, 0:dv] - ws[h] for h, u in enumerate(us)])
        for h, u in enumerate(us):
            o[u] = oc[h] + _dot(qkd[u], uu[h])
        s_new = [_dot_tn(k_dec[u], uu[h]) for h, u in enumerate(us)]
        s = [jnp.exp(g_last[u]) * s[h] + s_new[h] for h, u in enumerate(us)]
    for h in range(GDN_HEADS):
        s_ref[h] = s[h]
    for u, (cc, h) in enumerate(units):
        z = pd_ref[rows(cc), col(conv_dim, dv, h)]
        o_ref[rows(cc), col(0, dv, h)] = (_rms(o[u], g_out) * _silu(z)).astype(o_ref.dtype)

    gsc_ref[...] = chunk_log_decay(psn_ref)

    @pl.when(c == pl.num_programs(1) - 1)
    def _():
        st_ref[0] = s_ref[...]


def gdn_prompt(pd, ps, cw_3, alog_3, dtb_3, go_3, layer, o_all, *, batch, seq, col0):
    m, pd_cols = pd.shape
    dv = go_3.shape[2]
    dk = dv
    width = GDN_HEADS * dv
    conv_dim = cw_3.shape[2]
    step_rows = STEP_CHUNKS * CHUNK
    assert seq % step_rows == 0
    nc = seq // step_rows
    row = lambda b, c: (b * nc + c, 0)
    cb = col0 // width
    lane_spec = pl.BlockSpec((None, 1, SMALL_COLS), lambda b, c: (layer, 0, 0))
    return pl.pallas_call(
        functools.partial(_gdn_prompt_kernel, dk=dk, dv=dv),
        grid=(batch, nc),
        in_specs=[pl.BlockSpec((step_rows, pd_cols), row),
                  pl.BlockSpec((step_rows, SMALL_COLS), row),
                  pl.BlockSpec((step_rows, SMALL_COLS), lambda b, c: (b * nc + jnp.minimum(c + 1, nc - 1), 0)),
                  pl.BlockSpec((None, CONV_W, conv_dim), lambda b, c: (layer, 0, 0)),
                  lane_spec, lane_spec,
                  pl.BlockSpec((None, 1, dv), lambda b, c: (layer, 0, 0)),
                  pl.BlockSpec(memory_space=pl.ANY)],
        out_specs=[pl.BlockSpec((step_rows, width), lambda b, c: (b * nc + c, cb)),
                   pl.BlockSpec((1, GDN_HEADS, dk, dv), lambda b, c: (b, 0, 0, 0))],
        out_shape=[jax.ShapeDtypeStruct(o_all.shape, BF16),
                   jax.ShapeDtypeStruct((batch, GDN_HEADS, dk, dv), F32)],
        scratch_shapes=[pltpu.VMEM((step_rows + 8, conv_dim), F32),
                        pltpu.VMEM((GDN_HEADS, dk, dv), F32),
                        pltpu.VMEM((step_rows, SMALL_COLS), F32)],
        input_output_aliases={7: 0},
        compiler_params=_cparams(2),
        name="gdn_prompt",
    )(pd, ps, ps, cw_3, alog_3, dtb_3, go_3, o_all)


def _gdn_sample_kernel(pd_ref, ps_ref, cw_ref, alog_ref, dtb_ref, go_ref, cs_ref, st_ref,
                       o_in_ref, cs_in_ref, st_in_ref, o_ref, cso_ref, sto_ref, *, dk, dv):
    del o_in_ref, cs_in_ref, st_in_ref
    kw = GDN_HEADS * dk
    conv_dim = 2 * kw + GDN_HEADS * dv
    new = pd_ref[:, 0:conv_dim]
    conv = new * cw_ref[CONV_W - 1:CONV_W, :]
    for j in range(CONV_W - 1):
        conv = conv + cs_ref[j] * cw_ref[j:j + 1, :]
    conv = _silu(conv)
    for j in range(CONV_W - 2):
        cso_ref[j] = cs_ref[j + 1]
    cso_ref[CONV_W - 2] = new

    g_all, beta_all = _gdn_gates(ps_ref[...], alog_ref, dtb_ref)
    a_all = jnp.exp(g_all)
    g_out = go_ref[...]
    for h in range(GDN_HEADS):
        tiles = _column_tiles([_l2norm(conv[:, h * dk:(h + 1) * dk]) * (dk ** -0.5),
                               _l2norm(conv[:, kw + h * dk: kw + (h + 1) * dk])])
        v = conv[:, 2 * kw + h * dv: 2 * kw + (h + 1) * dv]
        z = pd_ref[:, conv_dim + h * dv: conv_dim + (h + 1) * dv]
        rows = []
        for s in range(MIX_ROWS):
            a = a_all[s:s + 1, DA_LANE + h: DA_LANE + h + 1]
            beta = beta_all[s:s + 1, DB_LANE + h: DB_LANE + h + 1]
            q_t, k_t = tiles(s)
            st = st_ref[s, h] * a
            u = beta * (v[s:s + 1, :] - jnp.sum(k_t * st, axis=0, keepdims=True))
            st = st + k_t * u
            sto_ref[s, h] = st
            rows.append(jnp.sum(q_t * st, axis=0, keepdims=True))
        o = jnp.concatenate(rows, axis=0)
        o_ref[:, h * dv:(h + 1) * dv] = (_rms(o, g_out) * _silu(z)).astype(o_ref.dtype)


def gdn_sample(pd, ps, cw_3, alog_3, dtb_3, go_3, conv3, state5, layer, o_all, cs_all, st_all,
               *, row0, n_samples, col0):
    m, pd_cols = pd.shape
    dv = go_3.shape[2]
    dk = dv
    width = GDN_HEADS * dv
    conv_dim = cw_3.shape[2]
    blk0 = row0 // MIX_ROWS
    row = lambda i: (blk0 + i, 0)
    cb = col0 // width
    lane_spec = pl.BlockSpec((None, 1, SMALL_COLS), lambda i: (layer, 0, 0))
    cs_spec = pl.BlockSpec((None, CONV_W - 1, MIX_ROWS, conv_dim), lambda i: (layer, 0, i, 0))
    st_spec = pl.BlockSpec((None, MIX_ROWS, GDN_HEADS, dk, dv), lambda i: (layer, i, 0, 0, 0))
    any_spec = pl.BlockSpec(memory_space=pl.ANY)
    in_specs = [pl.BlockSpec((MIX_ROWS, pd_cols), row),
                pl.BlockSpec((MIX_ROWS, SMALL_COLS), row),
                pl.BlockSpec((None, CONV_W, conv_dim), lambda i: (layer, 0, 0)),
                lane_spec, lane_spec,
                pl.BlockSpec((None, 1, dv), lambda i: (layer, 0, 0)),
                cs_spec, st_spec, any_spec]
    args = [pd, ps, cw_3, alog_3, dtb_3, go_3, conv3, state5, o_all]
    aliases = {8: 0}
    kern = functools.partial(_gdn_sample_kernel, dk=dk, dv=dv)
    if st_all is not None:
        in_specs += [any_spec, any_spec]
        args += [cs_all, st_all]
        aliases[9] = 1
        aliases[10] = 2
    else:
        kern = functools.partial(_no_state_in, functools.partial(_no_state_in, kern, 9), 9)
    return pl.pallas_call(
        kern,
        grid=(n_samples // MIX_ROWS,),
        in_specs=in_specs,
        out_specs=[pl.BlockSpec((MIX_ROWS, width), lambda i: (blk0 + i, cb)), cs_spec, st_spec],
        out_shape=[jax.ShapeDtypeStruct(o_all.shape, BF16),
                   jax.ShapeDtypeStruct(conv3.shape, F32),
                   jax.ShapeDtypeStruct(state5.shape, F32)],
        input_output_aliases=aliases,
        compiler_params=_cparams(1),
        name="gdn_sample",
    )(*args)


def _row_tile(m, cap):
    best = 16
    for t in range(16, cap + 1, 16):
        if m % t == 0:
            best = t
    return best


def kernel(x_prompt, x_sample, mem_prompt, cache_mem_k, cache_mem_v, state_gla, state_gdn, state_conv, g_mix, w_in, gla_w_gate2, gla_b_gate, gla_g_out, gdn_conv_w, gdn_a_log, gdn_dt_bias, gdn_g_out, w_out, g_xattn, g_mem, xa_w_q, xa_w_k, xa_w_v, xa_w_o, g_ffn, ffn_w_gate, ffn_w_up, ffn_w_down, g_final):
    batch, seq, d = x_prompt.shape
    n_s = x_sample.shape[0]
    depth = w_in.shape[0]
    n_mem = mem_prompt.shape[1]
    mp = batch * seq
    m = mp + n_s
    gla_kw = gla_w_gate2.shape[2]
    gla_dv = gla_g_out.shape[1]
    gla_width = GLA_HEADS * gla_dv
    gdn_dv = gdn_g_out.shape[1]
    gdn_width = GDN_HEADS * gdn_dv
    conv_dim = gdn_conv_w.shape[2]
    n_gla = 2 * gla_kw + 2 * gla_width
    c_code = n_gla
    c_conv = c_code + GLA_GATE_RANK
    n_gdn = conv_dim + gdn_width
    c_da = c_conv + n_gdn

    tm = _row_tile(m, 1088)
    tm_mem = _row_tile(batch * n_mem, 1024)

    row3 = lambda p: p.reshape(depth, 1, p.shape[-1])
    lane_row = lambda p: jnp.pad(p, ((0, 0), (DA_LANE, SMALL_COLS - DA_LANE - GDN_HEADS))).reshape(depth, 1, SMALL_COLS)
    g_mix3, g_mem3 = row3(g_mix), row3(g_mem)
    g_xattn_col, g_ffn_col = g_xattn.reshape(depth, d, 1), g_ffn.reshape(depth, d, 1)
    g_final3 = g_final.reshape(1, 1, d)
    bg3, gla_go3, gdn_go3 = row3(gla_b_gate), row3(gla_g_out), row3(gdn_g_out)
    alog3, dtb3 = lane_row(gdn_a_log), lane_row(gdn_dt_bias)

    w_in_t = jnp.swapaxes(w_in, 1, 2)

    conv3 = jnp.swapaxes(state_conv, 1, 2)
    mem2 = mem_prompt.reshape(batch * n_mem, d)
    cache_k4 = cache_token_rows(cache_mem_k)
    cache_v4 = cache_token_rows(cache_mem_v)

    x = jnp.concatenate([x_prompt.reshape(mp, d), x_sample.reshape(n_s, d)], axis=0)
    xb, ss = norm_inputs(x, tm)

    p_gla, p_gdn, p_conv, p_mk, p_mv = [], [], [], [], []
    s_gla_all = s_gdn_all = s_conv_all = None
    for l in range(depth):
        mix = (ss, g_mix3)
        pg = matmul(xb, w_in_t, l, tm=tm, tn=1024, col0=0, n_cols=n_gla, transposed=True, norm=mix, name="proj_gla")
        pd = matmul(xb, w_in_t, l, tm=tm, tn=1024, col0=c_conv, n_cols=n_gdn, transposed=True, norm=mix,
                    name="proj_gdn")
        ps = proj_small(xb, w_in_t, l, tm=tm, row_a=c_code, row_b=c_da, norm=mix)

        o, sa = gla_prompt(pg, ps, gla_w_gate2, bg3, gla_go3, l, batch=batch, seq=seq, d_model=d)
        o, sb = gdn_prompt(pd, ps, gdn_conv_w, alog3, dtb3, gdn_go3, l, o, batch=batch, seq=seq, col0=gla_width)
        o, s_gla_all = gla_sample(pg, ps, gla_w_gate2, bg3, gla_go3, state_gla, l, o, s_gla_all,
                                  row0=mp, n_samples=n_s)
        o, s_conv_all, s_gdn_all = gdn_sample(pd, ps, gdn_conv_w, alog3, dtb3, gdn_go3, conv3, state_gdn, l,
                                              o, s_conv_all, s_gdn_all, row0=mp, n_samples=n_s, col0=gla_width)
        p_gla.append(sa)
        p_gdn.append(sb)
        p_conv.append(jnp.stack([pd[(b + 1) * seq - (CONV_W - 1):(b + 1) * seq, :conv_dim] for b in range(batch)]))
        x, xb, ss = matmul(o, w_out, l, tm=tm, tn=1024, res=x, emit_norm_inputs=True, name="w_out")

        mem_n = rmsnorm_rows(mem2, g_mem3, l, BF16, tm_mem)
        mk = matmul(mem_n, xa_w_k, l, tm=tm_mem, tn=1024, name="mem_k")
        mv = matmul(mem_n, xa_w_v, l, tm=tm_mem, tn=1024, name="mem_v")
        p_mk.append(mk)
        p_mv.append(mv)
        qx = matmul(xb, xa_w_q, l, tm=tm, tn=1024, out_dtype=BF16, norm=(ss, g_xattn_col), name="xa_q")
        ox = xattn_prompt(qx, mk, mv, batch=batch, seq=seq, n_mem=n_mem, tq=min(seq, 1024))
        ox = xattn_sample(qx, cache_k4, cache_v4, l, ox, row0=mp, n_samples=n_s)
        x, xb, ss = matmul(ox, xa_w_o, l, tm=tm, tn=1024, res=x, emit_norm_inputs=True, name="xa_o")

        ff, w_down = swiglu(xb, ffn_w_gate, ffn_w_up, ffn_w_down, l, tm=tm, tn=512, norm=(ss, g_ffn_col))
        w_down = w_down[None]
        if l + 1 < depth:
            x, xb, ss = matmul(ff, w_down, 0, tm=tm, tn=512, res=x, emit_norm_inputs=True, name="ffn_down")
        else:
            x = matmul(ff, w_down, 0, tm=tm, tn=512, res=x, name="ffn_down")

    y_prompt = rmsnorm_rows(x, g_final3, 0, F32, _row_tile(mp, 1024), row0=0, n_rows=mp)
    y_sample = rmsnorm_rows(x, g_final3, 0, F32, n_s, row0=mp, n_rows=n_s)
    xa_shape = (depth, batch, n_mem, XA_HEADS, d // XA_HEADS)
    return (y_prompt.reshape(batch, seq, d),
            y_sample.reshape(n_s, 1, d),
            jnp.stack(p_gla),
            jnp.stack(p_gdn),
            jnp.stack(p_conv),
            jnp.stack(p_mk).reshape(xa_shape),
            jnp.stack(p_mv).reshape(xa_shape),
            s_gla_all,
            s_gdn_all,
            jnp.swapaxes(s_conv_all, 1, 2))
```

```python
import functools

import jax
import jax.numpy as jnp
from jax import lax
from jax.experimental import pallas as pl
from jax.experimental.pallas import tpu as pltpu

F32 = jnp.float32
BF16 = jnp.bfloat16

EPS = 1e-6
GLA_HEADS = 4
GLA_GATE_RANK = 16
GLA_GATE_TEMP = 16.0
GDN_HEADS = 8
CONV_W = 4
XA_HEADS = 4

V7X_VMEM_LIMIT_BYTES = 56 * 1024 * 1024
LANES = 128
CHUNK = 128
STEP_CHUNKS = 2
SUB = 8
LOG2E = 1.4426950408889634
MASKED_LOG = -1e30
SMALL_COLS = LANES
DA_LANE = 16
DB_LANE = 24


def _cparams(n_axes):
    return pltpu.CompilerParams(
        dimension_semantics=("arbitrary",) * n_axes,
        vmem_limit_bytes=V7X_VMEM_LIMIT_BYTES)


def _dot(a, b):
    return jnp.dot(a, b, preferred_element_type=F32)


def _dot_nt(a, b):
    return lax.dot_general(a, b, (((1,), (1,)), ((), ())), preferred_element_type=F32)


def _dot_tn(a, b):
    return lax.dot_general(a, b, (((0,), (0,)), ((), ())), preferred_element_type=F32)


def _dot_exact(a, b):
    return jnp.dot(a, b, preferred_element_type=F32, precision=lax.Precision.HIGHEST)


def _silu(x):
    return x * (1.0 / (1.0 + jnp.exp(-x)))


def _sigmoid(x):
    return 1.0 / (1.0 + jnp.exp(-x))


def _softplus(x):
    return jnp.maximum(x, 0.0) + jnp.log1p(jnp.exp(-jnp.abs(x)))


def _log_sigmoid(x):
    return jnp.minimum(x, 0.0) - jnp.log1p(jnp.exp(-jnp.abs(x)))


def _rms(x, g):
    return x * lax.rsqrt(jnp.mean(x * x, axis=-1, keepdims=True) + EPS) * g


def _iota2(shape, axis):
    return lax.broadcasted_iota(jnp.int32, shape, axis)


def _rmsnorm_kernel(x_ref, g_ref, o_ref):
    o_ref[...] = _rms(x_ref[...], g_ref[...]).astype(o_ref.dtype)


def rmsnorm_rows(x, g3, layer, out_dtype, tm, row0=0, n_rows=None):
    m, d = x.shape
    n_rows = m if n_rows is None else n_rows
    blk0 = row0 // tm
    return pl.pallas_call(
        _rmsnorm_kernel,
        grid=(n_rows // tm,),
        in_specs=[pl.BlockSpec((tm, d), lambda i: (blk0 + i, 0)),
                  pl.BlockSpec((None, 1, d), lambda i: (layer, 0, 0))],
        out_specs=pl.BlockSpec((tm, d), lambda i: (i, 0)),
        out_shape=jax.ShapeDtypeStruct((n_rows, d), out_dtype),
        compiler_params=_cparams(1),
        name="rmsnorm",
    )(x, g3)


CAST_ROWS = 256


def _stage_weight(w_ref, wb_ref, gain_ref=None, gain_is_row=False):
    for r in range(0, w_ref.shape[0], CAST_ROWS):
        blk = w_ref[r:r + CAST_ROWS, :]
        if gain_ref is not None:
            blk = blk * (gain_ref[...] if gain_is_row else gain_ref[r:r + CAST_ROWS, :])
        wb_ref[r:r + CAST_ROWS, :] = blk.astype(BF16)


def _row_rsqrt(ss_ref, d):
    tot = ss_ref[:, 0:1]
    for j in range(1, ss_ref.shape[1] // LANES):
        tot = tot + ss_ref[:, j * LANES:j * LANES + 1]
    return lax.rsqrt(tot * (1.0 / d) + EPS)


def _sum_squares(x):
    return jnp.broadcast_to(jnp.sum(x * x, axis=-1, keepdims=True), (x.shape[0], LANES))


def _mm_kernel(*refs, has_res, stage, transposed, has_norm, emit):
    it = iter(refs)
    a_ref, w_ref = next(it), next(it)
    if transposed:
        w_ref = w_ref.at[0]
    ss_ref, gain_ref = (next(it), next(it)) if has_norm else (None, None)
    res_ref = next(it) if has_res else None
    o_ref = next(it)
    ob_ref, sso_ref = (next(it), next(it)) if emit else (None, None)
    if stage:
        wb_ref = next(it)

        @pl.when(pl.program_id(1) == 0)
        def _():
            _stage_weight(w_ref, wb_ref, gain_ref, gain_is_row=transposed)

        w = wb_ref[...]
    else:
        w = w_ref[...]
    acc = _dot_nt(a_ref[...], w) if transposed else _dot(a_ref[...], w)
    if has_norm:
        acc = acc * _row_rsqrt(ss_ref, a_ref.shape[1])
    if has_res:
        acc = res_ref[...] + acc
    o_ref[...] = acc.astype(o_ref.dtype)
    if emit:
        ob_ref[...] = acc.astype(BF16)
        sso_ref[...] = _sum_squares(acc)


def matmul(a, w3, layer, *, tm, tn, col0=0, n_cols=None, res=None, out_dtype=F32, transposed=False,
           norm=None, emit_norm_inputs=False, rows_outer=False, name="matmul"):
    m, k = a.shape
    n_total = w3.shape[1] if transposed else w3.shape[2]
    n_cols = n_total if n_cols is None else n_cols
    n_tiles = n_cols // tn
    assert m % tm == 0 and n_cols % tn == 0
    stage = w3.dtype != BF16
    assert not stage or (tn if transposed else k) % CAST_ROWS == 0
    assert norm is None or stage
    assert not (rows_outer and stage)
    ix = (lambda f: (lambda i, j: f(j, i))) if rows_outer else (lambda f: f)
    if transposed:
        assert col0 % 8 == 0
        w_spec = pl.BlockSpec((pl.Element(1), pl.Element(tn), pl.Element(k)),
                              ix(lambda j, i: (layer, pl.multiple_of(col0 + j * tn, 8), 0)))
        wb_shape = (tn, k)
    else:
        assert col0 % tn == 0
        cb0 = col0 // tn
        w_spec = pl.BlockSpec((None, k, tn), ix(lambda j, i: (layer, 0, cb0 + j)))
        wb_shape = (k, tn)
    in_specs = [pl.BlockSpec((tm, k), ix(lambda j, i: (i, 0))), w_spec]
    args = [a, w3]
    if norm is not None:
        ss, gain3 = norm
        in_specs.append(pl.BlockSpec((tm, ss.shape[1]), ix(lambda j, i: (i, 0))))
        in_specs.append(pl.BlockSpec((None,) + gain3.shape[1:], ix(lambda j, i: (layer, 0, 0))))
        args += [ss, gain3]
    if res is not None:
        in_specs.append(pl.BlockSpec((tm, tn), ix(lambda j, i: (i, j))))
        args.append(res)
    tile = pl.BlockSpec((tm, tn), ix(lambda j, i: (i, j)))
    out_specs, out_shape = tile, jax.ShapeDtypeStruct((m, n_cols), out_dtype)
    if emit_norm_inputs:
        out_specs = [tile, tile, pl.BlockSpec((tm, LANES), ix(lambda j, i: (i, j)))]
        out_shape = [out_shape, jax.ShapeDtypeStruct((m, n_cols), BF16),
                     jax.ShapeDtypeStruct((m, LANES * n_tiles), F32)]
    return pl.pallas_call(
        functools.partial(_mm_kernel, has_res=res is not None, stage=stage, transposed=transposed,
                          has_norm=norm is not None, emit=emit_norm_inputs),
        grid=(m // tm, n_tiles) if rows_outer else (n_tiles, m // tm),
        in_specs=in_specs,
        out_specs=out_specs,
        out_shape=out_shape,
        scratch_shapes=[pltpu.VMEM(wb_shape, BF16)] if stage else [],
        compiler_params=_cparams(2),
        name=name,
    )(*args)


def _norm_inputs_kernel(x_ref, xb_ref, ss_ref):
    x = x_ref[...]
    xb_ref[...] = x.astype(BF16)
    ss_ref[...] = _sum_squares(x)


def norm_inputs(x, tm):
    m, d = x.shape
    return pl.pallas_call(
        _norm_inputs_kernel,
        grid=(m // tm,),
        in_specs=[pl.BlockSpec((tm, d), lambda i: (i, 0))],
        out_specs=[pl.BlockSpec((tm, d), lambda i: (i, 0)), pl.BlockSpec((tm, LANES), lambda i: (i, 0))],
        out_shape=[jax.ShapeDtypeStruct((m, d), BF16), jax.ShapeDtypeStruct((m, LANES), F32)],
        compiler_params=_cparams(1),
        name="norm_inputs",
    )(x)


SMALL_PIECE = 16


def _proj_small_kernel(a_ref, wa_ref, wb_ref, ss_ref, gain_ref, o_ref, w_ref):
    @pl.when(pl.program_id(0) == 0)
    def _():
        w_ref[...] = jnp.zeros_like(w_ref)
        w_ref[0:SMALL_PIECE, :] = (wa_ref[...] * gain_ref[...]).astype(BF16)
        w_ref[SMALL_PIECE:2 * SMALL_PIECE, :] = (wb_ref[...] * gain_ref[...]).astype(BF16)

    o_ref[...] = _dot_nt(a_ref[...], w_ref[...]) * _row_rsqrt(ss_ref, a_ref.shape[1])


def proj_small(a, wt3, layer, *, tm, row_a, row_b, norm):
    m, k = a.shape
    ss, gain3 = norm
    assert row_a % SMALL_PIECE == 0 and row_b % SMALL_PIECE == 0
    piece = lambda r: pl.BlockSpec((None, SMALL_PIECE, k), lambda i: (layer, r // SMALL_PIECE, 0))
    return pl.pallas_call(
        _proj_small_kernel,
        grid=(m // tm,),
        in_specs=[pl.BlockSpec((tm, k), lambda i: (i, 0)), piece(row_a), piece(row_b),
                  pl.BlockSpec((tm, ss.shape[1]), lambda i: (i, 0)),
                  pl.BlockSpec((None, 1, k), lambda i: (layer, 0, 0))],
        out_specs=pl.BlockSpec((tm, SMALL_COLS), lambda i: (i, 0)),
        out_shape=jax.ShapeDtypeStruct((m, SMALL_COLS), F32),
        scratch_shapes=[pltpu.VMEM((SMALL_COLS, k), BF16)],
        compiler_params=_cparams(1),
        name="proj_small",
    )(a, wt3, wt3, ss, gain3)


def _swiglu_kernel(a_ref, wg_ref, wu_ref, ss_ref, gain_ref, wd_ref, o_ref, wdb_ref, wgb_ref, wub_ref):
    @pl.when(pl.program_id(1) == 0)
    def _():
        _stage_weight(wg_ref, wgb_ref, gain_ref)
        _stage_weight(wu_ref, wub_ref, gain_ref)

    a = a_ref[...]
    r = _row_rsqrt(ss_ref, a_ref.shape[1])
    gate = _dot(a, wgb_ref[...]) * r
    up = _dot(a, wub_ref[...]) * r
    o_ref[...] = (_silu(gate) * up).astype(o_ref.dtype)
    wdb_ref[...] = wd_ref[...].astype(BF16)


def swiglu(a, wg3, wu3, wd3, layer, *, tm, tn, norm):
    m, k = a.shape
    f = wg3.shape[2]
    d_out = wd3.shape[2]
    ss, gain3 = norm
    n_j, n_i = f // tn, m // tm
    assert m % tm == 0 and f % tn == 0 and k % CAST_ROWS == 0 and f % (n_j * n_i * 16) == 0
    wd_rows = f // (n_j * n_i)
    wspec = pl.BlockSpec((None, k, tn), lambda j, i: (layer, 0, j))
    return pl.pallas_call(
        _swiglu_kernel,
        grid=(n_j, n_i),
        in_specs=[pl.BlockSpec((tm, k), lambda j, i: (i, 0)), wspec, wspec,
                  pl.BlockSpec((tm, ss.shape[1]), lambda j, i: (i, 0)),
                  pl.BlockSpec((None, k, 1), lambda j, i: (layer, 0, 0)),
                  pl.BlockSpec((None, wd_rows, d_out), lambda j, i: (layer, j * n_i + i, 0))],
        out_specs=[pl.BlockSpec((tm, tn), lambda j, i: (i, j)),
                   pl.BlockSpec((wd_rows, d_out), lambda j, i: (j * n_i + i, 0))],
        out_shape=[jax.ShapeDtypeStruct((m, f), BF16), jax.ShapeDtypeStruct((f, d_out), BF16)],
        scratch_shapes=[pltpu.VMEM((k, tn), BF16), pltpu.VMEM((k, tn), BF16)],
        compiler_params=_cparams(2),
        name="swiglu",
    )(a, wg3, wu3, ss, gain3, wd3)


def _xattn_prompt_kernel(q_ref, k_ref, v_ref, o_ref, *, scale, hd):
    heads = [slice(h * hd, (h + 1) * hd) for h in range(XA_HEADS)]
    s = [_dot_nt(q_ref[:, c], k_ref[:, c].astype(BF16)) * scale for c in heads]
    e = [jnp.exp(x - jnp.max(x, axis=-1, keepdims=True)) for x in s]
    p = [(x / jnp.sum(x, axis=-1, keepdims=True)).astype(BF16) for x in e]
    for x, c in zip(p, heads):
        o_ref[:, c] = _dot(x, v_ref[:, c].astype(BF16)).astype(o_ref.dtype)


def xattn_prompt(qx, mk, mv, *, batch, seq, n_mem, tq):
    m, d = qx.shape
    nt = seq // tq
    qspec = pl.BlockSpec((tq, d), lambda b, t: (b * nt + t, 0))
    kvspec = pl.BlockSpec((n_mem, d), lambda b, t: (b, 0))
    return pl.pallas_call(
        functools.partial(_xattn_prompt_kernel, scale=(d // XA_HEADS) ** -0.5, hd=d // XA_HEADS),
        grid=(batch, nt),
        in_specs=[qspec, kvspec, kvspec],
        out_specs=qspec,
        out_shape=jax.ShapeDtypeStruct((m, d), BF16),
        compiler_params=_cparams(2),
        name="xattn_prompt",
    )(qx, mk, mv)


XS_ROWS = 16
XS_STEP = 2


def _xattn_sample_kernel(q_ref, k_ref, v_ref, o_in_ref, o_ref, qf_ref, of_ref, *, scale, hd):
    del o_in_ref
    j = pl.program_id(1)

    @pl.when(j == 0)
    def _():
        qf_ref[...] = q_ref[...].astype(F32)

    nj = hd // LANES
    per_tok = nj * XA_HEADS
    half = per_tok // 2
    n_mem = k_ref.shape[1] // per_tok
    lanes = lambda h, jj: slice(h * hd + jj * LANES, h * hd + (jj + 1) * LANES)
    for t in range(XS_STEP):
        r = j * XS_STEP + t
        q_row = qf_ref[pl.ds(r, 1), :]
        q_tok = jnp.concatenate([q_row[:, lanes(h, jj)] for jj in range(nj) for h in range(XA_HEADS)], axis=0)
        prod = k_ref[t].reshape(n_mem, per_tok, LANES) * q_tok[None]
        part = prod[:, 0:half, :] + prod[:, half:per_tok, :]
        part = part + pltpu.roll(part, XA_HEADS, axis=1)
        s = jnp.sum(part, axis=-1, keepdims=True) * scale
        e = jnp.exp(s - jnp.max(s, axis=0, keepdims=True))
        p = e / jnp.sum(e, axis=0, keepdims=True)
        v3 = v_ref[t].reshape(n_mem, per_tok, LANES)
        acc = [jnp.sum(v3[:, g * half:(g + 1) * half, :] * p, axis=0) for g in range(2)]
        pieces = []
        for h in range(XA_HEADS):
            for jj in range(nj):
                row = (jj % 2) * XA_HEADS + h
                pieces.append(acc[jj // 2][row:row + 1, :])
        of_ref[pl.ds(r, 1), :] = jnp.concatenate(pieces, axis=1)

    @pl.when(j == pl.num_programs(1) - 1)
    def _():
        o_ref[...] = of_ref[...].astype(o_ref.dtype)


def cache_token_rows(cache):
    depth, s, n_mem, heads, hd = cache.shape
    nj = hd // LANES
    c6 = cache.reshape(depth, s, n_mem, heads, nj, LANES)
    return jnp.transpose(c6, (0, 1, 2, 4, 3, 5)).reshape(depth, s, n_mem * nj * heads, LANES)


def xattn_sample(qx, cache_k4, cache_v4, layer, o_all, *, row0, n_samples):
    m, d = qx.shape
    tok_rows = cache_k4.shape[2]
    hd = d // XA_HEADS
    assert (hd // LANES) * XA_HEADS == 16 and XA_HEADS == 4
    blk0 = row0 // XS_ROWS
    inner = XS_ROWS // XS_STEP
    qspec = pl.BlockSpec((XS_ROWS, d), lambda i, j: (blk0 + i, 0))
    cspec = pl.BlockSpec((None, XS_STEP, tok_rows, LANES), lambda i, j: (layer, i * inner + j, 0, 0))
    return pl.pallas_call(
        functools.partial(_xattn_sample_kernel, scale=hd ** -0.5, hd=hd),
        grid=(n_samples // XS_ROWS, inner),
        in_specs=[qspec, cspec, cspec, pl.BlockSpec(memory_space=pl.ANY)],
        out_specs=qspec,
        out_shape=jax.ShapeDtypeStruct((m, d), BF16),
        scratch_shapes=[pltpu.VMEM((XS_ROWS, d), F32), pltpu.VMEM((XS_ROWS, d), F32)],
        input_output_aliases={3: 0},
        compiler_params=_cparams(2),
        name="xattn_sample",
    )(qx, cache_k4, cache_v4, o_all)


def _gla_log_decay(code, w2_ref, bg_ref):
    pre = _dot_exact(code, w2_ref[...]) + bg_ref[...]
    return _log_sigmoid(pre) * (1.0 / GLA_GATE_TEMP)


def _gla_prompt_kernel(pg_ref, ps_ref, psn_ref, w2_ref, bg_ref, go_ref, o_ref, st_ref, stt_ref, bsc_ref, *, dk, dv):
    c = pl.program_id(1)
    C = CHUNK
    kw = GLA_HEADS * dk

    def chunk_log_decay(small_ref):
        la = _gla_log_decay(small_ref[:, 0:GLA_GATE_RANK], w2_ref, bg_ref)
        tri = (_iota2((C, C), 0) >= _iota2((C, C), 1)).astype(F32)
        la = la * LOG2E
        return jnp.concatenate([_dot_exact(tri, la[cc * C:(cc + 1) * C]) for cc in range(STEP_CHUNKS)], axis=0)

    @pl.when(c == 0)
    def _():
        stt_ref[...] = jnp.zeros_like(stt_ref)
        bsc_ref[...] = chunk_log_decay(ps_ref)

    b_all = bsc_ref[...]
    row_s = _iota2((SUB, dk), 0)
    lane_a = _iota2((SUB, C), 1)
    ones_k = jnp.ones((dk, LANES), BF16)
    g_out = go_ref[...]

    units = [(cc, h) for cc in range(STEP_CHUNKS) for h in range(GLA_HEADS)]
    heads = range(len(units))
    rows = lambda cc: slice(cc * C, (cc + 1) * C)
    col = lambda base, width, h: slice(base + h * width, base + (h + 1) * width)
    q = [pg_ref[rows(cc), col(0, dk, h)] * (dk ** -0.5) for cc, h in units]
    k = [pg_ref[rows(cc), col(kw, dk, h)] for cc, h in units]
    vb = [pg_ref[rows(cc), col(2 * kw, dv, h)].astype(BF16) for cc, h in units]
    b = [b_all[rows(cc), col(0, dk, h)] for cc, h in units]
    b_last = [bh[C - 1:C, :] for bh in b]

    q_in = [(q[u] * jnp.exp2(b[u])).astype(BF16) for u in heads]
    upd = [_dot_tn(vb[u], (k[u] * jnp.exp2(b_last[u] - b[u])).astype(BF16)) for u in heads]
    s_t = [stt_ref[h] for h in range(GLA_HEADS)]
    o = [None] * len(units)
    for cc in range(STEP_CHUNKS):
        for h in range(GLA_HEADS):
            u = cc * GLA_HEADS + h
            o[u] = _dot_nt(q_in[u], s_t[h].astype(BF16))
        for h in range(GLA_HEADS):
            u = cc * GLA_HEADS + h
            s_t[h] = s_t[h] * jnp.exp2(b_last[u]) + upd[u]
    for h in range(GLA_HEADS):
        stt_ref[h] = s_t[h]

    n_sub = C // SUB
    earlier = [[None] for _ in heads]
    for i in range(1, n_sub):
        lo = i * SUB
        for h in heads:
            b_ref = b[h][lo - 1:lo, :]
            q_rel = (q[h][lo:lo + SUB] * jnp.exp2(b[h][lo:lo + SUB] - b_ref)).astype(BF16)
            k_rel = jnp.concatenate([k[h][0:lo] * jnp.exp2(b_ref - b[h][0:lo]), jnp.zeros((C - lo, dk), F32)],
                                    axis=0).astype(BF16)
            earlier[h].append(_dot_nt(q_rel, k_rel))
    sums = []
    for h in heads:
        zs = []
        for i in range(n_sub):
            lo = i * SUB
            qi, ki, bi = q[h][lo:lo + SUB], k[h][lo:lo + SUB], b[h][lo:lo + SUB]
            for s in range(SUB):
                dec = jnp.exp2(jnp.where(row_s >= s, bi - bi[s:s + 1, :], MASKED_LOG))
                zs.append(qi * ki[s:s + 1, :] * dec)
        sums.append(_dot(jnp.concatenate(zs, axis=0).astype(BF16), ones_k))
    a_rows = [[] for _ in heads]
    for i in range(n_sub):
        lo = i * SUB
        for h in heads:
            a_i = jnp.zeros((SUB, C), F32) if i == 0 else earlier[h][i]
            for s in range(SUB):
                r0 = (i * SUB + s) * SUB
                a_i = jnp.where(lane_a == lo + s, sums[h][r0:r0 + SUB, 0:C], a_i)
            a_rows[h].append(a_i)
    o = [o[h] + _dot(jnp.concatenate(a_rows[h], axis=0).astype(BF16), vb[h]) for h in heads]

    for u, (cc, h) in enumerate(units):
        r = pg_ref[rows(cc), col(2 * kw + GLA_HEADS * dv, dv, h)]
        o_ref[rows(cc), col(0, dv, h)] = (_rms(o[u], g_out) * _silu(r)).astype(o_ref.dtype)

    bsc_ref[...] = chunk_log_decay(psn_ref)

    @pl.when(c == pl.num_programs(1) - 1)
    def _():
        for h in range(GLA_HEADS):
            st_ref[0, h] = stt_ref[h].T


def gla_prompt(pg, ps, w2_3, bg_3, go_3, layer, *, batch, seq, d_model):
    m, pg_cols = pg.shape
    kw = w2_3.shape[2]
    dk = kw // GLA_HEADS
    dv = go_3.shape[2]
    width = GLA_HEADS * dv
    step_rows = STEP_CHUNKS * CHUNK
    assert seq % step_rows == 0
    nc = seq // step_rows
    row = lambda b, c: (b * nc + c, 0)
    next_row = lambda b, c: (b * nc + jnp.minimum(c + 1, nc - 1), 0)
    return pl.pallas_call(
        functools.partial(_gla_prompt_kernel, dk=dk, dv=dv),
        grid=(batch, nc),
        in_specs=[pl.BlockSpec((step_rows, pg_cols), row),
                  pl.BlockSpec((step_rows, SMALL_COLS), row),
                  pl.BlockSpec((step_rows, SMALL_COLS), next_row),
                  pl.BlockSpec((None, GLA_GATE_RANK, kw), lambda b, c: (layer, 0, 0)),
                  pl.BlockSpec((None, 1, kw), lambda b, c: (layer, 0, 0)),
                  pl.BlockSpec((None, 1, dv), lambda b, c: (layer, 0, 0))],
        out_specs=[pl.BlockSpec((step_rows, width), row),
                   pl.BlockSpec((1, GLA_HEADS, dk, dv), lambda b, c: (b, 0, 0, 0))],
        out_shape=[jax.ShapeDtypeStruct((m, d_model), BF16),
                   jax.ShapeDtypeStruct((batch, GLA_HEADS, dk, dv), F32)],
        scratch_shapes=[pltpu.VMEM((GLA_HEADS, dv, dk), F32), pltpu.VMEM((step_rows, kw), F32)],
        compiler_params=_cparams(2),
        name="gla_prompt",
    )(pg, ps, ps, w2_3, bg_3, go_3)


MIX_ROWS = 16


def _column_tiles(xs):
    parts = []
    for x in xs:
        head = x.astype(BF16).astype(F32)
        both = jnp.concatenate([head, x - head, jnp.zeros((LANES - 2 * MIX_ROWS, x.shape[1]), F32)], axis=0)
        parts.append(both.T[:, 0:2 * MIX_ROWS].astype(BF16))
    lhs = jnp.concatenate(parts, axis=0)
    sel = jnp.bitwise_and(_iota2((2 * MIX_ROWS, LANES), 0), MIX_ROWS - 1)

    def tiles(s):
        out = _dot(lhs, jnp.where(sel == s, 1.0, 0.0).astype(BF16))
        return [out[j * LANES:(j + 1) * LANES] for j in range(len(xs))]

    return tiles


def _gla_sample_kernel(pg_ref, ps_ref, w2_ref, bg_ref, go_ref, st_ref, o_in_ref, st_in_ref,
                       o_ref, sto_ref, *, dk, dv):
    del o_in_ref, st_in_ref
    kw = GLA_HEADS * dk
    decay = jnp.exp(_gla_log_decay(ps_ref[:, 0:GLA_GATE_RANK], w2_ref, bg_ref))
    g_out = go_ref[...]
    wide = lambda t: jnp.concatenate([t] * (dv // LANES), axis=1)
    for h in range(GLA_HEADS):
        tiles = _column_tiles([pg_ref[:, h * dk:(h + 1) * dk] * (dk ** -0.5),
                               pg_ref[:, kw + h * dk: kw + (h + 1) * dk],
                               decay[:, h * dk:(h + 1) * dk]])
        v = pg_ref[:, 2 * kw + h * dv: 2 * kw + (h + 1) * dv]
        r = pg_ref[:, 2 * kw + GLA_HEADS * dv + h * dv: 2 * kw + GLA_HEADS * dv + (h + 1) * dv]
        rows = []
        for s in range(MIX_ROWS):
            q_t, k_t, a_t = tiles(s)
            st = st_ref[s, h] * wide(a_t) + wide(k_t) * v[s:s + 1, :]
            sto_ref[s, h] = st
            rows.append(jnp.sum(wide(q_t) * st, axis=0, keepdims=True))
        o = jnp.concatenate(rows, axis=0)
        o_ref[:, h * dv:(h + 1) * dv] = (_rms(o, g_out) * _silu(r)).astype(o_ref.dtype)


def gla_sample(pg, ps, w2_3, bg_3, go_3, state5, layer, o_all, st_all, *, row0, n_samples):
    m, pg_cols = pg.shape
    kw = w2_3.shape[2]
    dk = kw // GLA_HEADS
    dv = go_3.shape[2]
    width = GLA_HEADS * dv
    blk0 = row0 // MIX_ROWS
    row = lambda i: (blk0 + i, 0)
    st_spec = pl.BlockSpec((None, MIX_ROWS, GLA_HEADS, dk, dv), lambda i: (layer, i, 0, 0, 0))
    any_spec = pl.BlockSpec(memory_space=pl.ANY)
    in_specs = [pl.BlockSpec((MIX_ROWS, pg_cols), row),
                pl.BlockSpec((MIX_ROWS, SMALL_COLS), row),
                pl.BlockSpec((None, GLA_GATE_RANK, kw), lambda i: (layer, 0, 0)),
                pl.BlockSpec((None, 1, kw), lambda i: (layer, 0, 0)),
                pl.BlockSpec((None, 1, dv), lambda i: (layer, 0, 0)),
                st_spec, any_spec]
    args = [pg, ps, w2_3, bg_3, go_3, state5, o_all]
    aliases = {6: 0}
    kern = functools.partial(_gla_sample_kernel, dk=dk, dv=dv)
    if st_all is not None:
        in_specs.append(any_spec)
        args.append(st_all)
        aliases[7] = 1
    else:
        kern = functools.partial(_no_state_in, kern, 7)
    return pl.pallas_call(
        kern,
        grid=(n_samples // MIX_ROWS,),
        in_specs=in_specs,
        out_specs=[pl.BlockSpec((MIX_ROWS, width), row), st_spec],
        out_shape=[jax.ShapeDtypeStruct(o_all.shape, BF16),
                   jax.ShapeDtypeStruct(state5.shape, F32)],
        input_output_aliases=aliases,
        compiler_params=_cparams(1),
        name="gla_sample",
    )(*args)


def _no_state_in(kern, pos, *refs):
    return kern(*refs[:pos], None, *refs[pos:])


def _gdn_gates(ps, alog_ref, dtb_ref):
    g = -jnp.exp(alog_ref[...]) * _softplus(ps + dtb_ref[...])
    return g, _sigmoid(ps)


def _l2norm(x):
    return x * lax.rsqrt(jnp.sum(x * x, axis=-1, keepdims=True) + EPS)


def _inv_unit_lower(lows):
    n = lows[0].shape[0]
    r = _iota2((n, n), 0)
    c = _iota2((n, n), 1)
    eye = (r == c).astype(F32)
    base = 8
    same = lambda size: jnp.right_shift(r, size.bit_length() - 1) == jnp.right_shift(c, size.bit_length() - 1)
    bf = lambda xs: [x.astype(BF16) for x in xs]
    in_base = same(base)
    l8 = bf([jnp.where(in_base, low, 0.0) for low in lows])
    x = [eye - a.astype(F32) for a in l8]
    p = [_dot(a, a) for a in l8]
    for step in range(2):
        pb = bf(p)
        x = [xi + _dot(xb, pi) for xi, xb, pi in zip(x, bf(x), pb)]
        if step == 0:
            p = [_dot(pi, pi) for pi in pb]
    m = base
    while m < n:
        in_2m, in_m = same(2 * m), same(m)
        off = bf([jnp.where(in_2m, jnp.where(in_m, 0.0, low), 0.0) for low in lows])
        xb = bf(x)
        y = bf([_dot(oi, xi) for oi, xi in zip(off, xb)])
        x = [xi - _dot(xbi, yi) for xi, xbi, yi in zip(x, xb, y)]
        m *= 2
    return x


def _gdn_prompt_kernel(pd_ref, ps_ref, psn_ref, cw_ref, alog_ref, dtb_ref, go_ref, o_in_ref, o_ref, st_ref,
                       xb_ref, s_ref, gsc_ref, *, dk, dv):
    del o_in_ref
    c = pl.program_id(1)
    C = CHUNK
    R = STEP_CHUNKS * C
    kw = GDN_HEADS * dk
    conv_dim = 2 * kw + GDN_HEADS * dv
    halo = 8

    def chunk_log_decay(small_ref):
        g_all, _ = _gdn_gates(small_ref[...], alog_ref, dtb_ref)
        tri = (_iota2((C, C), 0) >= _iota2((C, C), 1)).astype(F32)
        g_all = g_all * LOG2E
        return jnp.concatenate([_dot_exact(tri, g_all[cc * C:(cc + 1) * C]) for cc in range(STEP_CHUNKS)], axis=0)

    @pl.when(c == 0)
    def _():
        xb_ref[0:halo, :] = jnp.zeros((halo, conv_dim), F32)
        s_ref[...] = jnp.zeros_like(s_ref)
        gsc_ref[...] = chunk_log_decay(ps_ref)

    xb_ref[halo:halo + R, :] = pd_ref[:, 0:conv_dim]
    window = xb_ref[...]
    conv = window[halo:halo + R] * cw_ref[CONV_W - 1:CONV_W, :]
    for back in range(1, CONV_W):
        tap = pltpu.roll(window, back, axis=0)[halo:halo + R]
        conv = conv + tap * cw_ref[CONV_W - 1 - back:CONV_W - back, :]
    conv = _silu(conv)
    xb_ref[0:halo, :] = window[R:R + halo]

    beta_all = _sigmoid(ps_ref[...])
    gc_all = gsc_ref[...]
    gr_all = [gc_all[cc * C:(cc + 1) * C].T for cc in range(STEP_CHUNKS)]
    r_i = _iota2((C, C), 0)
    c_i = _iota2((C, C), 1)
    incl = r_i >= c_i
    strict = r_i > c_i
    g_out = go_ref[...]

    units = [(cc, h) for cc in range(STEP_CHUNKS) for h in range(GDN_HEADS)]
    ids = range(len(units))
    rows = lambda cc: slice(cc * C, (cc + 1) * C)
    col = lambda base, width, h: slice(base + h * width, base + (h + 1) * width)
    bf = lambda xs: [x.astype(BF16) for x in xs]
    q = [_l2norm(conv[rows(cc), col(0, dk, h)]) * (dk ** -0.5) for cc, h in units]
    k = [_l2norm(conv[rows(cc), col(kw, dk, h)]) for cc, h in units]
    v = [conv[rows(cc), col(2 * kw, dv, h)] for cc, h in units]
    beta = [beta_all[rows(cc), DB_LANE + h: DB_LANE + h + 1] for cc, h in units]
    gc = [gc_all[rows(cc), DA_LANE + h: DA_LANE + h + 1] for cc, h in units]
    g_last = [g[C - 1:C, :] for g in gc]
    decay = [jnp.exp2(jnp.where(incl, gc[u] - gr_all[cc][DA_LANE + h: DA_LANE + h + 1, :], MASKED_LOG))
             for u, (cc, h) in enumerate(units)]
    kb = [k[u] * beta[u] for u in ids]
    kbf, qbf = bf(k), bf(q)
    kk = [_dot_nt(a, b) for a, b in zip(bf(kb), kbf)]
    qk = [_dot_nt(a, b) for a, b in zip(qbf, kbf)]
    t_inv = bf(_inv_unit_lower([jnp.where(strict, kk[u] * decay[u], 0.0) for u in ids]))
    rhs = bf([jnp.concatenate([v[u] * beta[u], kb[u] * jnp.exp2(gc[u])], axis=-1) for u in ids])
    sol = [_dot(t, x) for t, x in zip(t_inv, rhs)]
    wbf = bf([sol[u][:, dv:dv + dk] for u in ids])
    q_dec = bf([q[u] * jnp.exp2(gc[u]) for u in ids])
    k_dec = bf([k[u] * jnp.exp2(g_last[u] - gc[u]) for u in ids])
    qkd = bf([qk[u] * decay[u] for u in ids])

    s = [s_ref[h] for h in range(GDN_HEADS)]
    o = [None] * len(units)
    for cc in range(STEP_CHUNKS):
        us = [cc * GDN_HEADS + h for h in range(GDN_HEADS)]
        sb = bf(s)
        ws = [_dot(wbf[u], sb[h]) for h, u in enumerate(us)]
        oc = [_dot(q_dec[u], sb[h]) for h, u in enumerate(us)]
        uu = bf([sol[u][:, 0:dv] - ws[h] for h, u in enumerate(us)])
        for h, u in enumerate(us):
            o[u] = oc[h] + _dot(qkd[u], uu[h])
        s_new = [_dot_tn(k_dec[u], uu[h]) for h, u in enumerate(us)]
        s = [jnp.exp2(g_last[u]) * s[h] + s_new[h] for h, u in enumerate(us)]
    for h in range(GDN_HEADS):
        s_ref[h] = s[h]
    for u, (cc, h) in enumerate(units):
        z = pd_ref[rows(cc), col(conv_dim, dv, h)]
        o_ref[rows(cc), col(0, dv, h)] = (_rms(o[u], g_out) * _silu(z)).astype(o_ref.dtype)

    gsc_ref[...] = chunk_log_decay(psn_ref)

    @pl.when(c == pl.num_programs(1) - 1)
    def _():
        st_ref[0] = s_ref[...]


def gdn_prompt(pd, ps, cw_3, alog_3, dtb_3, go_3, layer, o_all, *, batch, seq, col0):
    m, pd_cols = pd.shape
    dv = go_3.shape[2]
    dk = dv
    width = GDN_HEADS * dv
    conv_dim = cw_3.shape[2]
    step_rows = STEP_CHUNKS * CHUNK
    assert seq % step_rows == 0
    nc = seq // step_rows
    row = lambda b, c: (b * nc + c, 0)
    cb = col0 // width
    lane_spec = pl.BlockSpec((None, 1, SMALL_COLS), lambda b, c: (layer, 0, 0))
    return pl.pallas_call(
        functools.partial(_gdn_prompt_kernel, dk=dk, dv=dv),
        grid=(batch, nc),
        in_specs=[pl.BlockSpec((step_rows, pd_cols), row),
                  pl.BlockSpec((step_rows, SMALL_COLS), row),
                  pl.BlockSpec((step_rows, SMALL_COLS), lambda b, c: (b * nc + jnp.minimum(c + 1, nc - 1), 0)),
                  pl.BlockSpec((None, CONV_W, conv_dim), lambda b, c: (layer, 0, 0)),
                  lane_spec, lane_spec,
                  pl.BlockSpec((None, 1, dv), lambda b, c: (layer, 0, 0)),
                  pl.BlockSpec(memory_space=pl.ANY)],
        out_specs=[pl.BlockSpec((step_rows, width), lambda b, c: (b * nc + c, cb)),
                   pl.BlockSpec((1, GDN_HEADS, dk, dv), lambda b, c: (b, 0, 0, 0))],
        out_shape=[jax.ShapeDtypeStruct(o_all.shape, BF16),
                   jax.ShapeDtypeStruct((batch, GDN_HEADS, dk, dv), F32)],
        scratch_shapes=[pltpu.VMEM((step_rows + 8, conv_dim), F32),
                        pltpu.VMEM((GDN_HEADS, dk, dv), F32),
                        pltpu.VMEM((step_rows, SMALL_COLS), F32)],
        input_output_aliases={7: 0},
        compiler_params=_cparams(2),
        name="gdn_prompt",
    )(pd, ps, ps, cw_3, alog_3, dtb_3, go_3, o_all)


def _gdn_sample_kernel(pd_ref, ps_ref, cw_ref, alog_ref, dtb_ref, go_ref, cs_ref, st_ref,
                       o_in_ref, cs_in_ref, st_in_ref, o_ref, cso_ref, sto_ref, *, dk, dv):
    del o_in_ref, cs_in_ref, st_in_ref
    kw = GDN_HEADS * dk
    conv_dim = 2 * kw + GDN_HEADS * dv
    new = pd_ref[:, 0:conv_dim]
    conv = new * cw_ref[CONV_W - 1:CONV_W, :]
    for j in range(CONV_W - 1):
        conv = conv + cs_ref[j] * cw_ref[j:j + 1, :]
    conv = _silu(conv)
    for j in range(CONV_W - 2):
        cso_ref[j] = cs_ref[j + 1]
    cso_ref[CONV_W - 2] = new

    g_all, beta_all = _gdn_gates(ps_ref[...], alog_ref, dtb_ref)
    a_all = jnp.exp(g_all)
    g_out = go_ref[...]
    for h in range(GDN_HEADS):
        tiles = _column_tiles([_l2norm(conv[:, h * dk:(h + 1) * dk]) * (dk ** -0.5),
                               _l2norm(conv[:, kw + h * dk: kw + (h + 1) * dk])])
        v = conv[:, 2 * kw + h * dv: 2 * kw + (h + 1) * dv]
        z = pd_ref[:, conv_dim + h * dv: conv_dim + (h + 1) * dv]
        rows = []
        for s in range(MIX_ROWS):
            a = a_all[s:s + 1, DA_LANE + h: DA_LANE + h + 1]
            beta = beta_all[s:s + 1, DB_LANE + h: DB_LANE + h + 1]
            q_t, k_t = tiles(s)
            st = st_ref[s, h] * a
            u = beta * (v[s:s + 1, :] - jnp.sum(k_t * st, axis=0, keepdims=True))
            st = st + k_t * u
            sto_ref[s, h] = st
            rows.append(jnp.sum(q_t * st, axis=0, keepdims=True))
        o = jnp.concatenate(rows, axis=0)
        o_ref[:, h * dv:(h + 1) * dv] = (_rms(o, g_out) * _silu(z)).astype(o_ref.dtype)


def gdn_sample(pd, ps, cw_3, alog_3, dtb_3, go_3, conv3, state5, layer, o_all, cs_all, st_all,
               *, row0, n_samples, col0):
    m, pd_cols = pd.shape
    dv = go_3.shape[2]
    dk = dv
    width = GDN_HEADS * dv
    conv_dim = cw_3.shape[2]
    blk0 = row0 // MIX_ROWS
    row = lambda i: (blk0 + i, 0)
    cb = col0 // width
    lane_spec = pl.BlockSpec((None, 1, SMALL_COLS), lambda i: (layer, 0, 0))
    cs_spec = pl.BlockSpec((None, CONV_W - 1, MIX_ROWS, conv_dim), lambda i: (layer, 0, i, 0))
    st_spec = pl.BlockSpec((None, MIX_ROWS, GDN_HEADS, dk, dv), lambda i: (layer, i, 0, 0, 0))
    any_spec = pl.BlockSpec(memory_space=pl.ANY)
    in_specs = [pl.BlockSpec((MIX_ROWS, pd_cols), row),
                pl.BlockSpec((MIX_ROWS, SMALL_COLS), row),
                pl.BlockSpec((None, CONV_W, conv_dim), lambda i: (layer, 0, 0)),
                lane_spec, lane_spec,
                pl.BlockSpec((None, 1, dv), lambda i: (layer, 0, 0)),
                cs_spec, st_spec, any_spec]
    args = [pd, ps, cw_3, alog_3, dtb_3, go_3, conv3, state5, o_all]
    aliases = {8: 0}
    kern = functools.partial(_gdn_sample_kernel, dk=dk, dv=dv)
    if st_all is not None:
        in_specs += [any_spec, any_spec]
        args += [cs_all, st_all]
        aliases[9] = 1
        aliases[10] = 2
    else:
        kern = functools.partial(_no_state_in, functools.partial(_no_state_in, kern, 9), 9)
    return pl.pallas_call(
        kern,
        grid=(n_samples // MIX_ROWS,),
        in_specs=in_specs,
        out_specs=[pl.BlockSpec((MIX_ROWS, width), lambda i: (blk0 + i, cb)), cs_spec, st_spec],
        out_shape=[jax.ShapeDtypeStruct(o_all.shape, BF16),
                   jax.ShapeDtypeStruct(conv3.shape, F32),
                   jax.ShapeDtypeStruct(state5.shape, F32)],
        input_output_aliases=aliases,
        compiler_params=_cparams(1),
        name="gdn_sample",
    )(*args)


def _row_tile(m, cap):
    best = 16
    for t in range(16, cap + 1, 16):
        if m % t == 0:
            best = t
    return best


def kernel(x_prompt, x_sample, mem_prompt, cache_mem_k, cache_mem_v, state_gla, state_gdn, state_conv, g_mix, w_in, gla_w_gate2, gla_b_gate, gla_g_out, gdn_conv_w, gdn_a_log, gdn_dt_bias, gdn_g_out, w_out, g_xattn, g_mem, xa_w_q, xa_w_k, xa_w_v, xa_w_o, g_ffn, ffn_w_gate, ffn_w_up, ffn_w_down, g_final):
    batch, seq, d = x_prompt.shape
    n_s = x_sample.shape[0]
    depth = w_in.shape[0]
    n_mem = mem_prompt.shape[1]
    mp = batch * seq
    m = mp + n_s
    gla_kw = gla_w_gate2.shape[2]
    gla_dv = gla_g_out.shape[1]
    gla_width = GLA_HEADS * gla_dv
    gdn_dv = gdn_g_out.shape[1]
    gdn_width = GDN_HEADS * gdn_dv
    conv_dim = gdn_conv_w.shape[2]
    n_gla = 2 * gla_kw + 2 * gla_width
    c_code = n_gla
    c_conv = c_code + GLA_GATE_RANK
    n_gdn = conv_dim + gdn_width
    c_da = c_conv + n_gdn

    tm = _row_tile(m, 1088)
    tm_mem = _row_tile(batch * n_mem, 1024)

    row3 = lambda p: p.reshape(depth, 1, p.shape[-1])
    lane_row = lambda p: jnp.pad(p, ((0, 0), (DA_LANE, SMALL_COLS - DA_LANE - GDN_HEADS))).reshape(depth, 1, SMALL_COLS)
    g_mix3, g_mem3 = row3(g_mix), row3(g_mem)
    g_xattn_col, g_ffn_col = g_xattn.reshape(depth, d, 1), g_ffn.reshape(depth, d, 1)
    g_final3 = g_final.reshape(1, 1, d)
    bg3, gla_go3, gdn_go3 = row3(gla_b_gate), row3(gla_g_out), row3(gdn_g_out)
    alog3, dtb3 = lane_row(gdn_a_log), lane_row(gdn_dt_bias)

    w_in_t = jnp.swapaxes(w_in, 1, 2)

    conv3 = jnp.swapaxes(state_conv, 1, 2)
    mem2 = mem_prompt.reshape(batch * n_mem, d)
    cache_k4 = cache_token_rows(cache_mem_k)
    cache_v4 = cache_token_rows(cache_mem_v)

    x = jnp.concatenate([x_prompt.reshape(mp, d), x_sample.reshape(n_s, d)], axis=0)
    xb, ss = norm_inputs(x, tm)

    p_gla, p_gdn, p_conv, p_mk, p_mv = [], [], [], [], []
    s_gla_all = s_gdn_all = s_conv_all = None
    for l in range(depth):
        mix = (ss, g_mix3)
        pg = matmul(xb, w_in_t, l, tm=tm, tn=1024, col0=0, n_cols=n_gla, transposed=True, norm=mix, name="proj_gla")
        pd = matmul(xb, w_in_t, l, tm=tm, tn=1024, col0=c_conv, n_cols=n_gdn, transposed=True, norm=mix,
                    name="proj_gdn")
        ps = proj_small(xb, w_in_t, l, tm=tm, row_a=c_code, row_b=c_da, norm=mix)

        o, sa = gla_prompt(pg, ps, gla_w_gate2, bg3, gla_go3, l, batch=batch, seq=seq, d_model=d)
        o, sb = gdn_prompt(pd, ps, gdn_conv_w, alog3, dtb3, gdn_go3, l, o, batch=batch, seq=seq, col0=gla_width)
        o, s_gla_all = gla_sample(pg, ps, gla_w_gate2, bg3, gla_go3, state_gla, l, o, s_gla_all,
                                  row0=mp, n_samples=n_s)
        o, s_conv_all, s_gdn_all = gdn_sample(pd, ps, gdn_conv_w, alog3, dtb3, gdn_go3, conv3, state_gdn, l,
                                              o, s_conv_all, s_gdn_all, row0=mp, n_samples=n_s, col0=gla_width)
        p_gla.append(sa)
        p_gdn.append(sb)
        p_conv.append(jnp.stack([pd[(b + 1) * seq - (CONV_W - 1):(b + 1) * seq, :conv_dim] for b in range(batch)]))
        x, xb, ss = matmul(o, w_out, l, tm=tm, tn=1024, res=x, emit_norm_inputs=True, name="w_out")

        mem_n = rmsnorm_rows(mem2, g_mem3, l, BF16, tm_mem)
        mk = matmul(mem_n, xa_w_k, l, tm=tm_mem, tn=1024, name="mem_k")
        mv = matmul(mem_n, xa_w_v, l, tm=tm_mem, tn=1024, name="mem_v")
        p_mk.append(mk)
        p_mv.append(mv)
        qx = matmul(xb, xa_w_q, l, tm=tm, tn=1024, out_dtype=BF16, norm=(ss, g_xattn_col), name="xa_q")
        ox = xattn_prompt(qx, mk, mv, batch=batch, seq=seq, n_mem=n_mem, tq=min(seq, 1024))
        ox = xattn_sample(qx, cache_k4, cache_v4, l, ox, row0=mp, n_samples=n_s)
        x, xb, ss = matmul(ox, xa_w_o, l, tm=tm, tn=1024, res=x, emit_norm_inputs=True, name="xa_o")

        ff, w_down = swiglu(xb, ffn_w_gate, ffn_w_up, ffn_w_down, l, tm=tm, tn=512, norm=(ss, g_ffn_col))
        w_down = w_down[None]
        if l + 1 < depth:
            x, xb, ss = matmul(ff, w_down, 0, tm=tm, tn=512, res=x, emit_norm_inputs=True, rows_outer=True,
                               name="ffn_down")
        else:
            x = matmul(ff, w_down, 0, tm=tm, tn=512, res=x, rows_outer=True, name="ffn_down")

    y_prompt = rmsnorm_rows(x, g_final3, 0, F32, _row_tile(mp, 1024), row0=0, n_rows=mp)
    y_sample = rmsnorm_rows(x, g_final3, 0, F32, n_s, row0=mp, n_rows=n_s)
    xa_shape = (depth, batch, n_mem, XA_HEADS, d // XA_HEADS)
    return (y_prompt.reshape(batch, seq, d),
            y_sample.reshape(n_s, 1, d),
            jnp.stack(p_gla),
            jnp.stack(p_gdn),
            jnp.stack(p_conv),
            jnp.stack(p_mk).reshape(xa_shape),
            jnp.stack(p_mv).reshape(xa_shape),
            s_gla_all,
            s_gdn_all,
            jnp.swapaxes(s_conv_all, 1, 2))
```

```python
import functools

import jax
import jax.numpy as jnp
from jax import lax
from jax.experimental import pallas as pl
from jax.experimental.pallas import tpu as pltpu

F32 = jnp.float32
BF16 = jnp.bfloat16

EPS = 1e-6
GLA_HEADS = 4
GLA_GATE_RANK = 16
GLA_GATE_TEMP = 16.0
GDN_HEADS = 8
CONV_W = 4
XA_HEADS = 4

V7X_VMEM_LIMIT_BYTES = 56 * 1024 * 1024
LANES = 128
CHUNK = 128
STEP_CHUNKS = 2
SUB = 8
LOG2E = 1.4426950408889634
MASKED_LOG = -1e30
SMALL_COLS = LANES
DA_LANE = 16
DB_LANE = 24


def _cparams(n_axes):
    return pltpu.CompilerParams(
        dimension_semantics=("arbitrary",) * n_axes,
        vmem_limit_bytes=V7X_VMEM_LIMIT_BYTES)


def _dot(a, b):
    return jnp.dot(a, b, preferred_element_type=F32)


def _dot_nt(a, b):
    return lax.dot_general(a, b, (((1,), (1,)), ((), ())), preferred_element_type=F32)


def _dot_tn(a, b):
    return lax.dot_general(a, b, (((0,), (0,)), ((), ())), preferred_element_type=F32)


def _dot_exact(a, b):
    return jnp.dot(a, b, preferred_element_type=F32, precision=lax.Precision.HIGHEST)


def _silu(x):
    return x * (1.0 / (1.0 + jnp.exp(-x)))


def _sigmoid(x):
    return 1.0 / (1.0 + jnp.exp(-x))


def _softplus(x):
    return jnp.maximum(x, 0.0) + jnp.log1p(jnp.exp(-jnp.abs(x)))


def _log_sigmoid(x):
    return jnp.minimum(x, 0.0) - jnp.log1p(jnp.exp(-jnp.abs(x)))


def _rms(x, g):
    return x * lax.rsqrt(jnp.mean(x * x, axis=-1, keepdims=True) + EPS) * g


def _iota2(shape, axis):
    return lax.broadcasted_iota(jnp.int32, shape, axis)


def _rmsnorm_kernel(x_ref, g_ref, o_ref):
    o_ref[...] = _rms(x_ref[...], g_ref[...]).astype(o_ref.dtype)


def rmsnorm_rows(x, g3, layer, out_dtype, tm, row0=0, n_rows=None):
    m, d = x.shape
    n_rows = m if n_rows is None else n_rows
    blk0 = row0 // tm
    return pl.pallas_call(
        _rmsnorm_kernel,
        grid=(n_rows // tm,),
        in_specs=[pl.BlockSpec((tm, d), lambda i: (blk0 + i, 0)),
                  pl.BlockSpec((None, 1, d), lambda i: (layer, 0, 0))],
        out_specs=pl.BlockSpec((tm, d), lambda i: (i, 0)),
        out_shape=jax.ShapeDtypeStruct((n_rows, d), out_dtype),
        compiler_params=_cparams(1),
        name="rmsnorm",
    )(x, g3)


CAST_ROWS = 256


def _stage_weight(w_ref, wb_ref, gain_ref=None, gain_is_row=False):
    for r in range(0, w_ref.shape[0], CAST_ROWS):
        blk = w_ref[r:r + CAST_ROWS, :]
        if gain_ref is not None:
            blk = blk * (gain_ref[...] if gain_is_row else gain_ref[r:r + CAST_ROWS, :])
        wb_ref[r:r + CAST_ROWS, :] = blk.astype(BF16)


def _row_rsqrt(ss_ref, d):
    tot = ss_ref[:, 0:1]
    for j in range(1, ss_ref.shape[1] // LANES):
        tot = tot + ss_ref[:, j * LANES:j * LANES + 1]
    return lax.rsqrt(tot * (1.0 / d) + EPS)


def _sum_squares(x):
    return jnp.broadcast_to(jnp.sum(x * x, axis=-1, keepdims=True), (x.shape[0], LANES))


def _mm_kernel(*refs, has_res, stage, transposed, has_norm, emit):
    it = iter(refs)
    a_ref, w_ref = next(it), next(it)
    if transposed:
        w_ref = w_ref.at[0]
    ss_ref, gain_ref = (next(it), next(it)) if has_norm else (None, None)
    res_ref = next(it) if has_res else None
    o_ref = next(it)
    ob_ref, sso_ref = (next(it), next(it)) if emit else (None, None)
    if stage:
        wb_ref = next(it)

        @pl.when(pl.program_id(1) == 0)
        def _():
            _stage_weight(w_ref, wb_ref, gain_ref, gain_is_row=transposed)

        w = wb_ref[...]
    else:
        w = w_ref[...]
    acc = _dot_nt(a_ref[...], w) if transposed else _dot(a_ref[...], w)
    if has_norm:
        acc = acc * _row_rsqrt(ss_ref, a_ref.shape[1])
    if has_res:
        acc = res_ref[...] + acc
    o_ref[...] = acc.astype(o_ref.dtype)
    if emit:
        ob_ref[...] = acc.astype(BF16)
        sso_ref[...] = _sum_squares(acc)


def matmul(a, w3, layer, *, tm, tn, col0=0, n_cols=None, res=None, out_dtype=F32, transposed=False,
           norm=None, emit_norm_inputs=False, name="matmul"):
    m, k = a.shape
    n_total = w3.shape[1] if transposed else w3.shape[2]
    n_cols = n_total if n_cols is None else n_cols
    n_tiles = n_cols // tn
    assert m % tm == 0 and n_cols % tn == 0
    stage = w3.dtype != BF16
    assert not stage or (tn if transposed else k) % CAST_ROWS == 0
    assert norm is None or stage
    if transposed:
        assert col0 % 8 == 0
        w_spec = pl.BlockSpec((pl.Element(1), pl.Element(tn), pl.Element(k)),
                              (lambda j, i: (layer, pl.multiple_of(col0 + j * tn, 8), 0)))
        wb_shape = (tn, k)
    else:
        assert col0 % tn == 0
        cb0 = col0 // tn
        w_spec = pl.BlockSpec((None, k, tn), (lambda j, i: (layer, 0, cb0 + j)))
        wb_shape = (k, tn)
    in_specs = [pl.BlockSpec((tm, k), (lambda j, i: (i, 0))), w_spec]
    args = [a, w3]
    if norm is not None:
        ss, gain3 = norm
        in_specs.append(pl.BlockSpec((tm, ss.shape[1]), (lambda j, i: (i, 0))))
        in_specs.append(pl.BlockSpec((None,) + gain3.shape[1:], (lambda j, i: (layer, 0, 0))))
        args += [ss, gain3]
    if res is not None:
        in_specs.append(pl.BlockSpec((tm, tn), (lambda j, i: (i, j))))
        args.append(res)
    tile = pl.BlockSpec((tm, tn), (lambda j, i: (i, j)))
    out_specs, out_shape = tile, jax.ShapeDtypeStruct((m, n_cols), out_dtype)
    if emit_norm_inputs:
        out_specs = [tile, tile, pl.BlockSpec((tm, LANES), (lambda j, i: (i, j)))]
        out_shape = [out_shape, jax.ShapeDtypeStruct((m, n_cols), BF16),
                     jax.ShapeDtypeStruct((m, LANES * n_tiles), F32)]
    return pl.pallas_call(
        functools.partial(_mm_kernel, has_res=res is not None, stage=stage, transposed=transposed,
                          has_norm=norm is not None, emit=emit_norm_inputs),
        grid=(n_tiles, m // tm),
        in_specs=in_specs,
        out_specs=out_specs,
        out_shape=out_shape,
        scratch_shapes=[pltpu.VMEM(wb_shape, BF16)] if stage else [],
        compiler_params=_cparams(2),
        name=name,
    )(*args)


def _norm_inputs_kernel(x_ref, *refs):
    xo_ref, xb_ref, ss_ref = refs[-3:]
    x = x_ref[...]
    xo_ref[...] = x
    xb_ref[...] = x.astype(BF16)
    ss_ref[...] = _sum_squares(x)


def norm_inputs(x_part, m, row0, tm, filled=None):
    n, d = x_part.shape
    blk0 = row0 // tm
    row = lambda i: (blk0 + i, 0)
    in_specs = [pl.BlockSpec((tm, d), lambda i: (i, 0))]
    args = [x_part]
    aliases = {}
    if filled is not None:
        in_specs += [pl.BlockSpec(memory_space=pl.ANY)] * 3
        args += list(filled)
        aliases = {1: 0, 2: 1, 3: 2}
    return pl.pallas_call(
        _norm_inputs_kernel,
        grid=(n // tm,),
        in_specs=in_specs,
        out_specs=[pl.BlockSpec((tm, d), row), pl.BlockSpec((tm, d), row), pl.BlockSpec((tm, LANES), row)],
        out_shape=[jax.ShapeDtypeStruct((m, d), F32), jax.ShapeDtypeStruct((m, d), BF16),
                   jax.ShapeDtypeStruct((m, LANES), F32)],
        input_output_aliases=aliases,
        compiler_params=_cparams(1),
        name="norm_inputs",
    )(*args)


def _memory_proj_kernel(*refs, per_tok, n_lane_tiles, n_tiles):
    a_ref, w_ref = refs[0], refs[1]
    nat_ref, tok_ref, wb_ref = refs[-3], refs[-2], refs[-1]
    _stage_weight(w_ref, wb_ref)
    acc = _dot(a_ref[...], wb_ref[...])
    nat_ref[...] = acc
    rows, tn = acc.shape
    blocks = tn // LANES
    heads = per_tok // n_lane_tiles
    for j in range(n_tiles):
        @pl.when(pl.program_id(0) == j)
        def _():
            for c in range(blocks):
                h, jj = divmod(j * blocks + c, n_lane_tiles)
                tok_ref[pl.ds(jj * heads + h, rows, stride=per_tok), :] = acc[:, c * LANES:(c + 1) * LANES]


def memory_proj(a, w3, layer, stacked, *, tn, name):
    t, k = a.shape
    depth, _, n = w3.shape
    per_tok = n // LANES
    n_lane_tiles = n // XA_HEADS // LANES
    assert n % tn == 0 and k % CAST_ROWS == 0 and t % 8 == 0
    in_specs = [pl.BlockSpec((t, k), lambda j: (0, 0)),
                pl.BlockSpec((None, k, tn), lambda j: (layer, 0, j))]
    args = [a, w3]
    aliases = {}
    if stacked is not None:
        in_specs.append(pl.BlockSpec(memory_space=pl.ANY))
        args.append(stacked)
        aliases = {2: 1}
    return pl.pallas_call(
        functools.partial(_memory_proj_kernel, per_tok=per_tok, n_lane_tiles=n_lane_tiles, n_tiles=n // tn),
        grid=(n // tn,),
        in_specs=in_specs,
        out_specs=[pl.BlockSpec((t, tn), lambda j: (0, j)),
                   pl.BlockSpec((t * per_tok, LANES), lambda j: (layer, 0))],
        out_shape=[jax.ShapeDtypeStruct((t, n), F32),
                   jax.ShapeDtypeStruct((depth * t * per_tok, LANES), F32)],
        scratch_shapes=[pltpu.VMEM((k, tn), BF16)],
        input_output_aliases=aliases,
        compiler_params=_cparams(1),
        name=name,
    )(*args)


SMALL_PIECE = 16


def _proj_small_kernel(a_ref, wa_ref, wb_ref, ss_ref, gain_ref, o_ref, w_ref):
    @pl.when(pl.program_id(0) == 0)
    def _():
        w_ref[...] = jnp.zeros_like(w_ref)
        w_ref[0:SMALL_PIECE, :] = (wa_ref[...] * gain_ref[...]).astype(BF16)
        w_ref[SMALL_PIECE:2 * SMALL_PIECE, :] = (wb_ref[...] * gain_ref[...]).astype(BF16)

    o_ref[...] = _dot_nt(a_ref[...], w_ref[...]) * _row_rsqrt(ss_ref, a_ref.shape[1])


def proj_small(a, wt3, layer, *, tm, row_a, row_b, norm):
    m, k = a.shape
    ss, gain3 = norm
    assert row_a % SMALL_PIECE == 0 and row_b % SMALL_PIECE == 0
    piece = lambda r: pl.BlockSpec((None, SMALL_PIECE, k), lambda i: (layer, r // SMALL_PIECE, 0))
    return pl.pallas_call(
        _proj_small_kernel,
        grid=(m // tm,),
        in_specs=[pl.BlockSpec((tm, k), lambda i: (i, 0)), piece(row_a), piece(row_b),
                  pl.BlockSpec((tm, ss.shape[1]), lambda i: (i, 0)),
                  pl.BlockSpec((None, 1, k), lambda i: (layer, 0, 0))],
        out_specs=pl.BlockSpec((tm, SMALL_COLS), lambda i: (i, 0)),
        out_shape=jax.ShapeDtypeStruct((m, SMALL_COLS), F32),
        scratch_shapes=[pltpu.VMEM((SMALL_COLS, k), BF16)],
        compiler_params=_cparams(1),
        name="proj_small",
    )(a, wt3, wt3, ss, gain3)


def _swiglu_kernel(a_ref, wg_ref, wu_ref, ss_ref, gain_ref, wd_ref, o_ref, wdb_ref, wgb_ref, wub_ref):
    @pl.when(pl.program_id(1) == 0)
    def _():
        _stage_weight(wg_ref, wgb_ref, gain_ref)
        _stage_weight(wu_ref, wub_ref, gain_ref)

    a = a_ref[...]
    r = _row_rsqrt(ss_ref, a_ref.shape[1])
    gate = _dot(a, wgb_ref[...]) * r
    up = _dot(a, wub_ref[...]) * r
    o_ref[...] = (_silu(gate) * up).astype(o_ref.dtype)
    wdb_ref[...] = wd_ref[...].astype(BF16)


def swiglu(a, wg3, wu3, wd3, layer, *, tm, tn, norm):
    m, k = a.shape
    f = wg3.shape[2]
    d_out = wd3.shape[2]
    ss, gain3 = norm
    n_j, n_i = f // tn, m // tm
    assert m % tm == 0 and f % tn == 0 and k % CAST_ROWS == 0 and f % (n_j * n_i * 16) == 0
    wd_rows = f // (n_j * n_i)
    wspec = pl.BlockSpec((None, k, tn), lambda j, i: (layer, 0, j))
    return pl.pallas_call(
        _swiglu_kernel,
        grid=(n_j, n_i),
        in_specs=[pl.BlockSpec((tm, k), lambda j, i: (i, 0)), wspec, wspec,
                  pl.BlockSpec((tm, ss.shape[1]), lambda j, i: (i, 0)),
                  pl.BlockSpec((None, k, 1), lambda j, i: (layer, 0, 0)),
                  pl.BlockSpec((None, wd_rows, d_out), lambda j, i: (layer, j * n_i + i, 0))],
        out_specs=[pl.BlockSpec((tm, tn), lambda j, i: (i, j)),
                   pl.BlockSpec((wd_rows, d_out), lambda j, i: (j * n_i + i, 0))],
        out_shape=[jax.ShapeDtypeStruct((m, f), BF16), jax.ShapeDtypeStruct((f, d_out), BF16)],
        scratch_shapes=[pltpu.VMEM((k, tn), BF16), pltpu.VMEM((k, tn), BF16)],
        compiler_params=_cparams(2),
        name="swiglu",
    )(a, wg3, wu3, ss, gain3, wd3)


def _xattn_prompt_kernel(q_ref, k_ref, v_ref, o_ref, *, scale, hd):
    heads = [slice(h * hd, (h + 1) * hd) for h in range(XA_HEADS)]
    s = [_dot_nt(q_ref[:, c], k_ref[:, c].astype(BF16)) * scale for c in heads]
    e = [jnp.exp(x - jnp.max(x, axis=-1, keepdims=True)) for x in s]
    p = [(x / jnp.sum(x, axis=-1, keepdims=True)).astype(BF16) for x in e]
    for x, c in zip(p, heads):
        o_ref[:, c] = _dot(x, v_ref[:, c].astype(BF16)).astype(o_ref.dtype)


def xattn_prompt(qx, mk, mv, *, batch, seq, n_mem, tq):
    m, d = qx.shape
    nt = seq // tq
    qspec = pl.BlockSpec((tq, d), lambda b, t: (b * nt + t, 0))
    kvspec = pl.BlockSpec((n_mem, d), lambda b, t: (b, 0))
    return pl.pallas_call(
        functools.partial(_xattn_prompt_kernel, scale=(d // XA_HEADS) ** -0.5, hd=d // XA_HEADS),
        grid=(batch, nt),
        in_specs=[qspec, kvspec, kvspec],
        out_specs=qspec,
        out_shape=jax.ShapeDtypeStruct((m, d), BF16),
        compiler_params=_cparams(2),
        name="xattn_prompt",
    )(qx, mk, mv)


XS_ROWS = 16
XS_STEP = 2


def _xattn_sample_kernel(q_ref, k_ref, v_ref, o_in_ref, o_ref, qf_ref, of_ref, *, scale, hd):
    del o_in_ref
    j = pl.program_id(1)

    @pl.when(j == 0)
    def _():
        qf_ref[...] = q_ref[...].astype(F32)

    nj = hd // LANES
    per_tok = nj * XA_HEADS
    half = per_tok // 2
    n_mem = k_ref.shape[1] // per_tok
    lanes = lambda h, jj: slice(h * hd + jj * LANES, h * hd + (jj + 1) * LANES)
    for t in range(XS_STEP):
        r = j * XS_STEP + t
        q_row = qf_ref[pl.ds(r, 1), :]
        q_tok = jnp.concatenate([q_row[:, lanes(h, jj)] for jj in range(nj) for h in range(XA_HEADS)], axis=0)
        prod = k_ref[t].reshape(n_mem, per_tok, LANES) * q_tok[None]
        part = prod[:, 0:half, :] + prod[:, half:per_tok, :]
        part = part + pltpu.roll(part, XA_HEADS, axis=1)
        s = jnp.sum(part, axis=-1, keepdims=True) * scale
        e = jnp.exp(s - jnp.max(s, axis=0, keepdims=True))
        p = e / jnp.sum(e, axis=0, keepdims=True)
        v3 = v_ref[t].reshape(n_mem, per_tok, LANES)
        acc = [jnp.sum(v3[:, g * half:(g + 1) * half, :] * p, axis=0) for g in range(2)]
        pieces = []
        for h in range(XA_HEADS):
            for jj in range(nj):
                row = (jj % 2) * XA_HEADS + h
                pieces.append(acc[jj // 2][row:row + 1, :])
        of_ref[pl.ds(r, 1), :] = jnp.concatenate(pieces, axis=1)

    @pl.when(j == pl.num_programs(1) - 1)
    def _():
        o_ref[...] = of_ref[...].astype(o_ref.dtype)


def cache_token_rows(cache):
    depth, s, n_mem, heads, hd = cache.shape
    nj = hd // LANES
    c6 = cache.reshape(depth, s, n_mem, heads, nj, LANES)
    return jnp.transpose(c6, (0, 1, 2, 4, 3, 5)).reshape(depth, s, n_mem * nj * heads, LANES)


def xattn_sample(qx, cache_k4, cache_v4, layer, o_all, *, row0, n_samples):
    m, d = qx.shape
    tok_rows = cache_k4.shape[2]
    hd = d // XA_HEADS
    assert (hd // LANES) * XA_HEADS == 16 and XA_HEADS == 4
    blk0 = row0 // XS_ROWS
    inner = XS_ROWS // XS_STEP
    qspec = pl.BlockSpec((XS_ROWS, d), lambda i, j: (blk0 + i, 0))
    cspec = pl.BlockSpec((None, XS_STEP, tok_rows, LANES), lambda i, j: (layer, i * inner + j, 0, 0))
    return pl.pallas_call(
        functools.partial(_xattn_sample_kernel, scale=hd ** -0.5, hd=hd),
        grid=(n_samples // XS_ROWS, inner),
        in_specs=[qspec, cspec, cspec, pl.BlockSpec(memory_space=pl.ANY)],
        out_specs=qspec,
        out_shape=jax.ShapeDtypeStruct((m, d), BF16),
        scratch_shapes=[pltpu.VMEM((XS_ROWS, d), F32), pltpu.VMEM((XS_ROWS, d), F32)],
        input_output_aliases={3: 0},
        compiler_params=_cparams(2),
        name="xattn_sample",
    )(qx, cache_k4, cache_v4, o_all)


def _gla_log_decay(code, w2_ref, bg_ref):
    pre = _dot_exact(code, w2_ref[...]) + bg_ref[...]
    return _log_sigmoid(pre) * (1.0 / GLA_GATE_TEMP)


def _gla_prompt_kernel(pg_ref, ps_ref, psn_ref, w2_ref, bg_ref, go_ref, o_ref, st_ref, stt_ref, bsc_ref, *, dk, dv):
    c = pl.program_id(1)
    C = CHUNK
    kw = GLA_HEADS * dk

    def chunk_log_decay(small_ref):
        la = _gla_log_decay(small_ref[:, 0:GLA_GATE_RANK], w2_ref, bg_ref)
        tri = (_iota2((C, C), 0) >= _iota2((C, C), 1)).astype(F32)
        la = la * LOG2E
        return jnp.concatenate([_dot_exact(tri, la[cc * C:(cc + 1) * C]) for cc in range(STEP_CHUNKS)], axis=0)

    @pl.when(c == 0)
    def _():
        stt_ref[...] = jnp.zeros_like(stt_ref)
        bsc_ref[...] = chunk_log_decay(ps_ref)

    b_all = bsc_ref[...]
    row_s = _iota2((SUB, dk), 0)
    lane_a = _iota2((SUB, C), 1)
    ones_k = jnp.ones((dk, LANES), BF16)
    g_out = go_ref[...]

    units = [(cc, h) for cc in range(STEP_CHUNKS) for h in range(GLA_HEADS)]
    heads = range(len(units))
    rows = lambda cc: slice(cc * C, (cc + 1) * C)
    col = lambda base, width, h: slice(base + h * width, base + (h + 1) * width)
    q = [pg_ref[rows(cc), col(0, dk, h)] * (dk ** -0.5) for cc, h in units]
    k = [pg_ref[rows(cc), col(kw, dk, h)] for cc, h in units]
    vb = [pg_ref[rows(cc), col(2 * kw, dv, h)].astype(BF16) for cc, h in units]
    b = [b_all[rows(cc), col(0, dk, h)] for cc, h in units]
    b_last = [bh[C - 1:C, :] for bh in b]

    q_in = [(q[u] * jnp.exp2(b[u])).astype(BF16) for u in heads]
    upd = [_dot_tn(vb[u], (k[u] * jnp.exp2(b_last[u] - b[u])).astype(BF16)) for u in heads]
    s_t = [stt_ref[h] for h in range(GLA_HEADS)]
    o = [None] * len(units)
    for cc in range(STEP_CHUNKS):
        for h in range(GLA_HEADS):
            u = cc * GLA_HEADS + h
            o[u] = _dot_nt(q_in[u], s_t[h].astype(BF16))
        for h in range(GLA_HEADS):
            u = cc * GLA_HEADS + h
            s_t[h] = s_t[h] * jnp.exp2(b_last[u]) + upd[u]
    for h in range(GLA_HEADS):
        stt_ref[h] = s_t[h]

    n_sub = C // SUB
    earlier = [[None] for _ in heads]
    for i in range(1, n_sub):
        lo = i * SUB
        for h in heads:
            b_ref = b[h][lo - 1:lo, :]
            q_rel = (q[h][lo:lo + SUB] * jnp.exp2(b[h][lo:lo + SUB] - b_ref)).astype(BF16)
            k_rel = jnp.concatenate([k[h][0:lo] * jnp.exp2(b_ref - b[h][0:lo]), jnp.zeros((C - lo, dk), F32)],
                                    axis=0).astype(BF16)
            earlier[h].append(_dot_nt(q_rel, k_rel))
    sums = []
    for h in heads:
        zs = []
        for i in range(n_sub):
            lo = i * SUB
            qi, ki, bi = q[h][lo:lo + SUB], k[h][lo:lo + SUB], b[h][lo:lo + SUB]
            for s in range(SUB):
                dec = jnp.exp2(jnp.where(row_s >= s, bi - bi[s:s + 1, :], MASKED_LOG))
                zs.append(qi * ki[s:s + 1, :] * dec)
        sums.append(_dot(jnp.concatenate(zs, axis=0).astype(BF16), ones_k))
    a_rows = [[] for _ in heads]
    for i in range(n_sub):
        lo = i * SUB
        for h in heads:
            a_i = jnp.zeros((SUB, C), F32) if i == 0 else earlier[h][i]
            for s in range(SUB):
                r0 = (i * SUB + s) * SUB
                a_i = jnp.where(lane_a == lo + s, sums[h][r0:r0 + SUB, 0:C], a_i)
            a_rows[h].append(a_i)
    o = [o[h] + _dot(jnp.concatenate(a_rows[h], axis=0).astype(BF16), vb[h]) for h in heads]

    for u, (cc, h) in enumerate(units):
        r = pg_ref[rows(cc), col(2 * kw + GLA_HEADS * dv, dv, h)]
        o_ref[rows(cc), col(0, dv, h)] = (_rms(o[u], g_out) * _silu(r)).astype(o_ref.dtype)

    bsc_ref[...] = chunk_log_decay(psn_ref)

    @pl.when(c == pl.num_programs(1) - 1)
    def _():
        for h in range(GLA_HEADS):
            st_ref[0, h] = stt_ref[h].T


def gla_prompt(pg, ps, w2_3, bg_3, go_3, layer, *, batch, seq, d_model):
    m, pg_cols = pg.shape
    kw = w2_3.shape[2]
    dk = kw // GLA_HEADS
    dv = go_3.shape[2]
    width = GLA_HEADS * dv
    step_rows = STEP_CHUNKS * CHUNK
    assert seq % step_rows == 0
    nc = seq // step_rows
    row = lambda b, c: (b * nc + c, 0)
    next_row = lambda b, c: (b * nc + jnp.minimum(c + 1, nc - 1), 0)
    return pl.pallas_call(
        functools.partial(_gla_prompt_kernel, dk=dk, dv=dv),
        grid=(batch, nc),
        in_specs=[pl.BlockSpec((step_rows, pg_cols), row),
                  pl.BlockSpec((step_rows, SMALL_COLS), row),
                  pl.BlockSpec((step_rows, SMALL_COLS), next_row),
                  pl.BlockSpec((None, GLA_GATE_RANK, kw), lambda b, c: (layer, 0, 0)),
                  pl.BlockSpec((None, 1, kw), lambda b, c: (layer, 0, 0)),
                  pl.BlockSpec((None, 1, dv), lambda b, c: (layer, 0, 0))],
        out_specs=[pl.BlockSpec((step_rows, width), row),
                   pl.BlockSpec((1, GLA_HEADS, dk, dv), lambda b, c: (b, 0, 0, 0))],
        out_shape=[jax.ShapeDtypeStruct((m, d_model), BF16),
                   jax.ShapeDtypeStruct((batch, GLA_HEADS, dk, dv), F32)],
        scratch_shapes=[pltpu.VMEM((GLA_HEADS, dv, dk), F32), pltpu.VMEM((step_rows, kw), F32)],
        compiler_params=_cparams(2),
        name="gla_prompt",
    )(pg, ps, ps, w2_3, bg_3, go_3)


MIX_ROWS = 16


def _column_tiles(xs):
    parts = []
    for x in xs:
        head = x.astype(BF16).astype(F32)
        both = jnp.concatenate([head, x - head, jnp.zeros((LANES - 2 * MIX_ROWS, x.shape[1]), F32)], axis=0)
        parts.append(both.T[:, 0:2 * MIX_ROWS].astype(BF16))
    lhs = jnp.concatenate(parts, axis=0)
    sel = jnp.bitwise_and(_iota2((2 * MIX_ROWS, LANES), 0), MIX_ROWS - 1)

    def tiles(s):
        out = _dot(lhs, jnp.where(sel == s, 1.0, 0.0).astype(BF16))
        return [out[j * LANES:(j + 1) * LANES] for j in range(len(xs))]

    return tiles


def _gla_sample_kernel(pg_ref, ps_ref, w2_ref, bg_ref, go_ref, st_ref, o_in_ref, st_in_ref,
                       o_ref, sto_ref, *, dk, dv):
    del o_in_ref, st_in_ref
    kw = GLA_HEADS * dk
    decay = jnp.exp(_gla_log_decay(ps_ref[:, 0:GLA_GATE_RANK], w2_ref, bg_ref))
    g_out = go_ref[...]
    wide = lambda t: jnp.concatenate([t] * (dv // LANES), axis=1)
    for h in range(GLA_HEADS):
        tiles = _column_tiles([pg_ref[:, h * dk:(h + 1) * dk] * (dk ** -0.5),
                               pg_ref[:, kw + h * dk: kw + (h + 1) * dk],
                               decay[:, h * dk:(h + 1) * dk]])
        v = pg_ref[:, 2 * kw + h * dv: 2 * kw + (h + 1) * dv]
        r = pg_ref[:, 2 * kw + GLA_HEADS * dv + h * dv: 2 * kw + GLA_HEADS * dv + (h + 1) * dv]
        rows = []
        for s in range(MIX_ROWS):
            q_t, k_t, a_t = tiles(s)
            st = st_ref[s, h] * wide(a_t) + wide(k_t) * v[s:s + 1, :]
            sto_ref[s, h] = st
            rows.append(jnp.sum(wide(q_t) * st, axis=0, keepdims=True))
        o = jnp.concatenate(rows, axis=0)
        o_ref[:, h * dv:(h + 1) * dv] = (_rms(o, g_out) * _silu(r)).astype(o_ref.dtype)


def gla_sample(pg, ps, w2_3, bg_3, go_3, state5, layer, o_all, st_all, *, row0, n_samples):
    m, pg_cols = pg.shape
    kw = w2_3.shape[2]
    dk = kw // GLA_HEADS
    dv = go_3.shape[2]
    width = GLA_HEADS * dv
    blk0 = row0 // MIX_ROWS
    row = lambda i: (blk0 + i, 0)
    st_spec = pl.BlockSpec((None, MIX_ROWS, GLA_HEADS, dk, dv), lambda i: (layer, i, 0, 0, 0))
    any_spec = pl.BlockSpec(memory_space=pl.ANY)
    in_specs = [pl.BlockSpec((MIX_ROWS, pg_cols), row),
                pl.BlockSpec((MIX_ROWS, SMALL_COLS), row),
                pl.BlockSpec((None, GLA_GATE_RANK, kw), lambda i: (layer, 0, 0)),
                pl.BlockSpec((None, 1, kw), lambda i: (layer, 0, 0)),
                pl.BlockSpec((None, 1, dv), lambda i: (layer, 0, 0)),
                st_spec, any_spec]
    args = [pg, ps, w2_3, bg_3, go_3, state5, o_all]
    aliases = {6: 0}
    kern = functools.partial(_gla_sample_kernel, dk=dk, dv=dv)
    if st_all is not None:
        in_specs.append(any_spec)
        args.append(st_all)
        aliases[7] = 1
    else:
        kern = functools.partial(_no_state_in, kern, 7)
    return pl.pallas_call(
        kern,
        grid=(n_samples // MIX_ROWS,),
        in_specs=in_specs,
        out_specs=[pl.BlockSpec((MIX_ROWS, width), row), st_spec],
        out_shape=[jax.ShapeDtypeStruct(o_all.shape, BF16),
                   jax.ShapeDtypeStruct(state5.shape, F32)],
        input_output_aliases=aliases,
        compiler_params=_cparams(1),
        name="gla_sample",
    )(*args)


def _no_state_in(kern, pos, *refs):
    return kern(*refs[:pos], None, *refs[pos:])


def _gdn_gates(ps, alog_ref, dtb_ref):
    g = -jnp.exp(alog_ref[...]) * _softplus(ps + dtb_ref[...])
    return g, _sigmoid(ps)


def _l2norm(x):
    return x * lax.rsqrt(jnp.sum(x * x, axis=-1, keepdims=True) + EPS)


def _inv_unit_lower(lows):
    n = lows[0].shape[0]
    r = _iota2((n, n), 0)
    c = _iota2((n, n), 1)
    eye = (r == c).astype(F32)
    base = 8
    same = lambda size: jnp.right_shift(r, size.bit_length() - 1) == jnp.right_shift(c, size.bit_length() - 1)
    bf = lambda xs: [x.astype(BF16) for x in xs]
    in_base = same(base)
    l8 = bf([jnp.where(in_base, low, 0.0) for low in lows])
    x = [eye - a.astype(F32) for a in l8]
    p = [_dot(a, a) for a in l8]
    for step in range(2):
        pb = bf(p)
        x = [xi + _dot(xb, pi) for xi, xb, pi in zip(x, bf(x), pb)]
        if step == 0:
            p = [_dot(pi, pi) for pi in pb]
    m = base
    while m < n:
        in_2m, in_m = same(2 * m), same(m)
        off = bf([jnp.where(in_2m, jnp.where(in_m, 0.0, low), 0.0) for low in lows])
        xb = bf(x)
        y = bf([_dot(oi, xi) for oi, xi in zip(off, xb)])
        x = [xi - _dot(xbi, yi) for xi, xbi, yi in zip(x, xb, y)]
        m *= 2
    return x


def _gdn_prompt_kernel(pd_ref, ps_ref, psn_ref, cw_ref, alog_ref, dtb_ref, go_ref, o_in_ref, o_ref, st_ref,
                       xb_ref, s_ref, gsc_ref, *, dk, dv):
    del o_in_ref
    c = pl.program_id(1)
    C = CHUNK
    R = STEP_CHUNKS * C
    kw = GDN_HEADS * dk
    conv_dim = 2 * kw + GDN_HEADS * dv
    halo = 8

    def chunk_log_decay(small_ref):
        g_all, _ = _gdn_gates(small_ref[...], alog_ref, dtb_ref)
        tri = (_iota2((C, C), 0) >= _iota2((C, C), 1)).astype(F32)
        g_all = g_all * LOG2E
        return jnp.concatenate([_dot_exact(tri, g_all[cc * C:(cc + 1) * C]) for cc in range(STEP_CHUNKS)], axis=0)

    @pl.when(c == 0)
    def _():
        xb_ref[0:halo, :] = jnp.zeros((halo, conv_dim), F32)
        s_ref[...] = jnp.zeros_like(s_ref)
        gsc_ref[...] = chunk_log_decay(ps_ref)

    xb_ref[halo:halo + R, :] = pd_ref[:, 0:conv_dim]
    window = xb_ref[...]
    conv = window[halo:halo + R] * cw_ref[CONV_W - 1:CONV_W, :]
    for back in range(1, CONV_W):
        tap = pltpu.roll(window, back, axis=0)[halo:halo + R]
        conv = conv + tap * cw_ref[CONV_W - 1 - back:CONV_W - back, :]
    conv = _silu(conv)
    xb_ref[0:halo, :] = window[R:R + halo]

    beta_all = _sigmoid(ps_ref[...])
    gc_all = gsc_ref[...]
    gr_all = [gc_all[cc * C:(cc + 1) * C].T for cc in range(STEP_CHUNKS)]
    r_i = _iota2((C, C), 0)
    c_i = _iota2((C, C), 1)
    incl = r_i >= c_i
    strict = r_i > c_i
    g_out = go_ref[...]

    units = [(cc, h) for cc in range(STEP_CHUNKS) for h in range(GDN_HEADS)]
    ids = range(len(units))
    rows = lambda cc: slice(cc * C, (cc + 1) * C)
    col = lambda base, width, h: slice(base + h * width, base + (h + 1) * width)
    bf = lambda xs: [x.astype(BF16) for x in xs]
    q = [_l2norm(conv[rows(cc), col(0, dk, h)]) * (dk ** -0.5) for cc, h in units]
    k = [_l2norm(conv[rows(cc), col(kw, dk, h)]) for cc, h in units]
    v = [conv[rows(cc), col(2 * kw, dv, h)] for cc, h in units]
    beta = [beta_all[rows(cc), DB_LANE + h: DB_LANE + h + 1] for cc, h in units]
    gc = [gc_all[rows(cc), DA_LANE + h: DA_LANE + h + 1] for cc, h in units]
    g_last = [g[C - 1:C, :] for g in gc]
    decay = [jnp.exp2(jnp.where(incl, gc[u] - gr_all[cc][DA_LANE + h: DA_LANE + h + 1, :], MASKED_LOG))
             for u, (cc, h) in enumerate(units)]
    kb = [k[u] * beta[u] for u in ids]
    kbf, qbf = bf(k), bf(q)
    kk = [_dot_nt(a, b) for a, b in zip(bf(kb), kbf)]
    qk = [_dot_nt(a, b) for a, b in zip(qbf, kbf)]
    t_inv = bf(_inv_unit_lower([jnp.where(strict, kk[u] * decay[u], 0.0) for u in ids]))
    rhs = bf([jnp.concatenate([v[u] * beta[u], kb[u] * jnp.exp2(gc[u])], axis=-1) for u in ids])
    sol = [_dot(t, x) for t, x in zip(t_inv, rhs)]
    wbf = bf([sol[u][:, dv:dv + dk] for u in ids])
    q_dec = bf([q[u] * jnp.exp2(gc[u]) for u in ids])
    k_dec = bf([k[u] * jnp.exp2(g_last[u] - gc[u]) for u in ids])
    qkd = bf([qk[u] * decay[u] for u in ids])

    s = [s_ref[h] for h in range(GDN_HEADS)]
    o = [None] * len(units)
    for cc in range(STEP_CHUNKS):
        us = [cc * GDN_HEADS + h for h in range(GDN_HEADS)]
        sb = bf(s)
        ws = [_dot(wbf[u], sb[h]) for h, u in enumerate(us)]
        oc = [_dot(q_dec[u], sb[h]) for h, u in enumerate(us)]
        uu = bf([sol[u][:, 0:dv] - ws[h] for h, u in enumerate(us)])
        for h, u in enumerate(us):
            o[u] = oc[h] + _dot(qkd[u], uu[h])
        s_new = [_dot_tn(k_dec[u], uu[h]) for h, u in enumerate(us)]
        s = [jnp.exp2(g_last[u]) * s[h] + s_new[h] for h, u in enumerate(us)]
    for h in range(GDN_HEADS):
        s_ref[h] = s[h]
    for u, (cc, h) in enumerate(units):
        z = pd_ref[rows(cc), col(conv_dim, dv, h)]
        o_ref[rows(cc), col(0, dv, h)] = (_rms(o[u], g_out) * _silu(z)).astype(o_ref.dtype)

    gsc_ref[...] = chunk_log_decay(psn_ref)

    @pl.when(c == pl.num_programs(1) - 1)
    def _():
        st_ref[0] = s_ref[...]


def gdn_prompt(pd, ps, cw_3, alog_3, dtb_3, go_3, layer, o_all, *, batch, seq, col0):
    m, pd_cols = pd.shape
    dv = go_3.shape[2]
    dk = dv
    width = GDN_HEADS * dv
    conv_dim = cw_3.shape[2]
    step_rows = STEP_CHUNKS * CHUNK
    assert seq % step_rows == 0
    nc = seq // step_rows
    row = lambda b, c: (b * nc + c, 0)
    cb = col0 // width
    lane_spec = pl.BlockSpec((None, 1, SMALL_COLS), lambda b, c: (layer, 0, 0))
    return pl.pallas_call(
        functools.partial(_gdn_prompt_kernel, dk=dk, dv=dv),
        grid=(batch, nc),
        in_specs=[pl.BlockSpec((step_rows, pd_cols), row),
                  pl.BlockSpec((step_rows, SMALL_COLS), row),
                  pl.BlockSpec((step_rows, SMALL_COLS), lambda b, c: (b * nc + jnp.minimum(c + 1, nc - 1), 0)),
                  pl.BlockSpec((None, CONV_W, conv_dim), lambda b, c: (layer, 0, 0)),
                  lane_spec, lane_spec,
                  pl.BlockSpec((None, 1, dv), lambda b, c: (layer, 0, 0)),
                  pl.BlockSpec(memory_space=pl.ANY)],
        out_specs=[pl.BlockSpec((step_rows, width), lambda b, c: (b * nc + c, cb)),
                   pl.BlockSpec((1, GDN_HEADS, dk, dv), lambda b, c: (b, 0, 0, 0))],
        out_shape=[jax.ShapeDtypeStruct(o_all.shape, BF16),
                   jax.ShapeDtypeStruct((batch, GDN_HEADS, dk, dv), F32)],
        scratch_shapes=[pltpu.VMEM((step_rows + 8, conv_dim), F32),
                        pltpu.VMEM((GDN_HEADS, dk, dv), F32),
                        pltpu.VMEM((step_rows, SMALL_COLS), F32)],
        input_output_aliases={7: 0},
        compiler_params=_cparams(2),
        name="gdn_prompt",
    )(pd, ps, ps, cw_3, alog_3, dtb_3, go_3, o_all)


def _gdn_sample_kernel(pd_ref, ps_ref, cw_ref, alog_ref, dtb_ref, go_ref, cs_ref, st_ref,
                       o_in_ref, cs_in_ref, st_in_ref, o_ref, cso_ref, sto_ref, *, dk, dv):
    del o_in_ref, cs_in_ref, st_in_ref
    kw = GDN_HEADS * dk
    conv_dim = 2 * kw + GDN_HEADS * dv
    new = pd_ref[:, 0:conv_dim]
    conv = new * cw_ref[CONV_W - 1:CONV_W, :]
    for j in range(CONV_W - 1):
        conv = conv + cs_ref[j] * cw_ref[j:j + 1, :]
    conv = _silu(conv)
    for j in range(CONV_W - 2):
        cso_ref[j] = cs_ref[j + 1]
    cso_ref[CONV_W - 2] = new

    g_all, beta_all = _gdn_gates(ps_ref[...], alog_ref, dtb_ref)
    a_all = jnp.exp(g_all)
    g_out = go_ref[...]
    for h in range(GDN_HEADS):
        tiles = _column_tiles([_l2norm(conv[:, h * dk:(h + 1) * dk]) * (dk ** -0.5),
                               _l2norm(conv[:, kw + h * dk: kw + (h + 1) * dk])])
        v = conv[:, 2 * kw + h * dv: 2 * kw + (h + 1) * dv]
        z = pd_ref[:, conv_dim + h * dv: conv_dim + (h + 1) * dv]
        rows = []
        for s in range(MIX_ROWS):
            a = a_all[s:s + 1, DA_LANE + h: DA_LANE + h + 1]
            beta = beta_all[s:s + 1, DB_LANE + h: DB_LANE + h + 1]
            q_t, k_t = tiles(s)
            st = st_ref[s, h] * a
            u = beta * (v[s:s + 1, :] - jnp.sum(k_t * st, axis=0, keepdims=True))
            st = st + k_t * u
            sto_ref[s, h] = st
            rows.append(jnp.sum(q_t * st, axis=0, keepdims=True))
        o = jnp.concatenate(rows, axis=0)
        o_ref[:, h * dv:(h + 1) * dv] = (_rms(o, g_out) * _silu(z)).astype(o_ref.dtype)


def gdn_sample(pd, ps, cw_3, alog_3, dtb_3, go_3, conv3, state5, layer, o_all, cs_all, st_all,
               *, row0, n_samples, col0):
    m, pd_cols = pd.shape
    dv = go_3.shape[2]
    dk = dv
    width = GDN_HEADS * dv
    conv_dim = cw_3.shape[2]
    blk0 = row0 // MIX_ROWS
    row = lambda i: (blk0 + i, 0)
    cb = col0 // width
    lane_spec = pl.BlockSpec((None, 1, SMALL_COLS), lambda i: (layer, 0, 0))
    cs_spec = pl.BlockSpec((None, CONV_W - 1, MIX_ROWS, conv_dim), lambda i: (layer, 0, i, 0))
    st_spec = pl.BlockSpec((None, MIX_ROWS, GDN_HEADS, dk, dv), lambda i: (layer, i, 0, 0, 0))
    any_spec = pl.BlockSpec(memory_space=pl.ANY)
    in_specs = [pl.BlockSpec((MIX_ROWS, pd_cols), row),
                pl.BlockSpec((MIX_ROWS, SMALL_COLS), row),
                pl.BlockSpec((None, CONV_W, conv_dim), lambda i: (layer, 0, 0)),
                lane_spec, lane_spec,
                pl.BlockSpec((None, 1, dv), lambda i: (layer, 0, 0)),
                cs_spec, st_spec, any_spec]
    args = [pd, ps, cw_3, alog_3, dtb_3, go_3, conv3, state5, o_all]
    aliases = {8: 0}
    kern = functools.partial(_gdn_sample_kernel, dk=dk, dv=dv)
    if st_all is not None:
        in_specs += [any_spec, any_spec]
        args += [cs_all, st_all]
        aliases[9] = 1
        aliases[10] = 2
    else:
        kern = functools.partial(_no_state_in, functools.partial(_no_state_in, kern, 9), 9)
    return pl.pallas_call(
        kern,
        grid=(n_samples // MIX_ROWS,),
        in_specs=in_specs,
        out_specs=[pl.BlockSpec((MIX_ROWS, width), lambda i: (blk0 + i, cb)), cs_spec, st_spec],
        out_shape=[jax.ShapeDtypeStruct(o_all.shape, BF16),
                   jax.ShapeDtypeStruct(conv3.shape, F32),
                   jax.ShapeDtypeStruct(state5.shape, F32)],
        input_output_aliases=aliases,
        compiler_params=_cparams(1),
        name="gdn_sample",
    )(*args)


def _row_tile(m, cap):
    best = 16
    for t in range(16, cap + 1, 16):
        if m % t == 0:
            best = t
    return best


def kernel(x_prompt, x_sample, mem_prompt, cache_mem_k, cache_mem_v, state_gla, state_gdn, state_conv, g_mix, w_in, gla_w_gate2, gla_b_gate, gla_g_out, gdn_conv_w, gdn_a_log, gdn_dt_bias, gdn_g_out, w_out, g_xattn, g_mem, xa_w_q, xa_w_k, xa_w_v, xa_w_o, g_ffn, ffn_w_gate, ffn_w_up, ffn_w_down, g_final):
    batch, seq, d = x_prompt.shape
    n_s = x_sample.shape[0]
    depth = w_in.shape[0]
    n_mem = mem_prompt.shape[1]
    mp = batch * seq
    m = mp + n_s
    gla_kw = gla_w_gate2.shape[2]
    gla_dv = gla_g_out.shape[1]
    gla_width = GLA_HEADS * gla_dv
    gdn_dv = gdn_g_out.shape[1]
    gdn_width = GDN_HEADS * gdn_dv
    conv_dim = gdn_conv_w.shape[2]
    n_gla = 2 * gla_kw + 2 * gla_width
    c_code = n_gla
    c_conv = c_code + GLA_GATE_RANK
    n_gdn = conv_dim + gdn_width
    c_da = c_conv + n_gdn

    tm = _row_tile(m, 1088)
    tm_mem = _row_tile(batch * n_mem, 1024)

    row3 = lambda p: p.reshape(depth, 1, p.shape[-1])
    lane_row = lambda p: jnp.pad(p, ((0, 0), (DA_LANE, SMALL_COLS - DA_LANE - GDN_HEADS))).reshape(depth, 1, SMALL_COLS)
    g_mix3, g_mem3 = row3(g_mix), row3(g_mem)
    g_xattn_col, g_ffn_col = g_xattn.reshape(depth, d, 1), g_ffn.reshape(depth, d, 1)
    g_final3 = g_final.reshape(1, 1, d)
    bg3, gla_go3, gdn_go3 = row3(gla_b_gate), row3(gla_g_out), row3(gdn_g_out)
    alog3, dtb3 = lane_row(gdn_a_log), lane_row(gdn_dt_bias)

    w_in_t = jnp.swapaxes(w_in, 1, 2)

    conv3 = jnp.swapaxes(state_conv, 1, 2)
    mem2 = mem_prompt.reshape(batch * n_mem, d)
    cache_k4 = cache_token_rows(cache_mem_k)
    cache_v4 = cache_token_rows(cache_mem_v)

    filled = norm_inputs(x_prompt.reshape(mp, d), m, 0, _row_tile(mp, 1024))
    x, xb, ss = norm_inputs(x_sample.reshape(n_s, d), m, mp, n_s, filled)

    p_gla, p_gdn, p_conv = [], [], []
    s_gla_all = s_gdn_all = s_conv_all = mk_all = mv_all = None
    for l in range(depth):
        mix = (ss, g_mix3)
        pg = matmul(xb, w_in_t, l, tm=tm, tn=1024, col0=0, n_cols=n_gla, transposed=True, norm=mix, name="proj_gla")
        pd = matmul(xb, w_in_t, l, tm=tm, tn=1024, col0=c_conv, n_cols=n_gdn, transposed=True, norm=mix,
                    name="proj_gdn")
        ps = proj_small(xb, w_in_t, l, tm=tm, row_a=c_code, row_b=c_da, norm=mix)

        o, sa = gla_prompt(pg, ps, gla_w_gate2, bg3, gla_go3, l, batch=batch, seq=seq, d_model=d)
        o, sb = gdn_prompt(pd, ps, gdn_conv_w, alog3, dtb3, gdn_go3, l, o, batch=batch, seq=seq, col0=gla_width)
        o, s_gla_all = gla_sample(pg, ps, gla_w_gate2, bg3, gla_go3, state_gla, l, o, s_gla_all,
                                  row0=mp, n_samples=n_s)
        o, s_conv_all, s_gdn_all = gdn_sample(pd, ps, gdn_conv_w, alog3, dtb3, gdn_go3, conv3, state_gdn, l,
                                              o, s_conv_all, s_gdn_all, row0=mp, n_samples=n_s, col0=gla_width)
        p_gla.append(sa)
        p_gdn.append(sb)
        p_conv.append(jnp.stack([pd[(b + 1) * seq - (CONV_W - 1):(b + 1) * seq, :conv_dim] for b in range(batch)]))
        x, xb, ss = matmul(o, w_out, l, tm=tm, tn=1024, res=x, emit_norm_inputs=True, name="w_out")

        mem_n = rmsnorm_rows(mem2, g_mem3, l, BF16, tm_mem)
        mk, mk_all = memory_proj(mem_n, xa_w_k, l, mk_all, tn=1024, name="mem_k")
        mv, mv_all = memory_proj(mem_n, xa_w_v, l, mv_all, tn=1024, name="mem_v")
        qx = matmul(xb, xa_w_q, l, tm=tm, tn=1024, out_dtype=BF16, norm=(ss, g_xattn_col), name="xa_q")
        ox = xattn_prompt(qx, mk, mv, batch=batch, seq=seq, n_mem=n_mem, tq=min(seq, 1024))
        ox = xattn_sample(qx, cache_k4, cache_v4, l, ox, row0=mp, n_samples=n_s)
        x, xb, ss = matmul(ox, xa_w_o, l, tm=tm, tn=1024, res=x, emit_norm_inputs=True, name="xa_o")

        ff, w_down = swiglu(xb, ffn_w_gate, ffn_w_up, ffn_w_down, l, tm=tm, tn=512, norm=(ss, g_ffn_col))
        w_down = w_down[None]
        if l + 1 < depth:
            x, xb, ss = matmul(ff, w_down, 0, tm=tm, tn=512, res=x, emit_norm_inputs=True, name="ffn_down")
        else:
            x = matmul(ff, w_down, 0, tm=tm, tn=512, res=x, name="ffn_down")

    y_prompt = rmsnorm_rows(x, g_final3, 0, F32, _row_tile(mp, 1024), row0=0, n_rows=mp)
    y_sample = rmsnorm_rows(x, g_final3, 0, F32, n_s, row0=mp, n_rows=n_s)
    hd = d // XA_HEADS

    def from_token_rows(rows_all):
        c6 = rows_all.reshape(depth, batch, n_mem, hd // LANES, XA_HEADS, LANES)
        return jnp.transpose(c6, (0, 1, 2, 4, 3, 5)).reshape(depth, batch, n_mem, XA_HEADS, hd)

    return (y_prompt.reshape(batch, seq, d),
            y_sample.reshape(n_s, 1, d),
            jnp.stack(p_gla),
            jnp.stack(p_gdn),
            jnp.stack(p_conv),
            from_token_rows(mk_all),
            from_token_rows(mv_all),
            s_gla_all,
            s_gdn_all,
            jnp.swapaxes(s_conv_all, 1, 2))
```

```python
import functools

import jax
import jax.numpy as jnp
from jax import lax
from jax.experimental import pallas as pl
from jax.experimental.pallas import tpu as pltpu

F32 = jnp.float32
BF16 = jnp.bfloat16

EPS = 1e-6
GLA_HEADS = 4
GLA_GATE_RANK = 16
GLA_GATE_TEMP = 16.0
GDN_HEADS = 8
CONV_W = 4
XA_HEADS = 4

V7X_VMEM_LIMIT_BYTES = 56 * 1024 * 1024
LANES = 128
CHUNK = 128
STEP_CHUNKS = 2
SUB = 8
LOG2E = 1.4426950408889634
MASKED_LOG = -1e30
SMALL_COLS = LANES
DA_LANE = 16
DB_LANE = 24


def _cparams(n_axes):
    return pltpu.CompilerParams(
        dimension_semantics=("arbitrary",) * n_axes,
        vmem_limit_bytes=V7X_VMEM_LIMIT_BYTES)


def _dot(a, b):
    return jnp.dot(a, b, preferred_element_type=F32)


def _dot_nt(a, b):
    return lax.dot_general(a, b, (((1,), (1,)), ((), ())), preferred_element_type=F32)


def _dot_tn(a, b):
    return lax.dot_general(a, b, (((0,), (0,)), ((), ())), preferred_element_type=F32)


def _dot_exact(a, b):
    return jnp.dot(a, b, preferred_element_type=F32, precision=lax.Precision.HIGHEST)


def _silu(x):
    return x * (1.0 / (1.0 + jnp.exp(-x)))


def _sigmoid(x):
    return 1.0 / (1.0 + jnp.exp(-x))


def _softplus(x):
    return jnp.maximum(x, 0.0) + jnp.log1p(jnp.exp(-jnp.abs(x)))


def _log_sigmoid(x):
    return jnp.minimum(x, 0.0) - jnp.log1p(jnp.exp(-jnp.abs(x)))


def _rms(x, g):
    return x * lax.rsqrt(jnp.mean(x * x, axis=-1, keepdims=True) + EPS) * g


def _iota2(shape, axis):
    return lax.broadcasted_iota(jnp.int32, shape, axis)


def _rmsnorm_kernel(x_ref, g_ref, o_ref):
    o_ref[...] = _rms(x_ref[...], g_ref[...]).astype(o_ref.dtype)


def rmsnorm_rows(x, g3, layer, out_dtype, tm, row0=0, n_rows=None):
    m, d = x.shape
    n_rows = m if n_rows is None else n_rows
    blk0 = row0 // tm
    return pl.pallas_call(
        _rmsnorm_kernel,
        grid=(n_rows // tm,),
        in_specs=[pl.BlockSpec((tm, d), lambda i: (blk0 + i, 0)),
                  pl.BlockSpec((None, 1, d), lambda i: (layer, 0, 0))],
        out_specs=pl.BlockSpec((tm, d), lambda i: (i, 0)),
        out_shape=jax.ShapeDtypeStruct((n_rows, d), out_dtype),
        compiler_params=_cparams(1),
        name="rmsnorm",
    )(x, g3)


CAST_ROWS = 256


def _stage_weight(w_ref, wb_ref, gain_ref=None, gain_is_row=False):
    for r in range(0, w_ref.shape[0], CAST_ROWS):
        blk = w_ref[r:r + CAST_ROWS, :]
        if gain_ref is not None:
            blk = blk * (gain_ref[...] if gain_is_row else gain_ref[r:r + CAST_ROWS, :])
        wb_ref[r:r + CAST_ROWS, :] = blk.astype(BF16)


def _row_rsqrt(ss_ref, d):
    tot = ss_ref[:, 0:1]
    for j in range(1, ss_ref.shape[1] // LANES):
        tot = tot + ss_ref[:, j * LANES:j * LANES + 1]
    return lax.rsqrt(tot * (1.0 / d) + EPS)


def _sum_squares(x):
    return jnp.broadcast_to(jnp.sum(x * x, axis=-1, keepdims=True), (x.shape[0], LANES))


def _mm_kernel(*refs, has_res, stage, transposed, has_norm, emit):
    it = iter(refs)
    a_ref, w_ref = next(it), next(it)
    if transposed:
        w_ref = w_ref.at[0]
    ss_ref, gain_ref = (next(it), next(it)) if has_norm else (None, None)
    res_ref = next(it) if has_res else None
    o_ref = next(it)
    ob_ref, sso_ref = (next(it), next(it)) if emit else (None, None)
    if stage:
        wb_ref = next(it)

        @pl.when(pl.program_id(1) == 0)
        def _():
            _stage_weight(w_ref, wb_ref, gain_ref, gain_is_row=transposed)

        w = wb_ref[...]
    else:
        w = w_ref[...]
    acc = _dot_nt(a_ref[...], w) if transposed else _dot(a_ref[...], w)
    if has_norm:
        acc = acc * _row_rsqrt(ss_ref, a_ref.shape[1])
    if has_res:
        acc = res_ref[...] + acc
    o_ref[...] = acc.astype(o_ref.dtype)
    if emit:
        ob_ref[...] = acc.astype(BF16)
        sso_ref[...] = _sum_squares(acc)


def matmul(a, w3, layer, *, tm, tn, col0=0, n_cols=None, res=None, out_dtype=F32, transposed=False,
           norm=None, emit_norm_inputs=False, name="matmul"):
    m, k = a.shape
    n_total = w3.shape[1] if transposed else w3.shape[2]
    n_cols = n_total if n_cols is None else n_cols
    n_tiles = n_cols // tn
    assert m % tm == 0 and n_cols % tn == 0
    stage = w3.dtype != BF16
    assert not stage or (tn if transposed else k) % CAST_ROWS == 0
    assert norm is None or stage
    if transposed:
        assert col0 % 8 == 0
        w_spec = pl.BlockSpec((pl.Element(1), pl.Element(tn), pl.Element(k)),
                              (lambda j, i: (layer, pl.multiple_of(col0 + j * tn, 8), 0)))
        wb_shape = (tn, k)
    else:
        assert col0 % tn == 0
        cb0 = col0 // tn
        w_spec = pl.BlockSpec((None, k, tn), (lambda j, i: (layer, 0, cb0 + j)))
        wb_shape = (k, tn)
    in_specs = [pl.BlockSpec((tm, k), (lambda j, i: (i, 0))), w_spec]
    args = [a, w3]
    if norm is not None:
        ss, gain3 = norm
        in_specs.append(pl.BlockSpec((tm, ss.shape[1]), (lambda j, i: (i, 0))))
        in_specs.append(pl.BlockSpec((None,) + gain3.shape[1:], (lambda j, i: (layer, 0, 0))))
        args += [ss, gain3]
    if res is not None:
        in_specs.append(pl.BlockSpec((tm, tn), (lambda j, i: (i, j))))
        args.append(res)
    tile = pl.BlockSpec((tm, tn), (lambda j, i: (i, j)))
    out_specs, out_shape = tile, jax.ShapeDtypeStruct((m, n_cols), out_dtype)
    if emit_norm_inputs:
        out_specs = [tile, tile, pl.BlockSpec((tm, LANES), (lambda j, i: (i, j)))]
        out_shape = [out_shape, jax.ShapeDtypeStruct((m, n_cols), BF16),
                     jax.ShapeDtypeStruct((m, LANES * n_tiles), F32)]
    return pl.pallas_call(
        functools.partial(_mm_kernel, has_res=res is not None, stage=stage, transposed=transposed,
                          has_norm=norm is not None, emit=emit_norm_inputs),
        grid=(n_tiles, m // tm),
        in_specs=in_specs,
        out_specs=out_specs,
        out_shape=out_shape,
        scratch_shapes=[pltpu.VMEM(wb_shape, BF16)] if stage else [],
        compiler_params=_cparams(2),
        name=name,
    )(*args)


def _norm_inputs_kernel(x_ref, *refs):
    xo_ref, xb_ref, ss_ref = refs[-3:]
    x = x_ref[...]
    xo_ref[...] = x
    xb_ref[...] = x.astype(BF16)
    ss_ref[...] = _sum_squares(x)


def norm_inputs(x_part, m, row0, tm, filled=None):
    n, d = x_part.shape
    blk0 = row0 // tm
    row = lambda i: (blk0 + i, 0)
    in_specs = [pl.BlockSpec((tm, d), lambda i: (i, 0))]
    args = [x_part]
    aliases = {}
    if filled is not None:
        in_specs += [pl.BlockSpec(memory_space=pl.ANY)] * 3
        args += list(filled)
        aliases = {1: 0, 2: 1, 3: 2}
    return pl.pallas_call(
        _norm_inputs_kernel,
        grid=(n // tm,),
        in_specs=in_specs,
        out_specs=[pl.BlockSpec((tm, d), row), pl.BlockSpec((tm, d), row), pl.BlockSpec((tm, LANES), row)],
        out_shape=[jax.ShapeDtypeStruct((m, d), F32), jax.ShapeDtypeStruct((m, d), BF16),
                   jax.ShapeDtypeStruct((m, LANES), F32)],
        input_output_aliases=aliases,
        compiler_params=_cparams(1),
        name="norm_inputs",
    )(*args)


def _memory_proj_kernel(*refs, per_tok, n_lane_tiles, n_tiles):
    a_ref, w_ref = refs[0], refs[1]
    nat_ref, tok_ref, wb_ref = refs[-3], refs[-2], refs[-1]
    _stage_weight(w_ref, wb_ref)
    acc = _dot(a_ref[...], wb_ref[...])
    nat_ref[...] = acc
    rows, tn = acc.shape
    blocks = tn // LANES
    heads = per_tok // n_lane_tiles
    for j in range(n_tiles):
        @pl.when(pl.program_id(0) == j)
        def _():
            for c in range(blocks):
                h, jj = divmod(j * blocks + c, n_lane_tiles)
                tok_ref[pl.ds(jj * heads + h, rows, stride=per_tok), :] = acc[:, c * LANES:(c + 1) * LANES]


def memory_proj(a, w3, layer, stacked, *, tn, name):
    t, k = a.shape
    depth, _, n = w3.shape
    per_tok = n // LANES
    n_lane_tiles = n // XA_HEADS // LANES
    assert n % tn == 0 and k % CAST_ROWS == 0 and t % 8 == 0
    in_specs = [pl.BlockSpec((t, k), lambda j: (0, 0)),
                pl.BlockSpec((None, k, tn), lambda j: (layer, 0, j))]
    args = [a, w3]
    aliases = {}
    if stacked is not None:
        in_specs.append(pl.BlockSpec(memory_space=pl.ANY))
        args.append(stacked)
        aliases = {2: 1}
    return pl.pallas_call(
        functools.partial(_memory_proj_kernel, per_tok=per_tok, n_lane_tiles=n_lane_tiles, n_tiles=n // tn),
        grid=(n // tn,),
        in_specs=in_specs,
        out_specs=[pl.BlockSpec((t, tn), lambda j: (0, j)),
                   pl.BlockSpec((t * per_tok, LANES), lambda j: (layer, 0))],
        out_shape=[jax.ShapeDtypeStruct((t, n), F32),
                   jax.ShapeDtypeStruct((depth * t * per_tok, LANES), F32)],
        scratch_shapes=[pltpu.VMEM((k, tn), BF16)],
        input_output_aliases=aliases,
        compiler_params=_cparams(1),
        name=name,
    )(*args)


SMALL_PIECE = 16


def _proj_small_kernel(a_ref, wa_ref, wb_ref, ss_ref, gain_ref, o_ref, w_ref):
    @pl.when(pl.program_id(0) == 0)
    def _():
        w_ref[...] = jnp.zeros_like(w_ref)
        w_ref[0:SMALL_PIECE, :] = (wa_ref[...] * gain_ref[...]).astype(BF16)
        w_ref[SMALL_PIECE:2 * SMALL_PIECE, :] = (wb_ref[...] * gain_ref[...]).astype(BF16)

    o_ref[...] = _dot_nt(a_ref[...], w_ref[...]) * _row_rsqrt(ss_ref, a_ref.shape[1])


def proj_small(a, wt3, layer, *, tm, row_a, row_b, norm):
    m, k = a.shape
    ss, gain3 = norm
    assert row_a % SMALL_PIECE == 0 and row_b % SMALL_PIECE == 0
    piece = lambda r: pl.BlockSpec((None, SMALL_PIECE, k), lambda i: (layer, r // SMALL_PIECE, 0))
    return pl.pallas_call(
        _proj_small_kernel,
        grid=(m // tm,),
        in_specs=[pl.BlockSpec((tm, k), lambda i: (i, 0)), piece(row_a), piece(row_b),
                  pl.BlockSpec((tm, ss.shape[1]), lambda i: (i, 0)),
                  pl.BlockSpec((None, 1, k), lambda i: (layer, 0, 0))],
        out_specs=pl.BlockSpec((tm, SMALL_COLS), lambda i: (i, 0)),
        out_shape=jax.ShapeDtypeStruct((m, SMALL_COLS), F32),
        scratch_shapes=[pltpu.VMEM((SMALL_COLS, k), BF16)],
        compiler_params=_cparams(1),
        name="proj_small",
    )(a, wt3, wt3, ss, gain3)


def _swiglu_kernel(a_ref, wg_ref, wu_ref, ss_ref, gain_ref, wd_ref, o_ref, wdb_ref, wgb_ref, wub_ref):
    @pl.when(pl.program_id(1) == 0)
    def _():
        _stage_weight(wg_ref, wgb_ref, gain_ref)
        _stage_weight(wu_ref, wub_ref, gain_ref)

    a = a_ref[...]
    r = _row_rsqrt(ss_ref, a_ref.shape[1])
    gate = _dot(a, wgb_ref[...]) * r
    up = _dot(a, wub_ref[...]) * r
    o_ref[...] = (_silu(gate) * up).astype(o_ref.dtype)
    wdb_ref[...] = wd_ref[...].astype(BF16)


def swiglu(a, wg3, wu3, wd3, layer, *, tm, tn, norm):
    m, k = a.shape
    f = wg3.shape[2]
    d_out = wd3.shape[2]
    ss, gain3 = norm
    n_j, n_i = f // tn, m // tm
    assert m % tm == 0 and f % tn == 0 and k % CAST_ROWS == 0 and f % (n_j * n_i * 16) == 0
    wd_rows = f // (n_j * n_i)
    wspec = pl.BlockSpec((None, k, tn), lambda j, i: (layer, 0, j))
    return pl.pallas_call(
        _swiglu_kernel,
        grid=(n_j, n_i),
        in_specs=[pl.BlockSpec((tm, k), lambda j, i: (i, 0)), wspec, wspec,
                  pl.BlockSpec((tm, ss.shape[1]), lambda j, i: (i, 0)),
                  pl.BlockSpec((None, k, 1), lambda j, i: (layer, 0, 0)),
                  pl.BlockSpec((None, wd_rows, d_out), lambda j, i: (layer, j * n_i + i, 0))],
        out_specs=[pl.BlockSpec((tm, tn), lambda j, i: (i, j)),
                   pl.BlockSpec((wd_rows, d_out), lambda j, i: (j * n_i + i, 0))],
        out_shape=[jax.ShapeDtypeStruct((m, f), BF16), jax.ShapeDtypeStruct((f, d_out), BF16)],
        scratch_shapes=[pltpu.VMEM((k, tn), BF16), pltpu.VMEM((k, tn), BF16)],
        compiler_params=_cparams(2),
        name="swiglu",
    )(a, wg3, wu3, ss, gain3, wd3)


def _xattn_prompt_kernel(q_ref, k_ref, v_ref, o_ref, *, scale, hd):
    heads = [slice(h * hd, (h + 1) * hd) for h in range(XA_HEADS)]
    s = [_dot_nt(q_ref[:, c], k_ref[:, c].astype(BF16)) * scale for c in heads]
    e = [jnp.exp(x - jnp.max(x, axis=-1, keepdims=True)) for x in s]
    p = [(x / jnp.sum(x, axis=-1, keepdims=True)).astype(BF16) for x in e]
    for x, c in zip(p, heads):
        o_ref[:, c] = _dot(x, v_ref[:, c].astype(BF16)).astype(o_ref.dtype)


def xattn_prompt(qx, mk, mv, *, batch, seq, n_mem, tq):
    m, d = qx.shape
    nt = seq // tq
    qspec = pl.BlockSpec((tq, d), lambda b, t: (b * nt + t, 0))
    kvspec = pl.BlockSpec((n_mem, d), lambda b, t: (b, 0))
    return pl.pallas_call(
        functools.partial(_xattn_prompt_kernel, scale=(d // XA_HEADS) ** -0.5, hd=d // XA_HEADS),
        grid=(batch, nt),
        in_specs=[qspec, kvspec, kvspec],
        out_specs=qspec,
        out_shape=jax.ShapeDtypeStruct((m, d), BF16),
        compiler_params=_cparams(2),
        name="xattn_prompt",
    )(qx, mk, mv)


XS_ROWS = 16
XS_STEP = 4


def _xattn_sample_kernel(q_ref, k_ref, v_ref, o_in_ref, o_ref, qf_ref, of_ref, *, scale, hd):
    del o_in_ref
    j = pl.program_id(1)

    @pl.when(j == 0)
    def _():
        qf_ref[...] = q_ref[...].astype(F32)

    nj = hd // LANES
    per_tok = nj * XA_HEADS
    half = per_tok // 2
    n_mem = k_ref.shape[1] // per_tok
    lanes = lambda h, jj: slice(h * hd + jj * LANES, h * hd + (jj + 1) * LANES)
    for t in range(XS_STEP):
        r = j * XS_STEP + t
        q_row = qf_ref[pl.ds(r, 1), :]
        q_tok = jnp.concatenate([q_row[:, lanes(h, jj)] for jj in range(nj) for h in range(XA_HEADS)], axis=0)
        prod = k_ref[t].reshape(n_mem, per_tok, LANES) * q_tok[None]
        part = prod[:, 0:half, :] + prod[:, half:per_tok, :]
        part = part + pltpu.roll(part, XA_HEADS, axis=1)
        s = jnp.sum(part, axis=-1, keepdims=True) * scale
        e = jnp.exp(s - jnp.max(s, axis=0, keepdims=True))
        p = e / jnp.sum(e, axis=0, keepdims=True)
        v3 = v_ref[t].reshape(n_mem, per_tok, LANES)
        acc = [jnp.sum(v3[:, g * half:(g + 1) * half, :] * p, axis=0) for g in range(2)]
        pieces = []
        for h in range(XA_HEADS):
            for jj in range(nj):
                row = (jj % 2) * XA_HEADS + h
                pieces.append(acc[jj // 2][row:row + 1, :])
        of_ref[pl.ds(r, 1), :] = jnp.concatenate(pieces, axis=1)

    @pl.when(j == pl.num_programs(1) - 1)
    def _():
        o_ref[...] = of_ref[...].astype(o_ref.dtype)


def cache_token_rows(cache):
    depth, s, n_mem, heads, hd = cache.shape
    nj = hd // LANES
    c6 = cache.reshape(depth, s, n_mem, heads, nj, LANES)
    return jnp.transpose(c6, (0, 1, 2, 4, 3, 5)).reshape(depth, s, n_mem * nj * heads, LANES)


def xattn_sample(qx, cache_k4, cache_v4, layer, o_all, *, row0, n_samples):
    m, d = qx.shape
    tok_rows = cache_k4.shape[2]
    hd = d // XA_HEADS
    assert (hd // LANES) * XA_HEADS == 16 and XA_HEADS == 4
    blk0 = row0 // XS_ROWS
    inner = XS_ROWS // XS_STEP
    qspec = pl.BlockSpec((XS_ROWS, d), lambda i, j: (blk0 + i, 0))
    cspec = pl.BlockSpec((None, XS_STEP, tok_rows, LANES), lambda i, j: (layer, i * inner + j, 0, 0))
    return pl.pallas_call(
        functools.partial(_xattn_sample_kernel, scale=hd ** -0.5, hd=hd),
        grid=(n_samples // XS_ROWS, inner),
        in_specs=[qspec, cspec, cspec, pl.BlockSpec(memory_space=pl.ANY)],
        out_specs=qspec,
        out_shape=jax.ShapeDtypeStruct((m, d), BF16),
        scratch_shapes=[pltpu.VMEM((XS_ROWS, d), F32), pltpu.VMEM((XS_ROWS, d), F32)],
        input_output_aliases={3: 0},
        compiler_params=_cparams(2),
        name="xattn_sample",
    )(qx, cache_k4, cache_v4, o_all)


def _gla_log_decay(code, w2_ref, bg_ref):
    pre = _dot_exact(code, w2_ref[...]) + bg_ref[...]
    return _log_sigmoid(pre) * (1.0 / GLA_GATE_TEMP)


def _gla_prompt_kernel(pg_ref, ps_ref, psn_ref, w2_ref, bg_ref, go_ref, o_ref, st_ref, stt_ref, bsc_ref, *, dk, dv):
    c = pl.program_id(1)
    C = CHUNK
    kw = GLA_HEADS * dk

    def chunk_log_decay(small_ref):
        la = _gla_log_decay(small_ref[:, 0:GLA_GATE_RANK], w2_ref, bg_ref)
        tri = (_iota2((C, C), 0) >= _iota2((C, C), 1)).astype(F32)
        la = la * LOG2E
        return jnp.concatenate([_dot_exact(tri, la[cc * C:(cc + 1) * C]) for cc in range(STEP_CHUNKS)], axis=0)

    @pl.when(c == 0)
    def _():
        stt_ref[...] = jnp.zeros_like(stt_ref)
        bsc_ref[...] = chunk_log_decay(ps_ref)

    b_all = bsc_ref[...]
    row_s = _iota2((SUB, dk), 0)
    lane_a = _iota2((SUB, C), 1)
    ones_k = jnp.ones((dk, LANES), BF16)
    g_out = go_ref[...]

    units = [(cc, h) for cc in range(STEP_CHUNKS) for h in range(GLA_HEADS)]
    heads = range(len(units))
    rows = lambda cc: slice(cc * C, (cc + 1) * C)
    col = lambda base, width, h: slice(base + h * width, base + (h + 1) * width)
    q = [pg_ref[rows(cc), col(0, dk, h)] * (dk ** -0.5) for cc, h in units]
    k = [pg_ref[rows(cc), col(kw, dk, h)] for cc, h in units]
    vb = [pg_ref[rows(cc), col(2 * kw, dv, h)].astype(BF16) for cc, h in units]
    b = [b_all[rows(cc), col(0, dk, h)] for cc, h in units]
    b_last = [bh[C - 1:C, :] for bh in b]

    q_in = [(q[u] * jnp.exp2(b[u])).astype(BF16) for u in heads]
    upd = [_dot_tn(vb[u], (k[u] * jnp.exp2(b_last[u] - b[u])).astype(BF16)) for u in heads]
    s_t = [stt_ref[h] for h in range(GLA_HEADS)]
    o = [None] * len(units)
    for cc in range(STEP_CHUNKS):
        for h in range(GLA_HEADS):
            u = cc * GLA_HEADS + h
            o[u] = _dot_nt(q_in[u], s_t[h].astype(BF16))
        for h in range(GLA_HEADS):
            u = cc * GLA_HEADS + h
            s_t[h] = s_t[h] * jnp.exp2(b_last[u]) + upd[u]
    for h in range(GLA_HEADS):
        stt_ref[h] = s_t[h]

    n_sub = C // SUB
    earlier = [[None] for _ in heads]
    for i in range(1, n_sub):
        lo = i * SUB
        for h in heads:
            b_ref = b[h][lo - 1:lo, :]
            q_rel = (q[h][lo:lo + SUB] * jnp.exp2(b[h][lo:lo + SUB] - b_ref)).astype(BF16)
            k_rel = jnp.concatenate([k[h][0:lo] * jnp.exp2(b_ref - b[h][0:lo]), jnp.zeros((C - lo, dk), F32)],
                                    axis=0).astype(BF16)
            earlier[h].append(_dot_nt(q_rel, k_rel))
    sums = []
    for h in heads:
        zs = []
        for i in range(n_sub):
            lo = i * SUB
            qi, ki, bi = q[h][lo:lo + SUB], k[h][lo:lo + SUB], b[h][lo:lo + SUB]
            for s in range(SUB):
                dec = jnp.exp2(jnp.where(row_s >= s, bi - bi[s:s + 1, :], MASKED_LOG))
                zs.append(qi * ki[s:s + 1, :] * dec)
        sums.append(_dot(jnp.concatenate(zs, axis=0).astype(BF16), ones_k))
    a_rows = [[] for _ in heads]
    for i in range(n_sub):
        lo = i * SUB
        for h in heads:
            a_i = jnp.zeros((SUB, C), F32) if i == 0 else earlier[h][i]
            for s in range(SUB):
                r0 = (i * SUB + s) * SUB
                a_i = jnp.where(lane_a == lo + s, sums[h][r0:r0 + SUB, 0:C], a_i)
            a_rows[h].append(a_i)
    o = [o[h] + _dot(jnp.concatenate(a_rows[h], axis=0).astype(BF16), vb[h]) for h in heads]

    for u, (cc, h) in enumerate(units):
        r = pg_ref[rows(cc), col(2 * kw + GLA_HEADS * dv, dv, h)]
        o_ref[rows(cc), col(0, dv, h)] = (_rms(o[u], g_out) * _silu(r)).astype(o_ref.dtype)

    bsc_ref[...] = chunk_log_decay(psn_ref)

    @pl.when(c == pl.num_programs(1) - 1)
    def _():
        for h in range(GLA_HEADS):
            st_ref[0, h] = stt_ref[h].T


def gla_prompt(pg, ps, w2_3, bg_3, go_3, layer, *, batch, seq, d_model):
    m, pg_cols = pg.shape
    kw = w2_3.shape[2]
    dk = kw // GLA_HEADS
    dv = go_3.shape[2]
    width = GLA_HEADS * dv
    step_rows = STEP_CHUNKS * CHUNK
    assert seq % step_rows == 0
    nc = seq // step_rows
    row = lambda b, c: (b * nc + c, 0)
    next_row = lambda b, c: (b * nc + jnp.minimum(c + 1, nc - 1), 0)
    return pl.pallas_call(
        functools.partial(_gla_prompt_kernel, dk=dk, dv=dv),
        grid=(batch, nc),
        in_specs=[pl.BlockSpec((step_rows, pg_cols), row),
                  pl.BlockSpec((step_rows, SMALL_COLS), row),
                  pl.BlockSpec((step_rows, SMALL_COLS), next_row),
                  pl.BlockSpec((None, GLA_GATE_RANK, kw), lambda b, c: (layer, 0, 0)),
                  pl.BlockSpec((None, 1, kw), lambda b, c: (layer, 0, 0)),
                  pl.BlockSpec((None, 1, dv), lambda b, c: (layer, 0, 0))],
        out_specs=[pl.BlockSpec((step_rows, width), row),
                   pl.BlockSpec((1, GLA_HEADS, dk, dv), lambda b, c: (b, 0, 0, 0))],
        out_shape=[jax.ShapeDtypeStruct((m, d_model), BF16),
                   jax.ShapeDtypeStruct((batch, GLA_HEADS, dk, dv), F32)],
        scratch_shapes=[pltpu.VMEM((GLA_HEADS, dv, dk), F32), pltpu.VMEM((step_rows, kw), F32)],
        compiler_params=_cparams(2),
        name="gla_prompt",
    )(pg, ps, ps, w2_3, bg_3, go_3)


MIX_ROWS = 16


def _column_tiles(xs):
    parts = []
    for x in xs:
        head = x.astype(BF16).astype(F32)
        both = jnp.concatenate([head, x - head, jnp.zeros((LANES - 2 * MIX_ROWS, x.shape[1]), F32)], axis=0)
        parts.append(both.T[:, 0:2 * MIX_ROWS].astype(BF16))
    lhs = jnp.concatenate(parts, axis=0)
    sel = jnp.bitwise_and(_iota2((2 * MIX_ROWS, LANES), 0), MIX_ROWS - 1)

    def tiles(s):
        out = _dot(lhs, jnp.where(sel == s, 1.0, 0.0).astype(BF16))
        return [out[j * LANES:(j + 1) * LANES] for j in range(len(xs))]

    return tiles


def _gla_sample_kernel(pg_ref, ps_ref, w2_ref, bg_ref, go_ref, st_ref, o_in_ref, st_in_ref,
                       o_ref, sto_ref, *, dk, dv):
    del o_in_ref, st_in_ref
    kw = GLA_HEADS * dk
    decay = jnp.exp(_gla_log_decay(ps_ref[:, 0:GLA_GATE_RANK], w2_ref, bg_ref))
    g_out = go_ref[...]
    wide = lambda t: jnp.concatenate([t] * (dv // LANES), axis=1)
    for h in range(GLA_HEADS):
        tiles = _column_tiles([pg_ref[:, h * dk:(h + 1) * dk] * (dk ** -0.5),
                               pg_ref[:, kw + h * dk: kw + (h + 1) * dk],
                               decay[:, h * dk:(h + 1) * dk]])
        v = pg_ref[:, 2 * kw + h * dv: 2 * kw + (h + 1) * dv]
        r = pg_ref[:, 2 * kw + GLA_HEADS * dv + h * dv: 2 * kw + GLA_HEADS * dv + (h + 1) * dv]
        rows = []
        for s in range(MIX_ROWS):
            q_t, k_t, a_t = tiles(s)
            st = st_ref[s, h] * wide(a_t) + wide(k_t) * v[s:s + 1, :]
            sto_ref[s, h] = st
            rows.append(jnp.sum(wide(q_t) * st, axis=0, keepdims=True))
        o = jnp.concatenate(rows, axis=0)
        o_ref[:, h * dv:(h + 1) * dv] = (_rms(o, g_out) * _silu(r)).astype(o_ref.dtype)


def gla_sample(pg, ps, w2_3, bg_3, go_3, state5, layer, o_all, st_all, *, row0, n_samples):
    m, pg_cols = pg.shape
    kw = w2_3.shape[2]
    dk = kw // GLA_HEADS
    dv = go_3.shape[2]
    width = GLA_HEADS * dv
    blk0 = row0 // MIX_ROWS
    row = lambda i: (blk0 + i, 0)
    st_spec = pl.BlockSpec((None, MIX_ROWS, GLA_HEADS, dk, dv), lambda i: (layer, i, 0, 0, 0))
    any_spec = pl.BlockSpec(memory_space=pl.ANY)
    in_specs = [pl.BlockSpec((MIX_ROWS, pg_cols), row),
                pl.BlockSpec((MIX_ROWS, SMALL_COLS), row),
                pl.BlockSpec((None, GLA_GATE_RANK, kw), lambda i: (layer, 0, 0)),
                pl.BlockSpec((None, 1, kw), lambda i: (layer, 0, 0)),
                pl.BlockSpec((None, 1, dv), lambda i: (layer, 0, 0)),
                st_spec, any_spec]
    args = [pg, ps, w2_3, bg_3, go_3, state5, o_all]
    aliases = {6: 0}
    kern = functools.partial(_gla_sample_kernel, dk=dk, dv=dv)
    if st_all is not None:
        in_specs.append(any_spec)
        args.append(st_all)
        aliases[7] = 1
    else:
        kern = functools.partial(_no_state_in, kern, 7)
    return pl.pallas_call(
        kern,
        grid=(n_samples // MIX_ROWS,),
        in_specs=in_specs,
        out_specs=[pl.BlockSpec((MIX_ROWS, width), row), st_spec],
        out_shape=[jax.ShapeDtypeStruct(o_all.shape, BF16),
                   jax.ShapeDtypeStruct(state5.shape, F32)],
        input_output_aliases=aliases,
        compiler_params=_cparams(1),
        name="gla_sample",
    )(*args)


def _no_state_in(kern, pos, *refs):
    return kern(*refs[:pos], None, *refs[pos:])


def _gdn_gates(ps, alog_ref, dtb_ref):
    g = -jnp.exp(alog_ref[...]) * _softplus(ps + dtb_ref[...])
    return g, _sigmoid(ps)


def _l2norm(x):
    return x * lax.rsqrt(jnp.sum(x * x, axis=-1, keepdims=True) + EPS)


def _inv_unit_lower(lows):
    n = lows[0].shape[0]
    r = _iota2((n, n), 0)
    c = _iota2((n, n), 1)
    eye = (r == c).astype(F32)
    base = 8
    same = lambda size: jnp.right_shift(r, size.bit_length() - 1) == jnp.right_shift(c, size.bit_length() - 1)
    bf = lambda xs: [x.astype(BF16) for x in xs]
    in_base = same(base)
    l8 = bf([jnp.where(in_base, low, 0.0) for low in lows])
    x = [eye - a.astype(F32) for a in l8]
    p = [_dot(a, a) for a in l8]
    for step in range(2):
        pb = bf(p)
        x = [xi + _dot(xb, pi) for xi, xb, pi in zip(x, bf(x), pb)]
        if step == 0:
            p = [_dot(pi, pi) for pi in pb]
    m = base
    while m < n:
        in_2m, in_m = same(2 * m), same(m)
        off = bf([jnp.where(in_2m, jnp.where(in_m, 0.0, low), 0.0) for low in lows])
        xb = bf(x)
        y = bf([_dot(oi, xi) for oi, xi in zip(off, xb)])
        x = [xi - _dot(xbi, yi) for xi, xbi, yi in zip(x, xb, y)]
        m *= 2
    return x


def _gdn_prompt_kernel(pd_ref, ps_ref, psn_ref, cw_ref, alog_ref, dtb_ref, go_ref, o_in_ref, o_ref, st_ref,
                       xb_ref, s_ref, gsc_ref, *, dk, dv):
    del o_in_ref
    c = pl.program_id(1)
    C = CHUNK
    R = STEP_CHUNKS * C
    kw = GDN_HEADS * dk
    conv_dim = 2 * kw + GDN_HEADS * dv
    halo = 8

    def chunk_log_decay(small_ref):
        g_all, _ = _gdn_gates(small_ref[...], alog_ref, dtb_ref)
        tri = (_iota2((C, C), 0) >= _iota2((C, C), 1)).astype(F32)
        g_all = g_all * LOG2E
        return jnp.concatenate([_dot_exact(tri, g_all[cc * C:(cc + 1) * C]) for cc in range(STEP_CHUNKS)], axis=0)

    @pl.when(c == 0)
    def _():
        xb_ref[0:halo, :] = jnp.zeros((halo, conv_dim), F32)
        s_ref[...] = jnp.zeros_like(s_ref)
        gsc_ref[...] = chunk_log_decay(ps_ref)

    xb_ref[halo:halo + R, :] = pd_ref[:, 0:conv_dim]
    window = xb_ref[...]
    conv = window[halo:halo + R] * cw_ref[CONV_W - 1:CONV_W, :]
    for back in range(1, CONV_W):
        tap = pltpu.roll(window, back, axis=0)[halo:halo + R]
        conv = conv + tap * cw_ref[CONV_W - 1 - back:CONV_W - back, :]
    conv = _silu(conv)
    xb_ref[0:halo, :] = window[R:R + halo]

    beta_all = _sigmoid(ps_ref[...])
    gc_all = gsc_ref[...]
    gr_all = [gc_all[cc * C:(cc + 1) * C].T for cc in range(STEP_CHUNKS)]
    r_i = _iota2((C, C), 0)
    c_i = _iota2((C, C), 1)
    incl = r_i >= c_i
    strict = r_i > c_i
    g_out = go_ref[...]

    units = [(cc, h) for cc in range(STEP_CHUNKS) for h in range(GDN_HEADS)]
    ids = range(len(units))
    rows = lambda cc: slice(cc * C, (cc + 1) * C)
    col = lambda base, width, h: slice(base + h * width, base + (h + 1) * width)
    bf = lambda xs: [x.astype(BF16) for x in xs]
    q = [_l2norm(conv[rows(cc), col(0, dk, h)]) * (dk ** -0.5) for cc, h in units]
    k = [_l2norm(conv[rows(cc), col(kw, dk, h)]) for cc, h in units]
    v = [conv[rows(cc), col(2 * kw, dv, h)] for cc, h in units]
    beta = [beta_all[rows(cc), DB_LANE + h: DB_LANE + h + 1] for cc, h in units]
    gc = [gc_all[rows(cc), DA_LANE + h: DA_LANE + h + 1] for cc, h in units]
    g_last = [g[C - 1:C, :] for g in gc]
    decay = [jnp.exp2(jnp.where(incl, gc[u] - gr_all[cc][DA_LANE + h: DA_LANE + h + 1, :], MASKED_LOG))
             for u, (cc, h) in enumerate(units)]
    kb = [k[u] * beta[u] for u in ids]
    kbf, qbf = bf(k), bf(q)
    kk = [_dot_nt(a, b) for a, b in zip(bf(kb), kbf)]
    qk = [_dot_nt(a, b) for a, b in zip(qbf, kbf)]
    t_inv = bf(_inv_unit_lower([jnp.where(strict, kk[u] * decay[u], 0.0) for u in ids]))
    rhs = bf([jnp.concatenate([v[u] * beta[u], kb[u] * jnp.exp2(gc[u])], axis=-1) for u in ids])
    sol = [_dot(t, x) for t, x in zip(t_inv, rhs)]
    wbf = bf([sol[u][:, dv:dv + dk] for u in ids])
    q_dec = bf([q[u] * jnp.exp2(gc[u]) for u in ids])
    k_dec = bf([k[u] * jnp.exp2(g_last[u] - gc[u]) for u in ids])
    qkd = bf([qk[u] * decay[u] for u in ids])

    s = [s_ref[h] for h in range(GDN_HEADS)]
    o = [None] * len(units)
    for cc in range(STEP_CHUNKS):
        us = [cc * GDN_HEADS + h for h in range(GDN_HEADS)]
        sb = bf(s)
        ws = [_dot(wbf[u], sb[h]) for h, u in enumerate(us)]
        oc = [_dot(q_dec[u], sb[h]) for h, u in enumerate(us)]
        uu = bf([sol[u][:, 0:dv] - ws[h] for h, u in enumerate(us)])
        for h, u in enumerate(us):
            o[u] = oc[h] + _dot(qkd[u], uu[h])
        s_new = [_dot_tn(k_dec[u], uu[h]) for h, u in enumerate(us)]
        s = [jnp.exp2(g_last[u]) * s[h] + s_new[h] for h, u in enumerate(us)]
    for h in range(GDN_HEADS):
        s_ref[h] = s[h]
    for u, (cc, h) in enumerate(units):
        z = pd_ref[rows(cc), col(conv_dim, dv, h)]
        o_ref[rows(cc), col(0, dv, h)] = (_rms(o[u], g_out) * _silu(z)).astype(o_ref.dtype)

    gsc_ref[...] = chunk_log_decay(psn_ref)

    @pl.when(c == pl.num_programs(1) - 1)
    def _():
        st_ref[0] = s_ref[...]


def gdn_prompt(pd, ps, cw_3, alog_3, dtb_3, go_3, layer, o_all, *, batch, seq, col0):
    m, pd_cols = pd.shape
    dv = go_3.shape[2]
    dk = dv
    width = GDN_HEADS * dv
    conv_dim = cw_3.shape[2]
    step_rows = STEP_CHUNKS * CHUNK
    assert seq % step_rows == 0
    nc = seq // step_rows
    row = lambda b, c: (b * nc + c, 0)
    cb = col0 // width
    lane_spec = pl.BlockSpec((None, 1, SMALL_COLS), lambda b, c: (layer, 0, 0))
    return pl.pallas_call(
        functools.partial(_gdn_prompt_kernel, dk=dk, dv=dv),
        grid=(batch, nc),
        in_specs=[pl.BlockSpec((step_rows, pd_cols), row),
                  pl.BlockSpec((step_rows, SMALL_COLS), row),
                  pl.BlockSpec((step_rows, SMALL_COLS), lambda b, c: (b * nc + jnp.minimum(c + 1, nc - 1), 0)),
                  pl.BlockSpec((None, CONV_W, conv_dim), lambda b, c: (layer, 0, 0)),
                  lane_spec, lane_spec,
                  pl.BlockSpec((None, 1, dv), lambda b, c: (layer, 0, 0)),
                  pl.BlockSpec(memory_space=pl.ANY)],
        out_specs=[pl.BlockSpec((step_rows, width), lambda b, c: (b * nc + c, cb)),
                   pl.BlockSpec((1, GDN_HEADS, dk, dv), lambda b, c: (b, 0, 0, 0))],
        out_shape=[jax.ShapeDtypeStruct(o_all.shape, BF16),
                   jax.ShapeDtypeStruct((batch, GDN_HEADS, dk, dv), F32)],
        scratch_shapes=[pltpu.VMEM((step_rows + 8, conv_dim), F32),
                        pltpu.VMEM((GDN_HEADS, dk, dv), F32),
                        pltpu.VMEM((step_rows, SMALL_COLS), F32)],
        input_output_aliases={7: 0},
        compiler_params=_cparams(2),
        name="gdn_prompt",
    )(pd, ps, ps, cw_3, alog_3, dtb_3, go_3, o_all)


def _gdn_sample_kernel(pd_ref, ps_ref, cw_ref, alog_ref, dtb_ref, go_ref, cs_ref, st_ref,
                       o_in_ref, cs_in_ref, st_in_ref, o_ref, cso_ref, sto_ref, *, dk, dv):
    del o_in_ref, cs_in_ref, st_in_ref
    kw = GDN_HEADS * dk
    conv_dim = 2 * kw + GDN_HEADS * dv
    new = pd_ref[:, 0:conv_dim]
    conv = new * cw_ref[CONV_W - 1:CONV_W, :]
    for j in range(CONV_W - 1):
        conv = conv + cs_ref[j] * cw_ref[j:j + 1, :]
    conv = _silu(conv)
    for j in range(CONV_W - 2):
        cso_ref[j] = cs_ref[j + 1]
    cso_ref[CONV_W - 2] = new

    g_all, beta_all = _gdn_gates(ps_ref[...], alog_ref, dtb_ref)
    a_all = jnp.exp(g_all)
    g_out = go_ref[...]
    for h in range(GDN_HEADS):
        tiles = _column_tiles([_l2norm(conv[:, h * dk:(h + 1) * dk]) * (dk ** -0.5),
                               _l2norm(conv[:, kw + h * dk: kw + (h + 1) * dk])])
        v = conv[:, 2 * kw + h * dv: 2 * kw + (h + 1) * dv]
        z = pd_ref[:, conv_dim + h * dv: conv_dim + (h + 1) * dv]
        rows = []
        for s in range(MIX_ROWS):
            a = a_all[s:s + 1, DA_LANE + h: DA_LANE + h + 1]
            beta = beta_all[s:s + 1, DB_LANE + h: DB_LANE + h + 1]
            q_t, k_t = tiles(s)
            st = st_ref[s, h] * a
            u = beta * (v[s:s + 1, :] - jnp.sum(k_t * st, axis=0, keepdims=True))
            st = st + k_t * u
            sto_ref[s, h] = st
            rows.append(jnp.sum(q_t * st, axis=0, keepdims=True))
        o = jnp.concatenate(rows, axis=0)
        o_ref[:, h * dv:(h + 1) * dv] = (_rms(o, g_out) * _silu(z)).astype(o_ref.dtype)


def gdn_sample(pd, ps, cw_3, alog_3, dtb_3, go_3, conv3, state5, layer, o_all, cs_all, st_all,
               *, row0, n_samples, col0):
    m, pd_cols = pd.shape
    dv = go_3.shape[2]
    dk = dv
    width = GDN_HEADS * dv
    conv_dim = cw_3.shape[2]
    blk0 = row0 // MIX_ROWS
    row = lambda i: (blk0 + i, 0)
    cb = col0 // width
    lane_spec = pl.BlockSpec((None, 1, SMALL_COLS), lambda i: (layer, 0, 0))
    cs_spec = pl.BlockSpec((None, CONV_W - 1, MIX_ROWS, conv_dim), lambda i: (layer, 0, i, 0))
    st_spec = pl.BlockSpec((None, MIX_ROWS, GDN_HEADS, dk, dv), lambda i: (layer, i, 0, 0, 0))
    any_spec = pl.BlockSpec(memory_space=pl.ANY)
    in_specs = [pl.BlockSpec((MIX_ROWS, pd_cols), row),
                pl.BlockSpec((MIX_ROWS, SMALL_COLS), row),
                pl.BlockSpec((None, CONV_W, conv_dim), lambda i: (layer, 0, 0)),
                lane_spec, lane_spec,
                pl.BlockSpec((None, 1, dv), lambda i: (layer, 0, 0)),
                cs_spec, st_spec, any_spec]
    args = [pd, ps, cw_3, alog_3, dtb_3, go_3, conv3, state5, o_all]
    aliases = {8: 0}
    kern = functools.partial(_gdn_sample_kernel, dk=dk, dv=dv)
    if st_all is not None:
        in_specs += [any_spec, any_spec]
        args += [cs_all, st_all]
        aliases[9] = 1
        aliases[10] = 2
    else:
        kern = functools.partial(_no_state_in, functools.partial(_no_state_in, kern, 9), 9)
    return pl.pallas_call(
        kern,
        grid=(n_samples // MIX_ROWS,),
        in_specs=in_specs,
        out_specs=[pl.BlockSpec((MIX_ROWS, width), lambda i: (blk0 + i, cb)), cs_spec, st_spec],
        out_shape=[jax.ShapeDtypeStruct(o_all.shape, BF16),
                   jax.ShapeDtypeStruct(conv3.shape, F32),
                   jax.ShapeDtypeStruct(state5.shape, F32)],
        input_output_aliases=aliases,
        compiler_params=_cparams(1),
        name="gdn_sample",
    )(*args)


def _row_tile(m, cap):
    best = 16
    for t in range(16, cap + 1, 16):
        if m % t == 0:
            best = t
    return best


def kernel(x_prompt, x_sample, mem_prompt, cache_mem_k, cache_mem_v, state_gla, state_gdn, state_conv, g_mix, w_in, gla_w_gate2, gla_b_gate, gla_g_out, gdn_conv_w, gdn_a_log, gdn_dt_bias, gdn_g_out, w_out, g_xattn, g_mem, xa_w_q, xa_w_k, xa_w_v, xa_w_o, g_ffn, ffn_w_gate, ffn_w_up, ffn_w_down, g_final):
    batch, seq, d = x_prompt.shape
    n_s = x_sample.shape[0]
    depth = w_in.shape[0]
    n_mem = mem_prompt.shape[1]
    mp = batch * seq
    m = mp + n_s
    gla_kw = gla_w_gate2.shape[2]
    gla_dv = gla_g_out.shape[1]
    gla_width = GLA_HEADS * gla_dv
    gdn_dv = gdn_g_out.shape[1]
    gdn_width = GDN_HEADS * gdn_dv
    conv_dim = gdn_conv_w.shape[2]
    n_gla = 2 * gla_kw + 2 * gla_width
    c_code = n_gla
    c_conv = c_code + GLA_GATE_RANK
    n_gdn = conv_dim + gdn_width
    c_da = c_conv + n_gdn

    tm = _row_tile(m, 1088)
    tm_mem = _row_tile(batch * n_mem, 1024)

    row3 = lambda p: p.reshape(depth, 1, p.shape[-1])
    lane_row = lambda p: jnp.pad(p, ((0, 0), (DA_LANE, SMALL_COLS - DA_LANE - GDN_HEADS))).reshape(depth, 1, SMALL_COLS)
    g_mix3, g_mem3 = row3(g_mix), row3(g_mem)
    g_xattn_col, g_ffn_col = g_xattn.reshape(depth, d, 1), g_ffn.reshape(depth, d, 1)
    g_final3 = g_final.reshape(1, 1, d)
    bg3, gla_go3, gdn_go3 = row3(gla_b_gate), row3(gla_g_out), row3(gdn_g_out)
    alog3, dtb3 = lane_row(gdn_a_log), lane_row(gdn_dt_bias)

    w_in_t = jnp.swapaxes(w_in, 1, 2)

    conv3 = jnp.swapaxes(state_conv, 1, 2)
    mem2 = mem_prompt.reshape(batch * n_mem, d)
    cache_k4 = cache_token_rows(cache_mem_k)
    cache_v4 = cache_token_rows(cache_mem_v)

    filled = norm_inputs(x_prompt.reshape(mp, d), m, 0, _row_tile(mp, 1024))
    x, xb, ss = norm_inputs(x_sample.reshape(n_s, d), m, mp, n_s, filled)

    p_gla, p_gdn, p_conv = [], [], []
    s_gla_all = s_gdn_all = s_conv_all = mk_all = mv_all = None
    for l in range(depth):
        mix = (ss, g_mix3)
        pg = matmul(xb, w_in_t, l, tm=tm, tn=1024, col0=0, n_cols=n_gla, transposed=True, norm=mix, name="proj_gla")
        pd = matmul(xb, w_in_t, l, tm=tm, tn=1024, col0=c_conv, n_cols=n_gdn, transposed=True, norm=mix,
                    name="proj_gdn")
        ps = proj_small(xb, w_in_t, l, tm=tm, row_a=c_code, row_b=c_da, norm=mix)

        o, sa = gla_prompt(pg, ps, gla_w_gate2, bg3, gla_go3, l, batch=batch, seq=seq, d_model=d)
        o, sb = gdn_prompt(pd, ps, gdn_conv_w, alog3, dtb3, gdn_go3, l, o, batch=batch, seq=seq, col0=gla_width)
        o, s_gla_all = gla_sample(pg, ps, gla_w_gate2, bg3, gla_go3, state_gla, l, o, s_gla_all,
                                  row0=mp, n_samples=n_s)
        o, s_conv_all, s_gdn_all = gdn_sample(pd, ps, gdn_conv_w, alog3, dtb3, gdn_go3, conv3, state_gdn, l,
                                              o, s_conv_all, s_gdn_all, row0=mp, n_samples=n_s, col0=gla_width)
        p_gla.append(sa)
        p_gdn.append(sb)
        p_conv.append(jnp.stack([pd[(b + 1) * seq - (CONV_W - 1):(b + 1) * seq, :conv_dim] for b in range(batch)]))
        x, xb, ss = matmul(o, w_out, l, tm=tm, tn=1024, res=x, emit_norm_inputs=True, name="w_out")

        mem_n = rmsnorm_rows(mem2, g_mem3, l, BF16, tm_mem)
        mk, mk_all = memory_proj(mem_n, xa_w_k, l, mk_all, tn=1024, name="mem_k")
        mv, mv_all = memory_proj(mem_n, xa_w_v, l, mv_all, tn=1024, name="mem_v")
        qx = matmul(xb, xa_w_q, l, tm=tm, tn=1024, out_dtype=BF16, norm=(ss, g_xattn_col), name="xa_q")
        ox = xattn_prompt(qx, mk, mv, batch=batch, seq=seq, n_mem=n_mem, tq=min(seq, 1024))
        ox = xattn_sample(qx, cache_k4, cache_v4, l, ox, row0=mp, n_samples=n_s)
        x, xb, ss = matmul(ox, xa_w_o, l, tm=tm, tn=1024, res=x, emit_norm_inputs=True, name="xa_o")

        ff, w_down = swiglu(xb, ffn_w_gate, ffn_w_up, ffn_w_down, l, tm=tm, tn=512, norm=(ss, g_ffn_col))
        w_down = w_down[None]
        if l + 1 < depth:
            x, xb, ss = matmul(ff, w_down, 0, tm=tm, tn=512, res=x, emit_norm_inputs=True, name="ffn_down")
        else:
            x = matmul(ff, w_down, 0, tm=tm, tn=512, res=x, name="ffn_down")

    y_prompt = rmsnorm_rows(x, g_final3, 0, F32, _row_tile(mp, 1024), row0=0, n_rows=mp)
    y_sample = rmsnorm_rows(x, g_final3, 0, F32, n_s, row0=mp, n_rows=n_s)
    hd = d // XA_HEADS

    def from_token_rows(rows_all):
        c6 = rows_all.reshape(depth, batch, n_mem, hd // LANES, XA_HEADS, LANES)
        return jnp.transpose(c6, (0, 1, 2, 4, 3, 5)).reshape(depth, batch, n_mem, XA_HEADS, hd)

    return (y_prompt.reshape(batch, seq, d),
            y_sample.reshape(n_s, 1, d),
            jnp.stack(p_gla),
            jnp.stack(p_gdn),
            jnp.stack(p_conv),
            from_token_rows(mk_all),
            from_token_rows(mv_all),
            s_gla_all,
            s_gdn_all,
            jnp.swapaxes(s_conv_all, 1, 2))
```

```python
import functools

import jax
import jax.numpy as jnp
from jax import lax
from jax.experimental import pallas as pl
from jax.experimental.pallas import tpu as pltpu

F32 = jnp.float32
BF16 = jnp.bfloat16

EPS = 1e-6
GLA_HEADS = 4
GLA_GATE_RANK = 16
GLA_GATE_TEMP = 16.0
GDN_HEADS = 8
CONV_W = 4
XA_HEADS = 4

V7X_VMEM_LIMIT_BYTES = 56 * 1024 * 1024
LANES = 128
CHUNK = 128
STEP_CHUNKS = 2
SUB = 8
LOG2E = 1.4426950408889634
MASKED_LOG = -1e30
SMALL_COLS = LANES
DA_LANE = 16
DB_LANE = 24


def _cparams(n_axes):
    return pltpu.CompilerParams(
        dimension_semantics=("arbitrary",) * n_axes,
        vmem_limit_bytes=V7X_VMEM_LIMIT_BYTES)


def _dot(a, b):
    return jnp.dot(a, b, preferred_element_type=F32)


def _dot_nt(a, b):
    return lax.dot_general(a, b, (((1,), (1,)), ((), ())), preferred_element_type=F32)


def _dot_tn(a, b):
    return lax.dot_general(a, b, (((0,), (0,)), ((), ())), preferred_element_type=F32)


def _dot_exact(a, b):
    return jnp.dot(a, b, preferred_element_type=F32, precision=lax.Precision.HIGHEST)


def _silu(x):
    return x * (1.0 / (1.0 + jnp.exp(-x)))


def _sigmoid(x):
    return 1.0 / (1.0 + jnp.exp(-x))


def _softplus(x):
    return jnp.maximum(x, 0.0) + jnp.log1p(jnp.exp(-jnp.abs(x)))


def _log_sigmoid(x):
    return jnp.minimum(x, 0.0) - jnp.log1p(jnp.exp(-jnp.abs(x)))


def _rms(x, g):
    return x * lax.rsqrt(jnp.mean(x * x, axis=-1, keepdims=True) + EPS) * g


def _iota2(shape, axis):
    return lax.broadcasted_iota(jnp.int32, shape, axis)


def _rmsnorm_kernel(x_ref, g_ref, o_ref):
    o_ref[...] = _rms(x_ref[...], g_ref[...]).astype(o_ref.dtype)


def rmsnorm_rows(x, g3, layer, out_dtype, tm, row0=0, n_rows=None):
    m, d = x.shape
    n_rows = m if n_rows is None else n_rows
    blk0 = row0 // tm
    return pl.pallas_call(
        _rmsnorm_kernel,
        grid=(n_rows // tm,),
        in_specs=[pl.BlockSpec((tm, d), lambda i: (blk0 + i, 0)),
                  pl.BlockSpec((None, 1, d), lambda i: (layer, 0, 0))],
        out_specs=pl.BlockSpec((tm, d), lambda i: (i, 0)),
        out_shape=jax.ShapeDtypeStruct((n_rows, d), out_dtype),
        compiler_params=_cparams(1),
        name="rmsnorm",
    )(x, g3)


CAST_ROWS = 256


def _stage_weight(w_ref, wb_ref, gain_ref=None, gain_is_row=False):
    for r in range(0, w_ref.shape[0], CAST_ROWS):
        blk = w_ref[r:r + CAST_ROWS, :]
        if gain_ref is not None:
            blk = blk * (gain_ref[...] if gain_is_row else gain_ref[r:r + CAST_ROWS, :])
        wb_ref[r:r + CAST_ROWS, :] = blk.astype(BF16)


def _row_rsqrt(ss_ref, d):
    tot = ss_ref[:, 0:1]
    for j in range(1, ss_ref.shape[1] // LANES):
        tot = tot + ss_ref[:, j * LANES:j * LANES + 1]
    return lax.rsqrt(tot * (1.0 / d) + EPS)


def _sum_squares(x):
    return jnp.broadcast_to(jnp.sum(x * x, axis=-1, keepdims=True), (x.shape[0], LANES))


def _mm_kernel(*refs, has_res, stage, transposed, has_norm, emit):
    it = iter(refs)
    a_ref, w_ref = next(it), next(it)
    if transposed:
        w_ref = w_ref.at[0]
    ss_ref, gain_ref = (next(it), next(it)) if has_norm else (None, None)
    res_ref = next(it) if has_res else None
    o_ref = next(it)
    ob_ref, sso_ref = (next(it), next(it)) if emit else (None, None)
    if stage:
        wb_ref = next(it)

        @pl.when(pl.program_id(1) == 0)
        def _():
            _stage_weight(w_ref, wb_ref, gain_ref, gain_is_row=transposed)

        w = wb_ref[...]
    else:
        w = w_ref[...]
    acc = _dot_nt(a_ref[...], w) if transposed else _dot(a_ref[...], w)
    if has_norm:
        acc = acc * _row_rsqrt(ss_ref, a_ref.shape[1])
    if has_res:
        acc = res_ref[...] + acc
    o_ref[...] = acc.astype(o_ref.dtype)
    if emit:
        ob_ref[...] = acc.astype(BF16)
        sso_ref[...] = _sum_squares(acc)


def matmul(a, w3, layer, *, tm, tn, col0=0, n_cols=None, res=None, out_dtype=F32, transposed=False,
           norm=None, emit_norm_inputs=False, name="matmul"):
    m, k = a.shape
    n_total = w3.shape[1] if transposed else w3.shape[2]
    n_cols = n_total if n_cols is None else n_cols
    n_tiles = n_cols // tn
    assert m % tm == 0 and n_cols % tn == 0
    stage = w3.dtype != BF16
    assert not stage or (tn if transposed else k) % CAST_ROWS == 0
    assert norm is None or stage
    if transposed:
        assert col0 % 8 == 0
        w_spec = pl.BlockSpec((pl.Element(1), pl.Element(tn), pl.Element(k)),
                              (lambda j, i: (layer, pl.multiple_of(col0 + j * tn, 8), 0)))
        wb_shape = (tn, k)
    else:
        assert col0 % tn == 0
        cb0 = col0 // tn
        w_spec = pl.BlockSpec((None, k, tn), (lambda j, i: (layer, 0, cb0 + j)))
        wb_shape = (k, tn)
    in_specs = [pl.BlockSpec((tm, k), (lambda j, i: (i, 0))), w_spec]
    args = [a, w3]
    if norm is not None:
        ss, gain3 = norm
        in_specs.append(pl.BlockSpec((tm, ss.shape[1]), (lambda j, i: (i, 0))))
        in_specs.append(pl.BlockSpec((None,) + gain3.shape[1:], (lambda j, i: (layer, 0, 0))))
        args += [ss, gain3]
    if res is not None:
        in_specs.append(pl.BlockSpec((tm, tn), (lambda j, i: (i, j))))
        args.append(res)
    tile = pl.BlockSpec((tm, tn), (lambda j, i: (i, j)))
    out_specs, out_shape = tile, jax.ShapeDtypeStruct((m, n_cols), out_dtype)
    if emit_norm_inputs:
        out_specs = [tile, tile, pl.BlockSpec((tm, LANES), (lambda j, i: (i, j)))]
        out_shape = [out_shape, jax.ShapeDtypeStruct((m, n_cols), BF16),
                     jax.ShapeDtypeStruct((m, LANES * n_tiles), F32)]
    return pl.pallas_call(
        functools.partial(_mm_kernel, has_res=res is not None, stage=stage, transposed=transposed,
                          has_norm=norm is not None, emit=emit_norm_inputs),
        grid=(n_tiles, m // tm),
        in_specs=in_specs,
        out_specs=out_specs,
        out_shape=out_shape,
        scratch_shapes=[pltpu.VMEM(wb_shape, BF16)] if stage else [],
        compiler_params=_cparams(2),
        name=name,
    )(*args)


def _norm_inputs_kernel(x_ref, *refs):
    xo_ref, xb_ref, ss_ref = refs[-3:]
    x = x_ref[...]
    xo_ref[...] = x
    xb_ref[...] = x.astype(BF16)
    ss_ref[...] = _sum_squares(x)


def norm_inputs(x_part, m, row0, tm, filled=None):
    n, d = x_part.shape
    blk0 = row0 // tm
    row = lambda i: (blk0 + i, 0)
    in_specs = [pl.BlockSpec((tm, d), lambda i: (i, 0))]
    args = [x_part]
    aliases = {}
    if filled is not None:
        in_specs += [pl.BlockSpec(memory_space=pl.ANY)] * 3
        args += list(filled)
        aliases = {1: 0, 2: 1, 3: 2}
    return pl.pallas_call(
        _norm_inputs_kernel,
        grid=(n // tm,),
        in_specs=in_specs,
        out_specs=[pl.BlockSpec((tm, d), row), pl.BlockSpec((tm, d), row), pl.BlockSpec((tm, LANES), row)],
        out_shape=[jax.ShapeDtypeStruct((m, d), F32), jax.ShapeDtypeStruct((m, d), BF16),
                   jax.ShapeDtypeStruct((m, LANES), F32)],
        input_output_aliases=aliases,
        compiler_params=_cparams(1),
        name="norm_inputs",
    )(*args)


def _memory_proj_kernel(*refs, per_tok, n_lane_tiles, n_tiles):
    a_ref, w_ref, ss_ref, gain_ref = refs[:4]
    nat_ref, tok_ref, wb_ref = refs[-3], refs[-2], refs[-1]
    _stage_weight(w_ref, wb_ref, gain_ref)
    acc = _dot(a_ref[...], wb_ref[...]) * _row_rsqrt(ss_ref, a_ref.shape[1])
    nat_ref[...] = acc
    rows, tn = acc.shape
    blocks = tn // LANES
    heads = per_tok // n_lane_tiles
    for j in range(n_tiles):
        @pl.when(pl.program_id(0) == j)
        def _():
            for c in range(blocks):
                h, jj = divmod(j * blocks + c, n_lane_tiles)
                tok_ref[pl.ds(jj * heads + h, rows, stride=per_tok), :] = acc[:, c * LANES:(c + 1) * LANES]


def memory_proj(a, w3, layer, stacked, *, tn, norm, name):
    t, k = a.shape
    depth, _, n = w3.shape
    per_tok = n // LANES
    n_lane_tiles = n // XA_HEADS // LANES
    assert n % tn == 0 and k % CAST_ROWS == 0 and t % 8 == 0
    ss, gain3 = norm
    in_specs = [pl.BlockSpec((t, k), lambda j: (0, 0)),
                pl.BlockSpec((None, k, tn), lambda j: (layer, 0, j)),
                pl.BlockSpec((t, ss.shape[1]), lambda j: (0, 0)),
                pl.BlockSpec((None, k, 1), lambda j: (layer, 0, 0))]
    args = [a, w3, ss, gain3]
    aliases = {}
    if stacked is not None:
        in_specs.append(pl.BlockSpec(memory_space=pl.ANY))
        args.append(stacked)
        aliases = {4: 1}
    return pl.pallas_call(
        functools.partial(_memory_proj_kernel, per_tok=per_tok, n_lane_tiles=n_lane_tiles, n_tiles=n // tn),
        grid=(n // tn,),
        in_specs=in_specs,
        out_specs=[pl.BlockSpec((t, tn), lambda j: (0, j)),
                   pl.BlockSpec((t * per_tok, LANES), lambda j: (layer, 0))],
        out_shape=[jax.ShapeDtypeStruct((t, n), F32),
                   jax.ShapeDtypeStruct((depth * t * per_tok, LANES), F32)],
        scratch_shapes=[pltpu.VMEM((k, tn), BF16)],
        input_output_aliases=aliases,
        compiler_params=_cparams(1),
        name=name,
    )(*args)


SMALL_PIECE = 16


def _proj_small_kernel(a_ref, wa_ref, wb_ref, ss_ref, gain_ref, o_ref, w_ref):
    @pl.when(pl.program_id(0) == 0)
    def _():
        w_ref[...] = jnp.zeros_like(w_ref)
        w_ref[0:SMALL_PIECE, :] = (wa_ref[...] * gain_ref[...]).astype(BF16)
        w_ref[SMALL_PIECE:2 * SMALL_PIECE, :] = (wb_ref[...] * gain_ref[...]).astype(BF16)

    o_ref[...] = _dot_nt(a_ref[...], w_ref[...]) * _row_rsqrt(ss_ref, a_ref.shape[1])


def proj_small(a, wt3, layer, *, tm, row_a, row_b, norm):
    m, k = a.shape
    ss, gain3 = norm
    assert row_a % SMALL_PIECE == 0 and row_b % SMALL_PIECE == 0
    piece = lambda r: pl.BlockSpec((None, SMALL_PIECE, k), lambda i: (layer, r // SMALL_PIECE, 0))
    return pl.pallas_call(
        _proj_small_kernel,
        grid=(m // tm,),
        in_specs=[pl.BlockSpec((tm, k), lambda i: (i, 0)), piece(row_a), piece(row_b),
                  pl.BlockSpec((tm, ss.shape[1]), lambda i: (i, 0)),
                  pl.BlockSpec((None, 1, k), lambda i: (layer, 0, 0))],
        out_specs=pl.BlockSpec((tm, SMALL_COLS), lambda i: (i, 0)),
        out_shape=jax.ShapeDtypeStruct((m, SMALL_COLS), F32),
        scratch_shapes=[pltpu.VMEM((SMALL_COLS, k), BF16)],
        compiler_params=_cparams(1),
        name="proj_small",
    )(a, wt3, wt3, ss, gain3)


def _swiglu_kernel(a_ref, wg_ref, wu_ref, ss_ref, gain_ref, wd_ref, o_ref, wdb_ref, wgb_ref, wub_ref):
    @pl.when(pl.program_id(1) == 0)
    def _():
        _stage_weight(wg_ref, wgb_ref, gain_ref)
        _stage_weight(wu_ref, wub_ref, gain_ref)

    a = a_ref[...]
    r = _row_rsqrt(ss_ref, a_ref.shape[1])
    gate = _dot(a, wgb_ref[...]) * r
    up = _dot(a, wub_ref[...]) * r
    o_ref[...] = (_silu(gate) * up).astype(o_ref.dtype)
    wdb_ref[...] = wd_ref[...].astype(BF16)


def swiglu(a, wg3, wu3, wd3, layer, *, tm, tn, norm):
    m, k = a.shape
    f = wg3.shape[2]
    d_out = wd3.shape[2]
    ss, gain3 = norm
    n_j, n_i = f // tn, m // tm
    assert m % tm == 0 and f % tn == 0 and k % CAST_ROWS == 0 and f % (n_j * n_i * 16) == 0
    wd_rows = f // (n_j * n_i)
    wspec = pl.BlockSpec((None, k, tn), lambda j, i: (layer, 0, j))
    return pl.pallas_call(
        _swiglu_kernel,
        grid=(n_j, n_i),
        in_specs=[pl.BlockSpec((tm, k), lambda j, i: (i, 0)), wspec, wspec,
                  pl.BlockSpec((tm, ss.shape[1]), lambda j, i: (i, 0)),
                  pl.BlockSpec((None, k, 1), lambda j, i: (layer, 0, 0)),
                  pl.BlockSpec((None, wd_rows, d_out), lambda j, i: (layer, j * n_i + i, 0))],
        out_specs=[pl.BlockSpec((tm, tn), lambda j, i: (i, j)),
                   pl.BlockSpec((wd_rows, d_out), lambda j, i: (j * n_i + i, 0))],
        out_shape=[jax.ShapeDtypeStruct((m, f), BF16), jax.ShapeDtypeStruct((f, d_out), BF16)],
        scratch_shapes=[pltpu.VMEM((k, tn), BF16), pltpu.VMEM((k, tn), BF16)],
        compiler_params=_cparams(2),
        name="swiglu",
    )(a, wg3, wu3, ss, gain3, wd3)


def _xattn_prompt_kernel(q_ref, k_ref, v_ref, o_ref, *, scale, hd):
    heads = [slice(h * hd, (h + 1) * hd) for h in range(XA_HEADS)]
    s = [_dot_nt(q_ref[:, c], k_ref[:, c].astype(BF16)) * scale for c in heads]
    e = [jnp.exp(x - jnp.max(x, axis=-1, keepdims=True)) for x in s]
    p = [(x / jnp.sum(x, axis=-1, keepdims=True)).astype(BF16) for x in e]
    for x, c in zip(p, heads):
        o_ref[:, c] = _dot(x, v_ref[:, c].astype(BF16)).astype(o_ref.dtype)


def xattn_prompt(qx, mk, mv, *, batch, seq, n_mem, tq):
    m, d = qx.shape
    nt = seq // tq
    qspec = pl.BlockSpec((tq, d), lambda b, t: (b * nt + t, 0))
    kvspec = pl.BlockSpec((n_mem, d), lambda b, t: (b, 0))
    return pl.pallas_call(
        functools.partial(_xattn_prompt_kernel, scale=(d // XA_HEADS) ** -0.5, hd=d // XA_HEADS),
        grid=(batch, nt),
        in_specs=[qspec, kvspec, kvspec],
        out_specs=qspec,
        out_shape=jax.ShapeDtypeStruct((m, d), BF16),
        compiler_params=_cparams(2),
        name="xattn_prompt",
    )(qx, mk, mv)


XS_ROWS = 16
XS_STEP = 4


def _xattn_sample_kernel(q_ref, k_ref, v_ref, o_in_ref, o_ref, qf_ref, of_ref, *, scale, hd):
    del o_in_ref
    j = pl.program_id(1)

    @pl.when(j == 0)
    def _():
        qf_ref[...] = q_ref[...].astype(F32)

    nj = hd // LANES
    per_tok = nj * XA_HEADS
    half = per_tok // 2
    n_mem = k_ref.shape[1] // per_tok
    lanes = lambda h, jj: slice(h * hd + jj * LANES, h * hd + (jj + 1) * LANES)
    for t in range(XS_STEP):
        r = j * XS_STEP + t
        q_row = qf_ref[pl.ds(r, 1), :]
        q_tok = jnp.concatenate([q_row[:, lanes(h, jj)] for jj in range(nj) for h in range(XA_HEADS)], axis=0)
        prod = k_ref[t].reshape(n_mem, per_tok, LANES) * q_tok[None]
        part = prod[:, 0:half, :] + prod[:, half:per_tok, :]
        part = part + pltpu.roll(part, XA_HEADS, axis=1)
        s = jnp.sum(part, axis=-1, keepdims=True) * scale
        e = jnp.exp(s - jnp.max(s, axis=0, keepdims=True))
        p = e / jnp.sum(e, axis=0, keepdims=True)
        v3 = v_ref[t].reshape(n_mem, per_tok, LANES)
        acc = [jnp.sum(v3[:, g * half:(g + 1) * half, :] * p, axis=0) for g in range(2)]
        pieces = []
        for h in range(XA_HEADS):
            for jj in range(nj):
                row = (jj % 2) * XA_HEADS + h
                pieces.append(acc[jj // 2][row:row + 1, :])
        of_ref[pl.ds(r, 1), :] = jnp.concatenate(pieces, axis=1)

    @pl.when(j == pl.num_programs(1) - 1)
    def _():
        o_ref[...] = of_ref[...].astype(o_ref.dtype)


def cache_token_rows(cache):
    depth, s, n_mem, heads, hd = cache.shape
    nj = hd // LANES
    c6 = cache.reshape(depth, s, n_mem, heads, nj, LANES)
    return jnp.transpose(c6, (0, 1, 2, 4, 3, 5)).reshape(depth, s, n_mem * nj * heads, LANES)


def xattn_sample(qx, cache_k4, cache_v4, layer, o_all, *, row0, n_samples):
    m, d = qx.shape
    tok_rows = cache_k4.shape[2]
    hd = d // XA_HEADS
    assert (hd // LANES) * XA_HEADS == 16 and XA_HEADS == 4
    blk0 = row0 // XS_ROWS
    inner = XS_ROWS // XS_STEP
    qspec = pl.BlockSpec((XS_ROWS, d), lambda i, j: (blk0 + i, 0))
    cspec = pl.BlockSpec((None, XS_STEP, tok_rows, LANES), lambda i, j: (layer, i * inner + j, 0, 0))
    return pl.pallas_call(
        functools.partial(_xattn_sample_kernel, scale=hd ** -0.5, hd=hd),
        grid=(n_samples // XS_ROWS, inner),
        in_specs=[qspec, cspec, cspec, pl.BlockSpec(memory_space=pl.ANY)],
        out_specs=qspec,
        out_shape=jax.ShapeDtypeStruct((m, d), BF16),
        scratch_shapes=[pltpu.VMEM((XS_ROWS, d), F32), pltpu.VMEM((XS_ROWS, d), F32)],
        input_output_aliases={3: 0},
        compiler_params=_cparams(2),
        name="xattn_sample",
    )(qx, cache_k4, cache_v4, o_all)


def _gla_log_decay(code, w2_ref, bg_ref):
    pre = _dot_exact(code, w2_ref[...]) + bg_ref[...]
    return _log_sigmoid(pre) * (1.0 / GLA_GATE_TEMP)


def _gla_prompt_kernel(pg_ref, ps_ref, psn_ref, w2_ref, bg_ref, go_ref, o_ref, st_ref, stt_ref, bsc_ref, *, dk, dv):
    c = pl.program_id(1)
    C = CHUNK
    kw = GLA_HEADS * dk

    def chunk_log_decay(small_ref):
        la = _gla_log_decay(small_ref[:, 0:GLA_GATE_RANK], w2_ref, bg_ref)
        tri = (_iota2((C, C), 0) >= _iota2((C, C), 1)).astype(F32)
        la = la * LOG2E
        return jnp.concatenate([_dot_exact(tri, la[cc * C:(cc + 1) * C]) for cc in range(STEP_CHUNKS)], axis=0)

    @pl.when(c == 0)
    def _():
        stt_ref[...] = jnp.zeros_like(stt_ref)
        bsc_ref[...] = chunk_log_decay(ps_ref)

    b_all = bsc_ref[...]
    row_s = _iota2((SUB, dk), 0)
    lane_a = _iota2((SUB, C), 1)
    ones_k = jnp.ones((dk, LANES), BF16)
    g_out = go_ref[...]

    units = [(cc, h) for cc in range(STEP_CHUNKS) for h in range(GLA_HEADS)]
    heads = range(len(units))
    rows = lambda cc: slice(cc * C, (cc + 1) * C)
    col = lambda base, width, h: slice(base + h * width, base + (h + 1) * width)
    q = [pg_ref[rows(cc), col(0, dk, h)] * (dk ** -0.5) for cc, h in units]
    k = [pg_ref[rows(cc), col(kw, dk, h)] for cc, h in units]
    vb = [pg_ref[rows(cc), col(2 * kw, dv, h)].astype(BF16) for cc, h in units]
    b = [b_all[rows(cc), col(0, dk, h)] for cc, h in units]
    b_last = [bh[C - 1:C, :] for bh in b]

    q_in = [(q[u] * jnp.exp2(b[u])).astype(BF16) for u in heads]
    upd = [_dot_tn(vb[u], (k[u] * jnp.exp2(b_last[u] - b[u])).astype(BF16)) for u in heads]
    s_t = [stt_ref[h] for h in range(GLA_HEADS)]
    o = [None] * len(units)
    for cc in range(STEP_CHUNKS):
        for h in range(GLA_HEADS):
            u = cc * GLA_HEADS + h
            o[u] = _dot_nt(q_in[u], s_t[h].astype(BF16))
        for h in range(GLA_HEADS):
            u = cc * GLA_HEADS + h
            s_t[h] = s_t[h] * jnp.exp2(b_last[u]) + upd[u]
    for h in range(GLA_HEADS):
        stt_ref[h] = s_t[h]

    n_sub = C // SUB
    earlier = [[None] for _ in heads]
    for i in range(1, n_sub):
        lo = i * SUB
        for h in heads:
            b_ref = b[h][lo - 1:lo, :]
            q_rel = (q[h][lo:lo + SUB] * jnp.exp2(b[h][lo:lo + SUB] - b_ref)).astype(BF16)
            k_rel = jnp.concatenate([k[h][0:lo] * jnp.exp2(b_ref - b[h][0:lo]), jnp.zeros((C - lo, dk), F32)],
                                    axis=0).astype(BF16)
            earlier[h].append(_dot_nt(q_rel, k_rel))
    sums = []
    for h in heads:
        zs = []
        for i in range(n_sub):
            lo = i * SUB
            qi, ki, bi = q[h][lo:lo + SUB], k[h][lo:lo + SUB], b[h][lo:lo + SUB]
            for s in range(SUB):
                dec = jnp.exp2(jnp.where(row_s >= s, bi - bi[s:s + 1, :], MASKED_LOG))
                zs.append(qi * ki[s:s + 1, :] * dec)
        sums.append(_dot(jnp.concatenate(zs, axis=0).astype(BF16), ones_k))
    a_rows = [[] for _ in heads]
    for i in range(n_sub):
        lo = i * SUB
        for h in heads:
            a_i = jnp.zeros((SUB, C), F32) if i == 0 else earlier[h][i]
            for s in range(SUB):
                r0 = (i * SUB + s) * SUB
                a_i = jnp.where(lane_a == lo + s, sums[h][r0:r0 + SUB, 0:C], a_i)
            a_rows[h].append(a_i)
    o = [o[h] + _dot(jnp.concatenate(a_rows[h], axis=0).astype(BF16), vb[h]) for h in heads]

    for u, (cc, h) in enumerate(units):
        r = pg_ref[rows(cc), col(2 * kw + GLA_HEADS * dv, dv, h)]
        o_ref[rows(cc), col(0, dv, h)] = (_rms(o[u], g_out) * _silu(r)).astype(o_ref.dtype)

    bsc_ref[...] = chunk_log_decay(psn_ref)

    @pl.when(c == pl.num_programs(1) - 1)
    def _():
        for h in range(GLA_HEADS):
            st_ref[0, h] = stt_ref[h].T


def gla_prompt(pg, ps, w2_3, bg_3, go_3, layer, *, batch, seq, d_model):
    m, pg_cols = pg.shape
    kw = w2_3.shape[2]
    dk = kw // GLA_HEADS
    dv = go_3.shape[2]
    width = GLA_HEADS * dv
    step_rows = STEP_CHUNKS * CHUNK
    assert seq % step_rows == 0
    nc = seq // step_rows
    row = lambda b, c: (b * nc + c, 0)
    next_row = lambda b, c: (b * nc + jnp.minimum(c + 1, nc - 1), 0)
    return pl.pallas_call(
        functools.partial(_gla_prompt_kernel, dk=dk, dv=dv),
        grid=(batch, nc),
        in_specs=[pl.BlockSpec((step_rows, pg_cols), row),
                  pl.BlockSpec((step_rows, SMALL_COLS), row),
                  pl.BlockSpec((step_rows, SMALL_COLS), next_row),
                  pl.BlockSpec((None, GLA_GATE_RANK, kw), lambda b, c: (layer, 0, 0)),
                  pl.BlockSpec((None, 1, kw), lambda b, c: (layer, 0, 0)),
                  pl.BlockSpec((None, 1, dv), lambda b, c: (layer, 0, 0))],
        out_specs=[pl.BlockSpec((step_rows, width), row),
                   pl.BlockSpec((1, GLA_HEADS, dk, dv), lambda b, c: (b, 0, 0, 0))],
        out_shape=[jax.ShapeDtypeStruct((m, d_model), BF16),
                   jax.ShapeDtypeStruct((batch, GLA_HEADS, dk, dv), F32)],
        scratch_shapes=[pltpu.VMEM((GLA_HEADS, dv, dk), F32), pltpu.VMEM((step_rows, kw), F32)],
        compiler_params=_cparams(2),
        name="gla_prompt",
    )(pg, ps, ps, w2_3, bg_3, go_3)


MIX_ROWS = 16


def _column_tiles(xs):
    parts = []
    for x in xs:
        head = x.astype(BF16).astype(F32)
        both = jnp.concatenate([head, x - head, jnp.zeros((LANES - 2 * MIX_ROWS, x.shape[1]), F32)], axis=0)
        parts.append(both.T[:, 0:2 * MIX_ROWS].astype(BF16))
    lhs = jnp.concatenate(parts, axis=0)
    sel = jnp.bitwise_and(_iota2((2 * MIX_ROWS, LANES), 0), MIX_ROWS - 1)

    def tiles(s):
        out = _dot(lhs, jnp.where(sel == s, 1.0, 0.0).astype(BF16))
        return [out[j * LANES:(j + 1) * LANES] for j in range(len(xs))]

    return tiles


def _gla_sample_kernel(pg_ref, ps_ref, w2_ref, bg_ref, go_ref, st_ref, o_in_ref, st_in_ref,
                       o_ref, sto_ref, *, dk, dv):
    del o_in_ref, st_in_ref
    kw = GLA_HEADS * dk
    decay = jnp.exp(_gla_log_decay(ps_ref[:, 0:GLA_GATE_RANK], w2_ref, bg_ref))
    g_out = go_ref[...]
    wide = lambda t: jnp.concatenate([t] * (dv // LANES), axis=1)
    for h in range(GLA_HEADS):
        tiles = _column_tiles([pg_ref[:, h * dk:(h + 1) * dk] * (dk ** -0.5),
                               pg_ref[:, kw + h * dk: kw + (h + 1) * dk],
                               decay[:, h * dk:(h + 1) * dk]])
        v = pg_ref[:, 2 * kw + h * dv: 2 * kw + (h + 1) * dv]
        r = pg_ref[:, 2 * kw + GLA_HEADS * dv + h * dv: 2 * kw + GLA_HEADS * dv + (h + 1) * dv]
        rows = []
        for s in range(MIX_ROWS):
            q_t, k_t, a_t = tiles(s)
            st = st_ref[s, h] * wide(a_t) + wide(k_t) * v[s:s + 1, :]
            sto_ref[s, h] = st
            rows.append(jnp.sum(wide(q_t) * st, axis=0, keepdims=True))
        o = jnp.concatenate(rows, axis=0)
        o_ref[:, h * dv:(h + 1) * dv] = (_rms(o, g_out) * _silu(r)).astype(o_ref.dtype)


def gla_sample(pg, ps, w2_3, bg_3, go_3, state5, layer, o_all, st_all, *, row0, n_samples):
    m, pg_cols = pg.shape
    kw = w2_3.shape[2]
    dk = kw // GLA_HEADS
    dv = go_3.shape[2]
    width = GLA_HEADS * dv
    blk0 = row0 // MIX_ROWS
    row = lambda i: (blk0 + i, 0)
    st_spec = pl.BlockSpec((None, MIX_ROWS, GLA_HEADS, dk, dv), lambda i: (layer, i, 0, 0, 0))
    any_spec = pl.BlockSpec(memory_space=pl.ANY)
    in_specs = [pl.BlockSpec((MIX_ROWS, pg_cols), row),
                pl.BlockSpec((MIX_ROWS, SMALL_COLS), row),
                pl.BlockSpec((None, GLA_GATE_RANK, kw), lambda i: (layer, 0, 0)),
                pl.BlockSpec((None, 1, kw), lambda i: (layer, 0, 0)),
                pl.BlockSpec((None, 1, dv), lambda i: (layer, 0, 0)),
                st_spec, any_spec]
    args = [pg, ps, w2_3, bg_3, go_3, state5, o_all]
    aliases = {6: 0}
    kern = functools.partial(_gla_sample_kernel, dk=dk, dv=dv)
    if st_all is not None:
        in_specs.append(any_spec)
        args.append(st_all)
        aliases[7] = 1
    else:
        kern = functools.partial(_no_state_in, kern, 7)
    return pl.pallas_call(
        kern,
        grid=(n_samples // MIX_ROWS,),
        in_specs=in_specs,
        out_specs=[pl.BlockSpec((MIX_ROWS, width), row), st_spec],
        out_shape=[jax.ShapeDtypeStruct(o_all.shape, BF16),
                   jax.ShapeDtypeStruct(state5.shape, F32)],
        input_output_aliases=aliases,
        compiler_params=_cparams(1),
        name="gla_sample",
    )(*args)


def _no_state_in(kern, pos, *refs):
    return kern(*refs[:pos], None, *refs[pos:])


def _gdn_gates(ps, alog_ref, dtb_ref):
    g = -jnp.exp(alog_ref[...]) * _softplus(ps + dtb_ref[...])
    return g, _sigmoid(ps)


def _l2norm(x):
    return x * lax.rsqrt(jnp.sum(x * x, axis=-1, keepdims=True) + EPS)


def _inv_unit_lower(lows):
    n = lows[0].shape[0]
    r = _iota2((n, n), 0)
    c = _iota2((n, n), 1)
    eye = (r == c).astype(F32)
    base = 8
    same = lambda size: jnp.right_shift(r, size.bit_length() - 1) == jnp.right_shift(c, size.bit_length() - 1)
    bf = lambda xs: [x.astype(BF16) for x in xs]
    in_base = same(base)
    l8 = bf([jnp.where(in_base, low, 0.0) for low in lows])
    x = [eye - a.astype(F32) for a in l8]
    p = [_dot(a, a) for a in l8]
    for step in range(2):
        pb = bf(p)
        x = [xi + _dot(xb, pi) for xi, xb, pi in zip(x, bf(x), pb)]
        if step == 0:
            p = [_dot(pi, pi) for pi in pb]
    m = base
    while m < n:
        in_2m, in_m = same(2 * m), same(m)
        off = bf([jnp.where(in_2m, jnp.where(in_m, 0.0, low), 0.0) for low in lows])
        xb = bf(x)
        y = bf([_dot(oi, xi) for oi, xi in zip(off, xb)])
        x = [xi - _dot(xbi, yi) for xi, xbi, yi in zip(x, xb, y)]
        m *= 2
    return x


def _gdn_prompt_kernel(pd_ref, ps_ref, psn_ref, cw_ref, alog_ref, dtb_ref, go_ref, o_in_ref, o_ref, st_ref,
                       xb_ref, s_ref, gsc_ref, *, dk, dv):
    del o_in_ref
    c = pl.program_id(1)
    C = CHUNK
    R = STEP_CHUNKS * C
    kw = GDN_HEADS * dk
    conv_dim = 2 * kw + GDN_HEADS * dv
    halo = 8

    def chunk_log_decay(small_ref):
        g_all, _ = _gdn_gates(small_ref[...], alog_ref, dtb_ref)
        tri = (_iota2((C, C), 0) >= _iota2((C, C), 1)).astype(F32)
        g_all = g_all * LOG2E
        return jnp.concatenate([_dot_exact(tri, g_all[cc * C:(cc + 1) * C]) for cc in range(STEP_CHUNKS)], axis=0)

    @pl.when(c == 0)
    def _():
        xb_ref[0:halo, :] = jnp.zeros((halo, conv_dim), F32)
        s_ref[...] = jnp.zeros_like(s_ref)
        gsc_ref[...] = chunk_log_decay(ps_ref)

    xb_ref[halo:halo + R, :] = pd_ref[:, 0:conv_dim]
    window = xb_ref[...]
    conv = window[halo:halo + R] * cw_ref[CONV_W - 1:CONV_W, :]
    for back in range(1, CONV_W):
        tap = pltpu.roll(window, back, axis=0)[halo:halo + R]
        conv = conv + tap * cw_ref[CONV_W - 1 - back:CONV_W - back, :]
    conv = _silu(conv)
    xb_ref[0:halo, :] = window[R:R + halo]

    beta_all = _sigmoid(ps_ref[...])
    gc_all = gsc_ref[...]
    gr_all = [gc_all[cc * C:(cc + 1) * C].T for cc in range(STEP_CHUNKS)]
    r_i = _iota2((C, C), 0)
    c_i = _iota2((C, C), 1)
    incl = r_i >= c_i
    strict = r_i > c_i
    g_out = go_ref[...]

    units = [(cc, h) for cc in range(STEP_CHUNKS) for h in range(GDN_HEADS)]
    ids = range(len(units))
    rows = lambda cc: slice(cc * C, (cc + 1) * C)
    col = lambda base, width, h: slice(base + h * width, base + (h + 1) * width)
    bf = lambda xs: [x.astype(BF16) for x in xs]
    q = [_l2norm(conv[rows(cc), col(0, dk, h)]) * (dk ** -0.5) for cc, h in units]
    k = [_l2norm(conv[rows(cc), col(kw, dk, h)]) for cc, h in units]
    v = [conv[rows(cc), col(2 * kw, dv, h)] for cc, h in units]
    beta = [beta_all[rows(cc), DB_LANE + h: DB_LANE + h + 1] for cc, h in units]
    gc = [gc_all[rows(cc), DA_LANE + h: DA_LANE + h + 1] for cc, h in units]
    g_last = [g[C - 1:C, :] for g in gc]
    decay = [jnp.exp2(jnp.where(incl, gc[u] - gr_all[cc][DA_LANE + h: DA_LANE + h + 1, :], MASKED_LOG))
             for u, (cc, h) in enumerate(units)]
    kb = [k[u] * beta[u] for u in ids]
    kbf, qbf = bf(k), bf(q)
    kk = [_dot_nt(a, b) for a, b in zip(bf(kb), kbf)]
    qk = [_dot_nt(a, b) for a, b in zip(qbf, kbf)]
    t_inv = bf(_inv_unit_lower([jnp.where(strict, kk[u] * decay[u], 0.0) for u in ids]))
    rhs = bf([jnp.concatenate([v[u] * beta[u], kb[u] * jnp.exp2(gc[u])], axis=-1) for u in ids])
    sol = [_dot(t, x) for t, x in zip(t_inv, rhs)]
    wbf = bf([sol[u][:, dv:dv + dk] for u in ids])
    q_dec = bf([q[u] * jnp.exp2(gc[u]) for u in ids])
    k_dec = bf([k[u] * jnp.exp2(g_last[u] - gc[u]) for u in ids])
    qkd = bf([qk[u] * decay[u] for u in ids])

    s = [s_ref[h] for h in range(GDN_HEADS)]
    o = [None] * len(units)
    for cc in range(STEP_CHUNKS):
        us = [cc * GDN_HEADS + h for h in range(GDN_HEADS)]
        sb = bf(s)
        ws = [_dot(wbf[u], sb[h]) for h, u in enumerate(us)]
        oc = [_dot(q_dec[u], sb[h]) for h, u in enumerate(us)]
        uu = bf([sol[u][:, 0:dv] - ws[h] for h, u in enumerate(us)])
        for h, u in enumerate(us):
            o[u] = oc[h] + _dot(qkd[u], uu[h])
        s_new = [_dot_tn(k_dec[u], uu[h]) for h, u in enumerate(us)]
        s = [jnp.exp2(g_last[u]) * s[h] + s_new[h] for h, u in enumerate(us)]
    for h in range(GDN_HEADS):
        s_ref[h] = s[h]
    for u, (cc, h) in enumerate(units):
        z = pd_ref[rows(cc), col(conv_dim, dv, h)]
        o_ref[rows(cc), col(0, dv, h)] = (_rms(o[u], g_out) * _silu(z)).astype(o_ref.dtype)

    gsc_ref[...] = chunk_log_decay(psn_ref)

    @pl.when(c == pl.num_programs(1) - 1)
    def _():
        st_ref[0] = s_ref[...]


def gdn_prompt(pd, ps, cw_3, alog_3, dtb_3, go_3, layer, o_all, *, batch, seq, col0):
    m, pd_cols = pd.shape
    dv = go_3.shape[2]
    dk = dv
    width = GDN_HEADS * dv
    conv_dim = cw_3.shape[2]
    step_rows = STEP_CHUNKS * CHUNK
    assert seq % step_rows == 0
    nc = seq // step_rows
    row = lambda b, c: (b * nc + c, 0)
    cb = col0 // width
    lane_spec = pl.BlockSpec((None, 1, SMALL_COLS), lambda b, c: (layer, 0, 0))
    return pl.pallas_call(
        functools.partial(_gdn_prompt_kernel, dk=dk, dv=dv),
        grid=(batch, nc),
        in_specs=[pl.BlockSpec((step_rows, pd_cols), row),
                  pl.BlockSpec((step_rows, SMALL_COLS), row),
                  pl.BlockSpec((step_rows, SMALL_COLS), lambda b, c: (b * nc + jnp.minimum(c + 1, nc - 1), 0)),
                  pl.BlockSpec((None, CONV_W, conv_dim), lambda b, c: (layer, 0, 0)),
                  lane_spec, lane_spec,
                  pl.BlockSpec((None, 1, dv), lambda b, c: (layer, 0, 0)),
                  pl.BlockSpec(memory_space=pl.ANY)],
        out_specs=[pl.BlockSpec((step_rows, width), lambda b, c: (b * nc + c, cb)),
                   pl.BlockSpec((1, GDN_HEADS, dk, dv), lambda b, c: (b, 0, 0, 0))],
        out_shape=[jax.ShapeDtypeStruct(o_all.shape, BF16),
                   jax.ShapeDtypeStruct((batch, GDN_HEADS, dk, dv), F32)],
        scratch_shapes=[pltpu.VMEM((step_rows + 8, conv_dim), F32),
                        pltpu.VMEM((GDN_HEADS, dk, dv), F32),
                        pltpu.VMEM((step_rows, SMALL_COLS), F32)],
        input_output_aliases={7: 0},
        compiler_params=_cparams(2),
        name="gdn_prompt",
    )(pd, ps, ps, cw_3, alog_3, dtb_3, go_3, o_all)


def _gdn_sample_kernel(pd_ref, ps_ref, cw_ref, alog_ref, dtb_ref, go_ref, cs_ref, st_ref,
                       o_in_ref, cs_in_ref, st_in_ref, o_ref, cso_ref, sto_ref, *, dk, dv):
    del o_in_ref, cs_in_ref, st_in_ref
    kw = GDN_HEADS * dk
    conv_dim = 2 * kw + GDN_HEADS * dv
    new = pd_ref[:, 0:conv_dim]
    conv = new * cw_ref[CONV_W - 1:CONV_W, :]
    for j in range(CONV_W - 1):
        conv = conv + cs_ref[j] * cw_ref[j:j + 1, :]
    conv = _silu(conv)
    for j in range(CONV_W - 2):
        cso_ref[j] = cs_ref[j + 1]
    cso_ref[CONV_W - 2] = new

    g_all, beta_all = _gdn_gates(ps_ref[...], alog_ref, dtb_ref)
    a_all = jnp.exp(g_all)
    g_out = go_ref[...]
    for h in range(GDN_HEADS):
        tiles = _column_tiles([_l2norm(conv[:, h * dk:(h + 1) * dk]) * (dk ** -0.5),
                               _l2norm(conv[:, kw + h * dk: kw + (h + 1) * dk])])
        v = conv[:, 2 * kw + h * dv: 2 * kw + (h + 1) * dv]
        z = pd_ref[:, conv_dim + h * dv: conv_dim + (h + 1) * dv]
        rows = []
        for s in range(MIX_ROWS):
            a = a_all[s:s + 1, DA_LANE + h: DA_LANE + h + 1]
            beta = beta_all[s:s + 1, DB_LANE + h: DB_LANE + h + 1]
            q_t, k_t = tiles(s)
            st = st_ref[s, h] * a
            u = beta * (v[s:s + 1, :] - jnp.sum(k_t * st, axis=0, keepdims=True))
            st = st + k_t * u
            sto_ref[s, h] = st
            rows.append(jnp.sum(q_t * st, axis=0, keepdims=True))
        o = jnp.concatenate(rows, axis=0)
        o_ref[:, h * dv:(h + 1) * dv] = (_rms(o, g_out) * _silu(z)).astype(o_ref.dtype)


def gdn_sample(pd, ps, cw_3, alog_3, dtb_3, go_3, conv3, state5, layer, o_all, cs_all, st_all,
               *, row0, n_samples, col0):
    m, pd_cols = pd.shape
    dv = go_3.shape[2]
    dk = dv
    width = GDN_HEADS * dv
    conv_dim = cw_3.shape[2]
    blk0 = row0 // MIX_ROWS
    row = lambda i: (blk0 + i, 0)
    cb = col0 // width
    lane_spec = pl.BlockSpec((None, 1, SMALL_COLS), lambda i: (layer, 0, 0))
    cs_spec = pl.BlockSpec((None, CONV_W - 1, MIX_ROWS, conv_dim), lambda i: (layer, 0, i, 0))
    st_spec = pl.BlockSpec((None, MIX_ROWS, GDN_HEADS, dk, dv), lambda i: (layer, i, 0, 0, 0))
    any_spec = pl.BlockSpec(memory_space=pl.ANY)
    in_specs = [pl.BlockSpec((MIX_ROWS, pd_cols), row),
                pl.BlockSpec((MIX_ROWS, SMALL_COLS), row),
                pl.BlockSpec((None, CONV_W, conv_dim), lambda i: (layer, 0, 0)),
                lane_spec, lane_spec,
                pl.BlockSpec((None, 1, dv), lambda i: (layer, 0, 0)),
                cs_spec, st_spec, any_spec]
    args = [pd, ps, cw_3, alog_3, dtb_3, go_3, conv3, state5, o_all]
    aliases = {8: 0}
    kern = functools.partial(_gdn_sample_kernel, dk=dk, dv=dv)
    if st_all is not None:
        in_specs += [any_spec, any_spec]
        args += [cs_all, st_all]
        aliases[9] = 1
        aliases[10] = 2
    else:
        kern = functools.partial(_no_state_in, functools.partial(_no_state_in, kern, 9), 9)
    return pl.pallas_call(
        kern,
        grid=(n_samples // MIX_ROWS,),
        in_specs=in_specs,
        out_specs=[pl.BlockSpec((MIX_ROWS, width), lambda i: (blk0 + i, cb)), cs_spec, st_spec],
        out_shape=[jax.ShapeDtypeStruct(o_all.shape, BF16),
                   jax.ShapeDtypeStruct(conv3.shape, F32),
                   jax.ShapeDtypeStruct(state5.shape, F32)],
        input_output_aliases=aliases,
        compiler_params=_cparams(1),
        name="gdn_sample",
    )(*args)


def _row_tile(m, cap):
    best = 16
    for t in range(16, cap + 1, 16):
        if m % t == 0:
            best = t
    return best


def kernel(x_prompt, x_sample, mem_prompt, cache_mem_k, cache_mem_v, state_gla, state_gdn, state_conv, g_mix, w_in, gla_w_gate2, gla_b_gate, gla_g_out, gdn_conv_w, gdn_a_log, gdn_dt_bias, gdn_g_out, w_out, g_xattn, g_mem, xa_w_q, xa_w_k, xa_w_v, xa_w_o, g_ffn, ffn_w_gate, ffn_w_up, ffn_w_down, g_final):
    batch, seq, d = x_prompt.shape
    n_s = x_sample.shape[0]
    depth = w_in.shape[0]
    n_mem = mem_prompt.shape[1]
    mp = batch * seq
    m = mp + n_s
    gla_kw = gla_w_gate2.shape[2]
    gla_dv = gla_g_out.shape[1]
    gla_width = GLA_HEADS * gla_dv
    gdn_dv = gdn_g_out.shape[1]
    gdn_width = GDN_HEADS * gdn_dv
    conv_dim = gdn_conv_w.shape[2]
    n_gla = 2 * gla_kw + 2 * gla_width
    c_code = n_gla
    c_conv = c_code + GLA_GATE_RANK
    n_gdn = conv_dim + gdn_width
    c_da = c_conv + n_gdn

    tm = _row_tile(m, 1088)
    tm_mem = _row_tile(batch * n_mem, 1024)

    row3 = lambda p: p.reshape(depth, 1, p.shape[-1])
    lane_row = lambda p: jnp.pad(p, ((0, 0), (DA_LANE, SMALL_COLS - DA_LANE - GDN_HEADS))).reshape(depth, 1, SMALL_COLS)
    g_mix3, g_mem_col = row3(g_mix), g_mem.reshape(depth, d, 1)
    g_xattn_col, g_ffn_col = g_xattn.reshape(depth, d, 1), g_ffn.reshape(depth, d, 1)
    g_final3 = g_final.reshape(1, 1, d)
    bg3, gla_go3, gdn_go3 = row3(gla_b_gate), row3(gla_g_out), row3(gdn_g_out)
    alog3, dtb3 = lane_row(gdn_a_log), lane_row(gdn_dt_bias)

    w_in_t = jnp.swapaxes(w_in, 1, 2)

    conv3 = jnp.swapaxes(state_conv, 1, 2)
    _, mem_b, mem_ss = norm_inputs(mem_prompt.reshape(batch * n_mem, d), batch * n_mem, 0, tm_mem)
    cache_k4 = cache_token_rows(cache_mem_k)
    cache_v4 = cache_token_rows(cache_mem_v)

    filled = norm_inputs(x_prompt.reshape(mp, d), m, 0, _row_tile(mp, 1024))
    x, xb, ss = norm_inputs(x_sample.reshape(n_s, d), m, mp, n_s, filled)

    p_gla, p_gdn, p_conv = [], [], []
    s_gla_all = s_gdn_all = s_conv_all = mk_all = mv_all = None
    for l in range(depth):
        mix = (ss, g_mix3)
        pg = matmul(xb, w_in_t, l, tm=tm, tn=1024, col0=0, n_cols=n_gla, transposed=True, norm=mix, name="proj_gla")
        pd = matmul(xb, w_in_t, l, tm=tm, tn=1024, col0=c_conv, n_cols=n_gdn, transposed=True, norm=mix,
                    name="proj_gdn")
        ps = proj_small(xb, w_in_t, l, tm=tm, row_a=c_code, row_b=c_da, norm=mix)

        o, sa = gla_prompt(pg, ps, gla_w_gate2, bg3, gla_go3, l, batch=batch, seq=seq, d_model=d)
        o, sb = gdn_prompt(pd, ps, gdn_conv_w, alog3, dtb3, gdn_go3, l, o, batch=batch, seq=seq, col0=gla_width)
        o, s_gla_all = gla_sample(pg, ps, gla_w_gate2, bg3, gla_go3, state_gla, l, o, s_gla_all,
                                  row0=mp, n_samples=n_s)
        o, s_conv_all, s_gdn_all = gdn_sample(pd, ps, gdn_conv_w, alog3, dtb3, gdn_go3, conv3, state_gdn, l,
                                              o, s_conv_all, s_gdn_all, row0=mp, n_samples=n_s, col0=gla_width)
        p_gla.append(sa)
        p_gdn.append(sb)
        p_conv.append(jnp.stack([pd[(b + 1) * seq - (CONV_W - 1):(b + 1) * seq, :conv_dim] for b in range(batch)]))
        x, xb, ss = matmul(o, w_out, l, tm=tm, tn=1024, res=x, emit_norm_inputs=True, name="w_out")

        mk, mk_all = memory_proj(mem_b, xa_w_k, l, mk_all, tn=1024, norm=(mem_ss, g_mem_col), name="mem_k")
        mv, mv_all = memory_proj(mem_b, xa_w_v, l, mv_all, tn=1024, norm=(mem_ss, g_mem_col), name="mem_v")
        qx = matmul(xb, xa_w_q, l, tm=tm, tn=1024, out_dtype=BF16, norm=(ss, g_xattn_col), name="xa_q")
        ox = xattn_prompt(qx, mk, mv, batch=batch, seq=seq, n_mem=n_mem, tq=min(seq, 1024))
        ox = xattn_sample(qx, cache_k4, cache_v4, l, ox, row0=mp, n_samples=n_s)
        x, xb, ss = matmul(ox, xa_w_o, l, tm=tm, tn=1024, res=x, emit_norm_inputs=True, name="xa_o")

        ff, w_down = swiglu(xb, ffn_w_gate, ffn_w_up, ffn_w_down, l, tm=tm, tn=512, norm=(ss, g_ffn_col))
        w_down = w_down[None]
        if l + 1 < depth:
            x, xb, ss = matmul(ff, w_down, 0, tm=tm, tn=512, res=x, emit_norm_inputs=True, name="ffn_down")
        else:
            x = matmul(ff, w_down, 0, tm=tm, tn=512, res=x, name="ffn_down")

    y_prompt = rmsnorm_rows(x, g_final3, 0, F32, _row_tile(mp, 1024), row0=0, n_rows=mp)
    y_sample = rmsnorm_rows(x, g_final3, 0, F32, n_s, row0=mp, n_rows=n_s)
    hd = d // XA_HEADS

    def from_token_rows(rows_all):
        c6 = rows_all.reshape(depth, batch, n_mem, hd // LANES, XA_HEADS, LANES)
        return jnp.transpose(c6, (0, 1, 2, 4, 3, 5)).reshape(depth, batch, n_mem, XA_HEADS, hd)

    return (y_prompt.reshape(batch, seq, d),
            y_sample.reshape(n_s, 1, d),
            jnp.stack(p_gla),
            jnp.stack(p_gdn),
            jnp.stack(p_conv),
            from_token_rows(mk_all),
            from_token_rows(mv_all),
            s_gla_all,
            s_gdn_all,
            jnp.swapaxes(s_conv_all, 1, 2))
```
